```python
import jax, jax.numpy as jnp
from jax import lax
import numpy as np

D_MODEL = 1024
BATCH = 2
SEQ = 8192
DEPTH = 2
DEC_BATCH = 128
DEC_SEQ = 1
PAST_LEN = 2048
PAGE_SIZE = 128

RET_HEADS = 4
RET_DK = 128
RET_DV = 256
RET_CHUNK = 128
RET_QK = RET_HEADS * RET_DK
RET_V = RET_HEADS * RET_DV
ATT_GROUPS = ((128, 1), (512, 4), (2048, 16))
N_GROUPS = 3
HEADS_PER_GROUP = 4
ATT_HEADS = N_GROUPS * HEADS_PER_GROUP
HEAD_DIM = 64
ATT_BLOCK = 128
ATT_W = ATT_HEADS * HEAD_DIM
ATT_OUT = HEADS_PER_GROUP * HEAD_DIM
D_FF = -(-8 * D_MODEL // (3 * 256)) * 256
COL_SIZES = (RET_QK, RET_QK, RET_V, RET_V, ATT_W, ATT_W, ATT_W, D_MODEL, D_MODEL)
D_IN = 2 * RET_QK + 2 * RET_V + 3 * ATT_W + 2 * D_MODEL
RMS_EPS = 1e-6
GN_EPS = 1e-5

kernel_name = 'retention_dilated_attn_hybrid_step'


def _rms_norm(x, g):
    x32 = x.astype(jnp.float32)
    y = x32 * lax.rsqrt(jnp.mean(x32 * x32, axis=-1, keepdims=True) + RMS_EPS)
    return (y * g.astype(jnp.float32)).astype(x.dtype)


def _log_gamma():
    return jnp.log1p(-jnp.exp2(-5.0 - jnp.arange(RET_HEADS, dtype=jnp.float32)))


def _alibi_slopes():
    return jnp.exp2(-8.0 * (jnp.arange(ATT_HEADS, dtype=jnp.float32) + 1.0) / ATT_HEADS)


def _retention_chunk(q, k, v, s0, log_gamma):
    c = q.shape[1]
    idx = jnp.arange(c, dtype=jnp.float32)
    diff = idx[:, None] - idx[None, :]
    decay = jnp.where(diff >= 0, jnp.exp(jnp.maximum(diff, 0.0)[None] * log_gamma[:, None, None]), 0.0)
    scores = jnp.einsum('nihd,njhd->nhij', q, k) * decay[None]
    intra = jnp.einsum('nhij,njhv->nihv', scores, v)
    inner = jnp.exp((idx + 1.0)[:, None] * log_gamma[None, :])
    cross = jnp.einsum('nihd,nhdv->nihv', q, s0) * inner[None, :, :, None]
    tail = jnp.exp((c - 1.0 - idx)[:, None] * log_gamma[None, :])
    s1 = s0 * jnp.exp(c * log_gamma)[None, :, None, None] + jnp.einsum(
        'njhd,njhv->nhdv', k * tail[None, :, :, None], v)
    return intra + cross, s1


def _retention(q, k, v, s0):
    n, t = q.shape[:2]
    chunk = RET_CHUNK if t % RET_CHUNK == 0 else t
    nc = t // chunk
    log_gamma = _log_gamma()

    def split(a):
        return a.reshape(n, nc, chunk, *a.shape[2:]).swapaxes(0, 1)

    def step(s, inp):
        o, s = _retention_chunk(inp[0], inp[1], inp[2], s, log_gamma)
        return s, o

    s_final, o = lax.scan(step, s0, (split(q), split(k), split(v)))
    o = o.swapaxes(0, 1).reshape(n, t, RET_HEADS, RET_DV)
    return o, s_final


def _dilated_group_prompt(q, k, v, slopes, window, dilation):
    b, t, h, dh = q.shape
    span = dilation * ATT_BLOCK
    tp = -(-t // span) * span
    lres = tp // dilation
    nb = lres // ATT_BLOCK

    def arrange(a):
        a = jnp.pad(a, ((0, 0), (0, tp - t), (0, 0), (0, 0)))
        a = a.reshape(b, lres, dilation, h, dh).swapaxes(1, 2)
        return a.reshape(b, dilation, nb, ATT_BLOCK, h, dh)

    def with_prev(a):
        prev = jnp.pad(a, ((0, 0), (0, 0), (1, 0), (0, 0), (0, 0), (0, 0)))[:, :, :-1]
        return jnp.concatenate([prev, a], axis=3)

    qb = arrange(q)
    kk = with_prev(arrange(k))
    vv = with_prev(arrange(v))
    steps = window // dilation
    qi = jnp.arange(ATT_BLOCK)[:, None]
    kj = jnp.arange(2 * ATT_BLOCK)[None, :]
    dist = qi + ATT_BLOCK - kj
    blk = jnp.arange(nb)[:, None, None]
    valid = ((dist >= 0) & (dist <= steps))[None] & ((blk > 0) | (kj >= ATT_BLOCK)[None])
    s = jnp.einsum('brnihd,brnjhd->brnhij', qb, kk) * (HEAD_DIM ** -0.5)
    s = s - slopes[:, None, None] * (dist * dilation).astype(jnp.float32)[None]
    s = jnp.where(valid[None, None, :, None], s, -jnp.inf)
    m = jnp.max(s, axis=-1, keepdims=True)
    p = jnp.exp(s - m)
    l = jnp.sum(p, axis=-1, keepdims=True)
    o = jnp.einsum('brnhij,brnjhd->brnihd', p / l, vv)
    lse = (m + jnp.log(l))[..., 0].swapaxes(3, 4)
    o = o.reshape(b, dilation, lres, h, dh).swapaxes(1, 2).reshape(b, tp, h, dh)[:, :t]
    lse = lse.reshape(b, dilation, lres, h).swapaxes(1, 2).reshape(b, tp, h)[:, :t]
    return o, lse


def _dilated_group_sample(q, k_new, v_new, kv_buf, slopes, window, dilation):
    wlen = kv_buf.shape[1]
    s_new = q.shape[1]
    buf = kv_buf.astype(jnp.float32)
    keys = jnp.concatenate([buf[:, :, 0], k_new], axis=1)
    vals = jnp.concatenate([buf[:, :, 1], v_new], axis=1)
    steps = window // dilation
    j = jnp.arange(steps + 1)
    idx = wlen + jnp.arange(s_new)[:, None] - dilation * j[None, :]
    valid = idx >= 0
    idx = jnp.clip(idx, 0)
    kg = jnp.take(keys, idx, axis=1)
    vg = jnp.take(vals, idx, axis=1)
    s = jnp.einsum('nshd,nsjhd->nshj', q, kg) * (HEAD_DIM ** -0.5)
    s = s - slopes[:, None] * (dilation * j).astype(jnp.float32)[None, :]
    s = jnp.where(valid[None, :, None, :], s, -jnp.inf)
    lse = jax.nn.logsumexp(s, axis=-1)
    p = jnp.exp(s - lse[..., None])
    o = jnp.einsum('nshj,nsjhd->nshd', p, vg)
    return o, lse


def _token_mix(hn, w_in, w_ret_branch, w_att_branch, w_out, ret_s0, kv_bufs):
    f32 = jnp.float32
    n, t = hn.shape[:2]
    proj = hn @ w_in
    rq, rk, rv, rg, aq, ak, av, gr, ga = jnp.split(proj, np.cumsum(COL_SIZES)[:-1].tolist(), axis=-1)
    rq = rq.reshape(n, t, RET_HEADS, RET_DK).astype(f32) * (RET_DK ** -0.5)
    rk = rk.reshape(n, t, RET_HEADS, RET_DK).astype(f32)
    rv = rv.reshape(n, t, RET_HEADS, RET_DV).astype(f32)
    ro, ret_s1 = _retention(rq, rk, rv, ret_s0)
    mu = jnp.mean(ro, axis=-1, keepdims=True)
    var = jnp.mean(jnp.square(ro - mu), axis=-1, keepdims=True)
    ro = (ro - mu) * lax.rsqrt(var + GN_EPS)
    ro = (ro.reshape(n, t, RET_V) * jax.nn.silu(rg.astype(f32))).astype(hn.dtype)
    r_branch = ro @ w_ret_branch
    aq = aq.reshape(n, t, N_GROUPS, HEADS_PER_GROUP, HEAD_DIM)
    ak = ak.reshape(n, t, N_GROUPS, HEADS_PER_GROUP, HEAD_DIM)
    av = av.reshape(n, t, N_GROUPS, HEADS_PER_GROUP, HEAD_DIM)
    slopes = _alibi_slopes().reshape(N_GROUPS, HEADS_PER_GROUP)
    outs, lses, new_kv = [], [], []
    for g, (win, dil) in enumerate(ATT_GROUPS):
        qg = aq[:, :, g].astype(f32)
        kg = ak[:, :, g].astype(f32)
        vg = av[:, :, g].astype(f32)
        if kv_bufs is None:
            o, lse = _dilated_group_prompt(qg, kg, vg, slopes[g], win, dil)
            keep = min(win, t)
            new_kv.append(jnp.stack([ak[:, t - keep:, g], av[:, t - keep:, g]], axis=2))
        else:
            o, lse = _dilated_group_sample(qg, kg, vg, kv_bufs[g], slopes[g], win, dil)
            new_kv.append(jnp.stack([ak[:, :, g], av[:, :, g]], axis=2))
        outs.append(o)
        lses.append(lse)
    wgt = jax.nn.softmax(jnp.stack(lses, axis=0), axis=0)
    ao = jnp.einsum('gnth,gnthd->nthd', wgt, jnp.stack(outs, axis=0))
    a_branch = ao.reshape(n, t, ATT_OUT).astype(hn.dtype) @ w_att_branch
    merged = jax.nn.sigmoid(gr) * r_branch + jax.nn.sigmoid(ga) * a_branch
    return merged @ w_out, ret_s1, new_kv


def _trunk(x, ret_states, kv_caches, norm_mix, w_in, w_ret_branch, w_att_branch, w_out,
           norm_ffn, w_gate_up, w_down, norm_final):
    n = x.shape[0]
    rets, kvs = [], []
    for l in range(DEPTH):
        if ret_states is None:
            s0 = jnp.zeros((n, RET_HEADS, RET_DK, RET_DV), jnp.float32)
            bufs = None
        else:
            s0 = ret_states[l].astype(jnp.float32)
            bufs = [c[l] for c in kv_caches]
        hn = _rms_norm(x, norm_mix[l])
        y, s1, kv = _token_mix(hn, w_in[l], w_ret_branch[l], w_att_branch[l], w_out[l], s0, bufs)
        x = x + y
        hn = _rms_norm(x, norm_ffn[l])
        gate, up = jnp.split(hn @ w_gate_up[l], 2, axis=-1)
        x = x + (jax.nn.silu(gate) * up) @ w_down[l]
        rets.append(s1.astype(x.dtype))
        kvs.append(kv)
    y = _rms_norm(x, norm_final)
    ret_out = jnp.stack(rets, axis=0)
    kv_out = [jnp.stack([kvs[l][g] for l in range(DEPTH)], axis=0) for g in range(N_GROUPS)]
    return y, ret_out, kv_out


def setup_inputs(seed: int = 0) -> dict:
    key = jax.random.key(seed)
    ks = jax.random.split(key, 16)
    f32 = jnp.float32
    nrm = jax.random.normal
    wl = [min(w, PAST_LEN) for (w, _) in ATT_GROUPS]
    return {
        'x_prompt': nrm(ks[0], (BATCH, SEQ, D_MODEL), f32),
        'x_sample': nrm(ks[1], (DEC_BATCH, DEC_SEQ, D_MODEL), f32),
        'state_ret': nrm(ks[2], (DEPTH, DEC_BATCH, RET_HEADS, RET_DK, RET_DV), f32),
        'cache_kv_w128': nrm(ks[3], (DEPTH, DEC_BATCH, wl[0], 2, HEADS_PER_GROUP, HEAD_DIM), f32),
        'cache_kv_w512': nrm(ks[4], (DEPTH, DEC_BATCH, wl[1], 2, HEADS_PER_GROUP, HEAD_DIM), f32),
        'cache_kv_w2048': nrm(ks[5], (DEPTH, DEC_BATCH, wl[2], 2, HEADS_PER_GROUP, HEAD_DIM), f32),
        'norm_mix': 1.0 + 0.01 * nrm(ks[6], (DEPTH, D_MODEL), f32),
        'w_in': nrm(ks[7], (DEPTH, D_MODEL, D_IN), f32) * D_MODEL ** -0.5,
        'w_ret_branch': nrm(ks[8], (DEPTH, RET_V, D_MODEL), f32) * RET_V ** -0.5,
        'w_att_branch': nrm(ks[9], (DEPTH, ATT_OUT, D_MODEL), f32) * ATT_OUT ** -0.5,
        'w_out': nrm(ks[10], (DEPTH, D_MODEL, D_MODEL), f32) * D_MODEL ** -0.5,
        'norm_ffn': 1.0 + 0.01 * nrm(ks[11], (DEPTH, D_MODEL), f32),
        'w_gate_up': nrm(ks[12], (DEPTH, D_MODEL, 2 * D_FF), f32) * D_MODEL ** -0.5,
        'w_down': nrm(ks[13], (DEPTH, D_FF, D_MODEL), f32) * D_FF ** -0.5,
        'norm_final': 1.0 + 0.01 * nrm(ks[14], (D_MODEL,), f32),
    }


def reference(x_prompt, x_sample, state_ret, cache_kv_w128, cache_kv_w512, cache_kv_w2048,
              norm_mix, w_in, w_ret_branch, w_att_branch, w_out, norm_ffn, w_gate_up, w_down,
              norm_final):
    y_prompt, p_ret, p_kv = _trunk(x_prompt, None, None, norm_mix, w_in, w_ret_branch,
                                   w_att_branch, w_out, norm_ffn, w_gate_up, w_down, norm_final)
    y_sample, s_ret, s_kv = _trunk(x_sample, state_ret, (cache_kv_w128, cache_kv_w512, cache_kv_w2048),
                                   norm_mix, w_in, w_ret_branch, w_att_branch, w_out, norm_ffn,
                                   w_gate_up, w_down, norm_final)
    return (y_prompt, y_sample, p_ret, p_kv[0], p_kv[1], p_kv[2], s_ret, s_kv[0], s_kv[1], s_kv[2])
```

```python
import functools

import jax
import jax.numpy as jnp
from jax import lax
from jax.experimental import pallas as pl
from jax.experimental.pallas import tpu as pltpu

F32 = jnp.float32
BF16 = jnp.bfloat16

D_MODEL = 1024
RET_HEADS = 4
RET_DK = 128
RET_DV = 256
RET_CHUNK = 128
ATT_GROUPS = ((128, 1), (512, 4), (2048, 16))
N_GROUPS = 3
HEADS_PER_GROUP = 4
ATT_HEADS = N_GROUPS * HEADS_PER_GROUP
HEAD_DIM = 64
ATT_BLOCK = 128
ATT_OUT = HEADS_PER_GROUP * HEAD_DIM
D_FF = 2816
D_IN = 7424
RMS_EPS = 1e-6
GN_EPS = 1e-5

COL = 256
N_COL_BLOCKS = D_IN // COL
FF_BLOCKS = D_FF // COL
RQ, RK, RV, RG, GR, GA, AQ, AK, AV = 0, 2, 4, 8, 12, 16, 20, 23, 26
MASKED = -1e30
VMEM_LIMIT = 56 * 1024 * 1024


def _params(n_axes):
    return pltpu.CompilerParams(dimension_semantics=("arbitrary",) * n_axes,
                                vmem_limit_bytes=VMEM_LIMIT)


def _dot(a, b):
    return jnp.dot(a, b, preferred_element_type=F32)


def _dot_nt(a, b):
    return lax.dot_general(a, b, (((1,), (1,)), ((), ())), preferred_element_type=F32)


def _dot_tn(a, b):
    return lax.dot_general(a, b, (((0,), (0,)), ((), ())), preferred_element_type=F32)


def _rms(x, g):
    return x * lax.rsqrt(jnp.mean(x * x, axis=-1, keepdims=True) + RMS_EPS) * g


def _silu(x):
    return x * jax.nn.sigmoid(x)


def _group_norm(o):
    mu = jnp.mean(o, axis=-1, keepdims=True)
    d = o - mu
    var = jnp.mean(d * d, axis=-1, keepdims=True)
    return d * lax.rsqrt(var + GN_EPS)


def _resident(shape):
    zeros = (0,) * len(shape)
    return pl.BlockSpec(shape, lambda *_: zeros, pipeline_mode=pl.Buffered(1))


def _inproj_kernel(x_ref, g_ref, w_ref, o_ref, hn_ref):
    hn_ref[...] = _rms(x_ref[...], g_ref[...]).astype(BF16)

    def body(j, carry):
        o_ref[j] = _dot(hn_ref[...], w_ref[j]).astype(o_ref.dtype)
        return carry

    lax.fori_loop(0, N_COL_BLOCKS, body, 0)


def _in_proj(x, g, w3, bm, out_dtype):
    m = x.shape[0]
    return pl.pallas_call(
        _inproj_kernel,
        grid=(m // bm,),
        in_specs=[pl.BlockSpec((bm, D_MODEL), lambda i: (i, 0)),
                  _resident((1, D_MODEL)),
                  _resident((N_COL_BLOCKS, D_MODEL, COL))],
        out_specs=pl.BlockSpec((N_COL_BLOCKS, bm, COL), lambda i: (0, i, 0)),
        out_shape=jax.ShapeDtypeStruct((N_COL_BLOCKS, m, COL), out_dtype),
        scratch_shapes=[pltpu.VMEM((bm, D_MODEL), BF16)],
        compiler_params=_params(1),
        name="in_proj",
    )(x, g.reshape(1, D_MODEL), w3)


def _ret_prompt_kernel(q_ref, k_ref, v_ref, rg_ref, dmat_ref, inner_ref, tail_ref, gc_ref,
                       ro_ref, s_ref, *, batch):
    @pl.when(pl.program_id(0) == 0)
    def _():
        s_ref[...] = jnp.zeros_like(s_ref)

    for b in range(batch):
        for h in range(RET_HEADS):
            lo = (h % 2) * RET_DK
            q = q_ref[h // 2, b, :, lo:lo + RET_DK]
            k = k_ref[h // 2, b, :, lo:lo + RET_DK]
            v = v_ref[h, b]
            s0 = s_ref[b, h]
            scores = _dot_nt(q, k) * dmat_ref[h]
            o = _dot(scores.astype(BF16), v) + _dot(q, s0.astype(BF16)) * inner_ref[h]
            kt = (k.astype(F32) * tail_ref[h]).astype(BF16)
            s_ref[b, h] = s0 * gc_ref[h] + _dot_tn(kt, v)
            y = _group_norm(o) * _silu(rg_ref[h, b].astype(F32))
            ro_ref[h, b] = y.astype(ro_ref.dtype)


def _ret_tables(chunk):
    log_gamma = jnp.log1p(-jnp.exp2(-5.0 - jnp.arange(RET_HEADS, dtype=F32)))
    scale = RET_DK ** -0.5
    idx = jnp.arange(chunk, dtype=F32)
    diff = idx[:, None] - idx[None, :]
    dmat = jnp.where(diff >= 0, jnp.exp(jnp.maximum(diff, 0.0)[None] * log_gamma[:, None, None]), 0.0) * scale
    inner = jnp.exp((idx + 1.0)[None, :] * log_gamma[:, None]) * scale
    tail = jnp.exp((chunk - 1.0 - idx)[None, :] * log_gamma[:, None])
    gc = jnp.exp(chunk * log_gamma)
    inner = jnp.broadcast_to(inner[:, :, None], (RET_HEADS, chunk, RET_DV))
    tail = jnp.broadcast_to(tail[:, :, None], (RET_HEADS, chunk, RET_DK))
    gc = jnp.broadcast_to(gc[:, None, None], (RET_HEADS, 1, RET_DV))
    return dmat, inner, tail, gc


def _ret_prompt(proj4):
    _, b, t, _ = proj4.shape
    c = RET_CHUNK
    dmat, inner, tail, gc = _ret_tables(c)
    return pl.pallas_call(
        functools.partial(_ret_prompt_kernel, batch=b),
        grid=(t // c,),
        in_specs=[pl.BlockSpec((2, b, c, COL), lambda i: (RQ // 2, 0, i, 0)),
                  pl.BlockSpec((2, b, c, COL), lambda i: (RK // 2, 0, i, 0)),
                  pl.BlockSpec((4, b, c, COL), lambda i: (RV // 4, 0, i, 0)),
                  pl.BlockSpec((4, b, c, COL), lambda i: (RG // 4, 0, i, 0)),
                  _resident(dmat.shape), _resident(inner.shape), _resident(tail.shape),
                  _resident(gc.shape)],
        out_specs=[pl.BlockSpec((4, b, c, COL), lambda i: (0, 0, i, 0)),
                   pl.BlockSpec((b, RET_HEADS, RET_DK, RET_DV), lambda i: (0, 0, 0, 0))],
        out_shape=[jax.ShapeDtypeStruct((4, b, t, COL), BF16),
                   jax.ShapeDtypeStruct((b, RET_HEADS, RET_DK, RET_DV), F32)],
        compiler_params=_params(1),
        name="ret_prompt",
    )(proj4, proj4, proj4, proj4, dmat, inner, tail, gc)


def _attn_prompt_kernel(q_ref, kc_ref, kp_ref, vc_ref, vp_ref, bias_ref, o_ref, lse_ref):
    n = pl.program_id(2)
    lane_head = lax.broadcasted_iota(jnp.int32, (ATT_BLOCK, ATT_OUT), 1) // HEAD_DIM
    q = q_ref[...]
    qs = jnp.concatenate([jnp.where(lane_head == h, q, jnp.zeros_like(q))
                          for h in range(HEADS_PER_GROUP)], axis=0)
    kcat = jnp.concatenate([kp_ref[...], kc_ref[...]], axis=0)
    vcat = jnp.concatenate([vp_ref[...], vc_ref[...]], axis=0)
    s = _dot_nt(qs, kcat) * (HEAD_DIM ** -0.5) + bias_ref[jnp.minimum(n, 1)]
    m = jnp.max(s, axis=-1, keepdims=True)
    p = jnp.exp(s - m)
    l = jnp.sum(p, axis=-1, keepdims=True)
    on = _dot(p.astype(BF16), vcat) / l
    lse = m + jnp.log(l)
    o = jnp.zeros((ATT_BLOCK, ATT_OUT), F32)
    ls = jnp.zeros((ATT_BLOCK, ATT_OUT), F32)
    for h in range(HEADS_PER_GROUP):
        rows = slice(h * ATT_BLOCK, (h + 1) * ATT_BLOCK)
        o = jnp.where(lane_head == h, on[rows], o)
        ls = jnp.where(lane_head == h, lse[rows], ls)
    o_ref[...] = o.astype(o_ref.dtype)
    lse_ref[...] = ls


def _alibi_slopes():
    return jnp.exp2(-8.0 * (jnp.arange(ATT_HEADS, dtype=F32) + 1.0) / ATT_HEADS).reshape(
        N_GROUPS, HEADS_PER_GROUP)


def _attn_prompt_bias(g):
    window, dilation = ATT_GROUPS[g]
    steps = window // dilation
    qi = jnp.arange(ATT_BLOCK)[:, None]
    kj = jnp.arange(2 * ATT_BLOCK)[None, :]
    dist = qi + ATT_BLOCK - kj
    valid = (dist >= 0) & (dist <= steps)
    pen = -_alibi_slopes()[g][:, None, None] * (dist * dilation).astype(F32)[None]
    later = jnp.where(valid[None], pen, MASKED)
    first = jnp.where((valid & (kj >= ATT_BLOCK))[None], pen, MASKED)
    return jnp.stack([first, later]).reshape(2, HEADS_PER_GROUP * ATT_BLOCK, 2 * ATT_BLOCK)


def _attn_prompt(proj, g, batch):
    _, dil = ATT_GROUPS[g]
    m = proj.shape[1]
    rows = m // dil
    nb = rows // batch // ATT_BLOCK
    view = proj.reshape(N_COL_BLOCKS, rows, dil * COL)
    bias = _attn_prompt_bias(g)

    def cur(col):
        return pl.BlockSpec((None, ATT_BLOCK, COL), lambda b, r, n: (col + g, b * nb + n, r))

    def prev(col):
        return pl.BlockSpec((None, ATT_BLOCK, COL),
                            lambda b, r, n: (col + g, b * nb + jnp.maximum(n - 1, 0), r))

    out_spec = pl.BlockSpec((ATT_BLOCK, COL), lambda b, r, n: (b * nb + n, r))
    o, lse = pl.pallas_call(
        _attn_prompt_kernel,
        grid=(batch, dil, nb),
        in_specs=[cur(AQ), cur(AK), prev(AK), cur(AV), prev(AV), _resident(bias.shape)],
        out_specs=[out_spec, out_spec],
        out_shape=[jax.ShapeDtypeStruct((rows, dil * COL), BF16),
                   jax.ShapeDtypeStruct((rows, dil * COL), F32)],
        compiler_params=_params(3),
        name=f"attn_prompt_g{g}",
    )(view, view, view, view, view, bias)
    return o.reshape(m, COL), lse.reshape(m, COL)


def _mix_out_kernel(*refs, n_att):
    att = refs[:n_att]
    ro_ref, gr_ref, ga_ref, x_ref, wr_ref, wa_ref, wo_ref, o_ref = refs[n_att:]
    if n_att == 1:
        ao = att[0][...]
    else:
        outs, lses = att[:N_GROUPS], [r[...] for r in att[N_GROUPS:]]
        top = functools.reduce(jnp.maximum, lses)
        es = [jnp.exp(x - top) for x in lses]
        den = functools.reduce(jnp.add, es)
        ao = functools.reduce(jnp.add, [e * r[...].astype(F32) for e, r in zip(es, outs)]) / den
    a_br = _dot(ao.astype(BF16), wa_ref[...])
    r_br = functools.reduce(jnp.add, [_dot(ro_ref[h].astype(BF16), wr_ref[h])
                                      for h in range(RET_HEADS)])
    y = x_ref[...]
    for c in range(D_MODEL // COL):
        cols = slice(c * COL, (c + 1) * COL)
        merged = (jax.nn.sigmoid(gr_ref[c].astype(F32)) * r_br[:, cols]
                  + jax.nn.sigmoid(ga_ref[c].astype(F32)) * a_br[:, cols])
        y = y + _dot(merged.astype(BF16), wo_ref[c])
    o_ref[...] = y


def _mix_out(att, ro, proj, x, wr, wa, wo, bm):
    m = x.shape[0]
    row = pl.BlockSpec((bm, COL), lambda i: (i, 0))
    return pl.pallas_call(
        functools.partial(_mix_out_kernel, n_att=len(att)),
        grid=(m // bm,),
        in_specs=[row] * len(att) + [
            pl.BlockSpec((4, bm, COL), lambda i: (0, i, 0)),
            pl.BlockSpec((4, bm, COL), lambda i: (GR // 4, i, 0)),
            pl.BlockSpec((4, bm, COL), lambda i: (GA // 4, i, 0)),
            pl.BlockSpec((bm, D_MODEL), lambda i: (i, 0)),
            _resident(wr.shape), _resident(wa.shape), _resident(wo.shape)],
        out_specs=pl.BlockSpec((bm, D_MODEL), lambda i: (i, 0)),
        out_shape=jax.ShapeDtypeStruct((m, D_MODEL), F32),
        compiler_params=_params(1),
        name="mix_out",
    )(*att, ro, proj, proj, x, wr, wa, wo)


def _ffn_kernel(x_ref, g_ref, wg_ref, wu_ref, wd_ref, gf_ref, o_ref, hn_ref, acc_ref, *, final):
    x = x_ref[...]
    hn_ref[...] = _rms(x, g_ref[...]).astype(BF16)
    acc_ref[...] = x

    def body(c, carry):
        hn = hn_ref[...]
        act = _silu(_dot(hn, wg_ref[c])) * _dot(hn, wu_ref[c])
        acc_ref[...] += _dot(act.astype(BF16), wd_ref[c])
        return carry

    lax.fori_loop(0, FF_BLOCKS, body, 0)
    y = acc_ref[...]
    o_ref[...] = _rms(y, gf_ref[...]) if final else y


def _ffn(x, g, wg, wu, wd, gf, bm, final):
    m = x.shape[0]
    return pl.pallas_call(
        functools.partial(_ffn_kernel, final=final),
        grid=(m // bm,),
        in_specs=[pl.BlockSpec((bm, D_MODEL), lambda i: (i, 0)),
                  _resident((1, D_MODEL)),
                  _resident(wg.shape), _resident(wu.shape), _resident(wd.shape),
                  _resident((1, D_MODEL))],
        out_specs=pl.BlockSpec((bm, D_MODEL), lambda i: (i, 0)),
        out_shape=jax.ShapeDtypeStruct((m, D_MODEL), F32),
        scratch_shapes=[pltpu.VMEM((bm, D_MODEL), BF16), pltpu.VMEM((bm, D_MODEL), F32)],
        compiler_params=_params(1),
        name="ffn",
    )(x, g.reshape(1, D_MODEL), wg, wu, wd, gf.reshape(1, D_MODEL))


def _ret_sample_kernel(q_ref, k_ref, v_ref, rg_ref, s0_ref, gam_ref, *rest, nbb, aliased):
    ro_ref, s1_ref = rest[1:] if aliased else rest
    for h in range(RET_HEADS):
        lo = (h % 2) * RET_DK
        qt = q_ref[h // 2, :, lo:lo + RET_DK].T
        kt = k_ref[h // 2, :, lo:lo + RET_DK].T
        for b in range(nbb):
            s1 = s0_ref[b, h] * gam_ref[h] + kt[:, b:b + 1] * v_ref[h, b:b + 1, :]
            s1_ref[b, h] = s1
            o = jnp.sum(s1 * qt[:, b:b + 1], axis=0, keepdims=True) * (RET_DK ** -0.5)
            ro_ref[h, b:b + 1, :] = _group_norm(o) * _silu(rg_ref[h, b:b + 1, :])


def _ret_sample(proj, state, layer, s_prev, nbb):
    n = proj.shape[1]
    gam = jnp.exp(jnp.log1p(-jnp.exp2(-5.0 - jnp.arange(RET_HEADS, dtype=F32))))
    gam = jnp.broadcast_to(gam[:, None, None], (RET_HEADS, 1, RET_DV))
    st_spec = pl.BlockSpec((None, nbb, RET_HEADS, RET_DK, RET_DV), lambda i: (layer, i, 0, 0, 0))
    aliased = s_prev is not None
    in_specs = [pl.BlockSpec((2, nbb, COL), lambda i: (RQ // 2, i, 0)),
                pl.BlockSpec((2, nbb, COL), lambda i: (RK // 2, i, 0)),
                pl.BlockSpec((4, nbb, COL), lambda i: (RV // 4, i, 0)),
                pl.BlockSpec((4, nbb, COL), lambda i: (RG // 4, i, 0)),
                st_spec, _resident(gam.shape)]
    args = [proj, proj, proj, proj, state, gam]
    if aliased:
        in_specs.append(pl.BlockSpec(memory_space=pl.ANY))
        args.append(s_prev)
    return pl.pallas_call(
        functools.partial(_ret_sample_kernel, nbb=nbb, aliased=aliased),
        grid=(n // nbb,),
        in_specs=in_specs,
        out_specs=[pl.BlockSpec((4, nbb, COL), lambda i: (0, i, 0)), st_spec],
        out_shape=[jax.ShapeDtypeStruct((4, n, COL), F32),
                   jax.ShapeDtypeStruct(state.shape, F32)],
        input_output_aliases={6: 1} if aliased else {},
        compiler_params=_params(1),
        name="ret_sample",
    )(*args)


def _attn_sample_kernel(q_ref, kn_ref, vn_ref, c0_ref, c1_ref, c2_ref, bias_ref, ones_ref, exp_ref,
                        ao_ref, prod_ref, s_ref, p_ref, pn_ref, pe_ref, *, nbb):
    caches = (c0_ref, c1_ref, c2_ref)
    scale = HEAD_DIM ** -0.5
    s_new = []
    for g in range(N_GROUPS):
        for b in range(nbb):
            prod_ref[b * ATT_BLOCK:(b + 1) * ATT_BLOCK, :] = (
                caches[g][b, :, :ATT_OUT] * q_ref[g, b:b + 1, :]).astype(BF16)
        s_ref[g] = _dot(prod_ref[...], ones_ref[...]) * scale
        s_new.append(_dot((q_ref[g] * kn_ref[g]).astype(BF16), ones_ref[...]) * scale)

    for b in range(nbb):
        rows = slice(b * ATT_BLOCK, (b + 1) * ATT_BLOCK)
        ps, pns, ls, lses = [], [], [], []
        for g in range(N_GROUPS):
            s = s_ref[g, rows, :] + bias_ref[g]
            sn = s_new[g][b:b + 1]
            m = jnp.maximum(jnp.max(s, axis=0, keepdims=True), sn)
            p = jnp.exp(s - m)
            pn = jnp.exp(sn - m)
            l = jnp.sum(p, axis=0, keepdims=True) + pn
            ps.append(p), pns.append(pn), ls.append(l), lses.append(m + jnp.log(l))
        top = functools.reduce(jnp.maximum, lses)
        es = [jnp.exp(x - top) for x in lses]
        den = functools.reduce(jnp.add, es)
        for g in range(N_GROUPS):
            w = es[g] / (den * ls[g])
            p_ref[g, rows, :] = (ps[g] * w).astype(BF16)
            pn_ref[g, b:b + 1, :] = pns[g] * w

    acc = jnp.zeros((nbb, ATT_OUT), F32)
    for g in range(N_GROUPS):
        acc = acc + _dot(pn_ref[g].astype(BF16), exp_ref[...]) * vn_ref[g]
    ao_ref[...] = acc
    for g in range(N_GROUPS):
        pe_ref[...] = _dot(p_ref[g], exp_ref[...])
        for b in range(nbb):
            rows = slice(b * ATT_BLOCK, (b + 1) * ATT_BLOCK)
            ao_ref[b:b + 1, :] += jnp.sum(pe_ref[rows, :] * caches[g][b, :, ATT_OUT:],
                                          axis=0, keepdims=True)


def _attn_sample(proj, caches, layer, nbb):
    n = proj.shape[1]
    views, cache_specs, biases = [], [], []
    slopes = _alibi_slopes()
    for g, (window, dil) in enumerate(ATT_GROUPS):
        depth, n_c, wlen = caches[g].shape[:3]
        assert n_c == n and wlen == window == ATT_BLOCK * dil, (caches[g].shape, window, dil)
        views.append(caches[g].reshape(depth, n, ATT_BLOCK, dil * 2 * ATT_OUT))
        cache_specs.append(pl.BlockSpec((None, nbb, ATT_BLOCK, 2 * ATT_OUT),
                                        lambda i: (layer, i, 0, 0)))
        steps_back = (ATT_BLOCK - jnp.arange(ATT_BLOCK, dtype=F32)) * dil
        pen = -slopes[g][None, :] * steps_back[:, None]
        biases.append(jnp.pad(pen, ((0, 0), (0, ATT_BLOCK - HEADS_PER_GROUP))))
    bias = jnp.stack(biases)
    lane_head = jnp.arange(ATT_OUT) // HEAD_DIM
    ones = (lane_head[:, None] == jnp.arange(ATT_BLOCK)[None, :]).astype(BF16)
    expand = ones.T

    def grp(col):
        return pl.BlockSpec((N_GROUPS, nbb, COL), lambda i: (col // N_GROUPS, i, 0))

    assert AQ % N_GROUPS == 2 and AK % N_GROUPS == 2 and AV % N_GROUPS == 2
    shifted = proj[2:]
    return pl.pallas_call(
        functools.partial(_attn_sample_kernel, nbb=nbb),
        grid=(n // nbb,),
        in_specs=[grp(AQ - 2), grp(AK - 2), grp(AV - 2)] + cache_specs + [
            _resident(bias.shape), _resident(ones.shape), _resident(expand.shape)],
        out_specs=pl.BlockSpec((nbb, ATT_OUT), lambda i: (i, 0)),
        out_shape=jax.ShapeDtypeStruct((n, ATT_OUT), F32),
        scratch_shapes=[pltpu.VMEM((nbb * ATT_BLOCK, ATT_OUT), BF16),
                        pltpu.VMEM((N_GROUPS, nbb * ATT_BLOCK, ATT_BLOCK), F32),
                        pltpu.VMEM((N_GROUPS, nbb * ATT_BLOCK, ATT_BLOCK), BF16),
                        pltpu.VMEM((N_GROUPS, nbb, ATT_BLOCK), F32),
                        pltpu.VMEM((nbb * ATT_BLOCK, ATT_OUT), F32)],
        compiler_params=_params(1),
        name="attn_sample",
    )(shifted, shifted, shifted, *views, bias, ones, expand)


def _prep_weights(w_in, w_ret_branch, w_att_branch, w_out, w_gate_up, w_down):
    depth = w_in.shape[0]
    w3 = w_in.astype(BF16).reshape(depth, D_MODEL, N_COL_BLOCKS, COL).transpose(0, 2, 1, 3)
    w3 = jnp.concatenate([w3[:, :12], w3[:, 21:29], w3[:, 12:21]], axis=1)
    wr = w_ret_branch.astype(BF16).reshape(depth, RET_HEADS, RET_DV, D_MODEL)
    wa = w_att_branch.astype(BF16)
    wo = w_out.astype(BF16).reshape(depth, D_MODEL // COL, COL, D_MODEL)
    wgu = w_gate_up.astype(BF16).reshape(depth, D_MODEL, 2, FF_BLOCKS, COL).transpose(0, 2, 3, 1, 4)
    wd = w_down.astype(BF16).reshape(depth, FF_BLOCKS, COL, D_MODEL)
    return w3, wr, wa, wo, wgu[:, 0], wgu[:, 1], wd


def _kv_rows(proj, g, batch, keep):
    m = proj.shape[1]
    t = m // batch

    def tail(col):
        a = proj[col + g].reshape(batch, t, HEADS_PER_GROUP, HEAD_DIM)
        return a[:, t - keep:].astype(F32)

    return jnp.stack([tail(AK), tail(AV)], axis=2)


def _block_rows(m, target):
    bm = min(m, target)
    assert m % bm == 0, (m, bm)
    return bm


def kernel(x_prompt, x_sample, state_ret, cache_kv_w128, cache_kv_w512, cache_kv_w2048, norm_mix,
           w_in, w_ret_branch, w_att_branch, w_out, norm_ffn, w_gate_up, w_down, norm_final):
    depth = w_in.shape[0]
    batch, seq, _ = x_prompt.shape
    n_dec, dec_seq, _ = x_sample.shape
    assert dec_seq == 1 and seq % (ATT_BLOCK * ATT_GROUPS[-1][1]) == 0 and seq % RET_CHUNK == 0
    w3, wr, wa, wo, wg, wu, wd = _prep_weights(w_in, w_ret_branch, w_att_branch, w_out,
                                               w_gate_up, w_down)
    caches = (cache_kv_w128, cache_kv_w512, cache_kv_w2048)

    m = batch * seq
    bm = _block_rows(m, 512)
    x = x_prompt.reshape(m, D_MODEL)
    p_ret, p_kv = [], [[] for _ in ATT_GROUPS]
    for l in range(depth):
        proj = _in_proj(x, norm_mix[l], w3[l], bm, BF16)
        ro, s_fin = _ret_prompt(proj.reshape(N_COL_BLOCKS, batch, seq, COL))
        p_ret.append(s_fin)
        outs, lses = [], []
        for g, (window, _) in enumerate(ATT_GROUPS):
            o, lse = _attn_prompt(proj, g, batch)
            outs.append(o), lses.append(lse)
            p_kv[g].append(_kv_rows(proj, g, batch, min(window, seq)))
        x = _mix_out(outs + lses, ro.reshape(4, m, COL), proj, x, wr[l], wa[l], wo[l], bm)
        x = _ffn(x, norm_ffn[l], wg[l], wu[l], wd[l], norm_final, bm, final=(l == depth - 1))
    y_prompt = x.reshape(batch, seq, D_MODEL)

    bs = _block_rows(n_dec, 512)
    nbb = 8
    assert n_dec % nbb == 0
    x = x_sample.reshape(n_dec, D_MODEL)
    s_ret, s_kv = None, [[] for _ in ATT_GROUPS]
    for l in range(depth):
        proj = _in_proj(x, norm_mix[l], w3[l], bs, F32)
        ro, s_ret = _ret_sample(proj, state_ret, l, s_ret, nbb)
        ao = _attn_sample(proj, caches, l, nbb)
        for g in range(N_GROUPS):
            s_kv[g].append(_kv_rows(proj, g, n_dec, 1))
        x = _mix_out([ao], ro, proj, x, wr[l], wa[l], wo[l], bs)
        x = _ffn(x, norm_ffn[l], wg[l], wu[l], wd[l], norm_final, bs, final=(l == depth - 1))
    y_sample = x.reshape(n_dec, 1, D_MODEL)

    return (y_prompt, y_sample, jnp.stack(p_ret),
            jnp.stack(p_kv[0]), jnp.stack(p_kv[1]), jnp.stack(p_kv[2]),
            s_ret, jnp.stack(s_kv[0]), jnp.stack(s_kv[1]), jnp.stack(s_kv[2]))
```

```python
import functools

import jax
import jax.numpy as jnp
from jax import lax
from jax.experimental import pallas as pl
from jax.experimental.pallas import tpu as pltpu

F32 = jnp.float32
BF16 = jnp.bfloat16

D_MODEL = 1024
RET_HEADS = 4
RET_DK = 128
RET_DV = 256
RET_CHUNK = 128
ATT_GROUPS = ((128, 1), (512, 4), (2048, 16))
N_GROUPS = 3
HEADS_PER_GROUP = 4
ATT_HEADS = N_GROUPS * HEADS_PER_GROUP
HEAD_DIM = 64
ATT_BLOCK = 128
ATT_OUT = HEADS_PER_GROUP * HEAD_DIM
D_FF = 2816
D_IN = 7424
RMS_EPS = 1e-6
GN_EPS = 1e-5

LANES = 128
COL = 256
N_COL_BLOCKS = D_IN // COL
FF_BLOCKS = D_FF // COL
RQ, RK, RV, RG, GR, GA, AQ, AK, AV = 0, 2, 4, 8, 12, 16, 20, 23, 26
N_MAIN_BLOCKS = AQ
DILATIONS = tuple(d for _, d in ATT_GROUPS)
MASKED = -1e30
VMEM_LIMIT = 56 * 1024 * 1024


def _params(n_axes):
    return pltpu.CompilerParams(dimension_semantics=("arbitrary",) * n_axes,
                                vmem_limit_bytes=VMEM_LIMIT)


def _dot(a, b):
    return jnp.dot(a, b, preferred_element_type=F32)


def _dot_nt(a, b):
    return lax.dot_general(a, b, (((1,), (1,)), ((), ())), preferred_element_type=F32)


def _dot_tn(a, b):
    return lax.dot_general(a, b, (((0,), (0,)), ((), ())), preferred_element_type=F32)


def _rms(x, g):
    return x * lax.rsqrt(jnp.mean(x * x, axis=-1, keepdims=True) + RMS_EPS) * g


def _silu(x):
    return x * jax.nn.sigmoid(x)


def _group_norm(o):
    mu = jnp.mean(o, axis=-1, keepdims=True)
    d = o - mu
    var = jnp.mean(d * d, axis=-1, keepdims=True)
    return d * lax.rsqrt(var + GN_EPS)


def _resident(shape):
    zeros = (0,) * len(shape)
    return pl.BlockSpec(shape, lambda *_: zeros, pipeline_mode=pl.Buffered(1))


def _inproj_kernel(x_ref, g_ref, w_ref, main_ref, a0_ref, a1_ref, a2_ref, xn_ref, hn_ref, hd_ref,
                   *, dils):
    bm = x_ref.shape[0]
    xn = _rms(x_ref[...], g_ref[...])
    hn_ref[...] = xn.astype(BF16)
    if any(d > 1 for d in dils):
        for c in range(D_MODEL // LANES):
            xn_ref[c] = xn[:, c * LANES:(c + 1) * LANES]

    def body(j, carry):
        main_ref[j] = _dot(hn_ref[...], w_ref[j]).astype(main_ref.dtype)
        return carry

    lax.fori_loop(0, N_MAIN_BLOCKS, body, 0)
    for g, (a_ref, d) in enumerate(zip((a0_ref, a1_ref, a2_ref), dils)):
        if d == 1:
            lhs_ref = hn_ref
        else:
            lhs_ref = hd_ref
            n = bm // d
            for r in range(d):
                for c in range(D_MODEL // LANES):
                    hd_ref[r * n:(r + 1) * n, c * LANES:(c + 1) * LANES] = (
                        xn_ref[c, pl.ds(r, n, stride=d), :].astype(BF16))
        for c, col in enumerate((AQ, AK, AV)):
            out = _dot(lhs_ref[...], w_ref[col + g])
            a_ref[c] = out.reshape(d, bm // d, COL).astype(a_ref.dtype)


def _in_proj(x, g, w3, bm, out_dtype, dils):
    m = x.shape[0]
    att_specs = [pl.BlockSpec((3, d, bm // d, COL), lambda i: (0, 0, i, 0)) for d in dils]
    att_shapes = [jax.ShapeDtypeStruct((3, d, m // d, COL), out_dtype) for d in dils]
    return pl.pallas_call(
        functools.partial(_inproj_kernel, dils=dils),
        grid=(m // bm,),
        in_specs=[pl.BlockSpec((bm, D_MODEL), lambda i: (i, 0)),
                  _resident((1, D_MODEL)),
                  _resident((N_COL_BLOCKS, D_MODEL, COL))],
        out_specs=[pl.BlockSpec((N_MAIN_BLOCKS, bm, COL), lambda i: (0, i, 0))] + att_specs,
        out_shape=[jax.ShapeDtypeStruct((N_MAIN_BLOCKS, m, COL), out_dtype)] + att_shapes,
        scratch_shapes=[pltpu.VMEM((D_MODEL // LANES, bm, LANES), F32),
                        pltpu.VMEM((bm, D_MODEL), BF16), pltpu.VMEM((bm, D_MODEL), BF16)],
        compiler_params=_params(1),
        name="in_proj",
    )(x, g.reshape(1, D_MODEL), w3)


def _ret_prompt_kernel(q_ref, k_ref, v_ref, rg_ref, dmat_ref, inner_ref, tail_ref, gc_ref,
                       ro_ref, s_ref, *, batch):
    @pl.when(pl.program_id(0) == 0)
    def _():
        s_ref[...] = jnp.zeros_like(s_ref)

    for b in range(batch):
        for h in range(RET_HEADS):
            lo = (h % 2) * RET_DK
            q = q_ref[h // 2, b, :, lo:lo + RET_DK]
            k = k_ref[h // 2, b, :, lo:lo + RET_DK]
            v = v_ref[h, b]
            s0 = s_ref[b, h]
            scores = _dot_nt(q, k) * dmat_ref[h]
            o = _dot(scores.astype(BF16), v) + _dot(q, s0.astype(BF16)) * inner_ref[h]
            kt = (k.astype(F32) * tail_ref[h]).astype(BF16)
            s_ref[b, h] = s0 * gc_ref[h] + _dot_tn(kt, v)
            y = _group_norm(o) * _silu(rg_ref[h, b].astype(F32))
            ro_ref[h, b] = y.astype(ro_ref.dtype)


def _ret_tables(chunk):
    log_gamma = jnp.log1p(-jnp.exp2(-5.0 - jnp.arange(RET_HEADS, dtype=F32)))
    scale = RET_DK ** -0.5
    idx = jnp.arange(chunk, dtype=F32)
    diff = idx[:, None] - idx[None, :]
    dmat = jnp.where(diff >= 0, jnp.exp(jnp.maximum(diff, 0.0)[None] * log_gamma[:, None, None]), 0.0) * scale
    inner = jnp.exp((idx + 1.0)[None, :] * log_gamma[:, None]) * scale
    tail = jnp.exp((chunk - 1.0 - idx)[None, :] * log_gamma[:, None])
    gc = jnp.exp(chunk * log_gamma)
    inner = jnp.broadcast_to(inner[:, :, None], (RET_HEADS, chunk, RET_DV))
    tail = jnp.broadcast_to(tail[:, :, None], (RET_HEADS, chunk, RET_DK))
    gc = jnp.broadcast_to(gc[:, None, None], (RET_HEADS, 1, RET_DV))
    return dmat, inner, tail, gc


def _ret_prompt(proj4):
    _, b, t, _ = proj4.shape
    c = RET_CHUNK
    dmat, inner, tail, gc = _ret_tables(c)
    return pl.pallas_call(
        functools.partial(_ret_prompt_kernel, batch=b),
        grid=(t // c,),
        in_specs=[pl.BlockSpec((2, b, c, COL), lambda i: (RQ // 2, 0, i, 0)),
                  pl.BlockSpec((2, b, c, COL), lambda i: (RK // 2, 0, i, 0)),
                  pl.BlockSpec((4, b, c, COL), lambda i: (RV // 4, 0, i, 0)),
                  pl.BlockSpec((4, b, c, COL), lambda i: (RG // 4, 0, i, 0)),
                  _resident(dmat.shape), _resident(inner.shape), _resident(tail.shape),
                  _resident(gc.shape)],
        out_specs=[pl.BlockSpec((4, b, c, COL), lambda i: (0, 0, i, 0)),
                   pl.BlockSpec((b, RET_HEADS, RET_DK, RET_DV), lambda i: (0, 0, 0, 0))],
        out_shape=[jax.ShapeDtypeStruct((4, b, t, COL), BF16),
                   jax.ShapeDtypeStruct((b, RET_HEADS, RET_DK, RET_DV), F32)],
        compiler_params=_params(1),
        name="ret_prompt",
    )(proj4, proj4, proj4, proj4, dmat, inner, tail, gc)


def _attn_prompt_kernel(q_ref, kc_ref, kp_ref, vc_ref, vp_ref, bias_ref, o_ref, lse_ref):
    n = pl.program_id(2)
    lane_head = lax.broadcasted_iota(jnp.int32, (ATT_BLOCK, ATT_OUT), 1) // HEAD_DIM
    q = q_ref[...]
    qs = jnp.concatenate([jnp.where(lane_head == h, q, jnp.zeros_like(q))
                          for h in range(HEADS_PER_GROUP)], axis=0)
    kcat = jnp.concatenate([kp_ref[...], kc_ref[...]], axis=0)
    vcat = jnp.concatenate([vp_ref[...], vc_ref[...]], axis=0)
    s = _dot_nt(qs, kcat) * (HEAD_DIM ** -0.5) + bias_ref[jnp.minimum(n, 1)]
    m = jnp.max(s, axis=-1, keepdims=True)
    p = jnp.exp(s - m)
    l = jnp.sum(p, axis=-1, keepdims=True)
    on = _dot(p.astype(BF16), vcat) / l
    lse = m + jnp.log(l)
    o = jnp.zeros((ATT_BLOCK, ATT_OUT), F32)
    ls = jnp.zeros((ATT_BLOCK, ATT_OUT), F32)
    for h in range(HEADS_PER_GROUP):
        rows = slice(h * ATT_BLOCK, (h + 1) * ATT_BLOCK)
        o = jnp.where(lane_head == h, on[rows], o)
        ls = jnp.where(lane_head == h, lse[rows], ls)
    o_ref[...] = o.astype(o_ref.dtype)
    lse_ref[...] = ls


def _alibi_slopes():
    return jnp.exp2(-8.0 * (jnp.arange(ATT_HEADS, dtype=F32) + 1.0) / ATT_HEADS).reshape(
        N_GROUPS, HEADS_PER_GROUP)


def _attn_prompt_bias(g):
    window, dilation = ATT_GROUPS[g]
    steps = window // dilation
    qi = jnp.arange(ATT_BLOCK)[:, None]
    kj = jnp.arange(2 * ATT_BLOCK)[None, :]
    dist = qi + ATT_BLOCK - kj
    valid = (dist >= 0) & (dist <= steps)
    pen = -_alibi_slopes()[g][:, None, None] * (dist * dilation).astype(F32)[None]
    later = jnp.where(valid[None], pen, MASKED)
    first = jnp.where((valid & (kj >= ATT_BLOCK))[None], pen, MASKED)
    return jnp.stack([first, later]).reshape(2, HEADS_PER_GROUP * ATT_BLOCK, 2 * ATT_BLOCK)


def _attn_prompt(qkv, g, batch):
    _, dil, rows, _ = qkv.shape
    nb = rows // batch // ATT_BLOCK
    bias = _attn_prompt_bias(g)

    def cur(c):
        return pl.BlockSpec((None, None, ATT_BLOCK, COL), lambda b, r, n: (c, r, b * nb + n, 0))

    def prev(c):
        return pl.BlockSpec((None, None, ATT_BLOCK, COL),
                            lambda b, r, n: (c, r, b * nb + jnp.maximum(n - 1, 0), 0))

    out_spec = pl.BlockSpec((None, ATT_BLOCK, COL), lambda b, r, n: (r, b * nb + n, 0))
    return pl.pallas_call(
        _attn_prompt_kernel,
        grid=(batch, dil, nb),
        in_specs=[cur(0), cur(1), prev(1), cur(2), prev(2), _resident(bias.shape)],
        out_specs=[out_spec, out_spec],
        out_shape=[jax.ShapeDtypeStruct((dil, rows, COL), BF16),
                   jax.ShapeDtypeStruct((dil, rows, COL), F32)],
        compiler_params=_params(3),
        name=f"attn_prompt_g{g}",
    )(qkv, qkv, qkv, qkv, qkv, bias)


def _token_order(ref, scr_ref):
    d, n, _ = ref.shape
    if d == 1:
        return ref[0].astype(F32)
    for r in range(d):
        part = ref[r].astype(F32)
        for c in range(COL // LANES):
            scr_ref[c, pl.ds(r, n, stride=d), :] = part[:, c * LANES:(c + 1) * LANES]
    return jnp.concatenate([scr_ref[c] for c in range(COL // LANES)], axis=1)


def _mix_out_kernel(*refs, n_att):
    att = refs[:n_att]
    ro_ref, gr_ref, ga_ref, x_ref, wr_ref, wa_ref, wo_ref, o_ref = refs[n_att:n_att + 8]
    scratch = refs[n_att + 8:]
    if n_att == 1:
        ao = att[0][...]
    else:
        vals = [_token_order(r, s) for r, s in zip(att, scratch)]
        outs, lses = vals[:N_GROUPS], vals[N_GROUPS:]
        top = functools.reduce(jnp.maximum, lses)
        es = [jnp.exp(x - top) for x in lses]
        den = functools.reduce(jnp.add, es)
        ao = functools.reduce(jnp.add, [e * o for e, o in zip(es, outs)]) / den
    a_br = _dot(ao.astype(BF16), wa_ref[...])
    r_br = functools.reduce(jnp.add, [_dot(ro_ref[h].astype(BF16), wr_ref[h])
                                      for h in range(RET_HEADS)])
    y = x_ref[...]
    for c in range(D_MODEL // COL):
        cols = slice(c * COL, (c + 1) * COL)
        merged = (jax.nn.sigmoid(gr_ref[c].astype(F32)) * r_br[:, cols]
                  + jax.nn.sigmoid(ga_ref[c].astype(F32)) * a_br[:, cols])
        y = y + _dot(merged.astype(BF16), wo_ref[c])
    o_ref[...] = y


def _mix_out(att, ro, proj, x, wr, wa, wo, bm):
    m = x.shape[0]
    if len(att) == 1:
        att_specs = [pl.BlockSpec((bm, COL), lambda i: (i, 0))]
    else:
        att_specs = [pl.BlockSpec((a.shape[0], bm // a.shape[0], COL), lambda i: (0, i, 0))
                     for a in att]
    scratch = [pltpu.VMEM((COL // LANES, bm, LANES), F32) for a in att if a.ndim == 3]
    return pl.pallas_call(
        functools.partial(_mix_out_kernel, n_att=len(att)),
        grid=(m // bm,),
        scratch_shapes=scratch,
        in_specs=att_specs + [
            pl.BlockSpec((4, bm, COL), lambda i: (0, i, 0)),
            pl.BlockSpec((4, bm, COL), lambda i: (GR // 4, i, 0)),
            pl.BlockSpec((4, bm, COL), lambda i: (GA // 4, i, 0)),
            pl.BlockSpec((bm, D_MODEL), lambda i: (i, 0)),
            _resident(wr.shape), _resident(wa.shape), _resident(wo.shape)],
        out_specs=pl.BlockSpec((bm, D_MODEL), lambda i: (i, 0)),
        out_shape=jax.ShapeDtypeStruct((m, D_MODEL), F32),
        compiler_params=_params(1),
        name="mix_out",
    )(*att, ro, proj, proj, x, wr, wa, wo)


def _ffn_kernel(x_ref, g_ref, wg_ref, wu_ref, wd_ref, gf_ref, o_ref, hn_ref, acc_ref, *, final):
    x = x_ref[...]
    hn_ref[...] = _rms(x, g_ref[...]).astype(BF16)
    acc_ref[...] = x

    def body(c, carry):
        hn = hn_ref[...]
        act = _silu(_dot(hn, wg_ref[c])) * _dot(hn, wu_ref[c])
        acc_ref[...] += _dot(act.astype(BF16), wd_ref[c])
        return carry

    lax.fori_loop(0, FF_BLOCKS, body, 0)
    y = acc_ref[...]
    o_ref[...] = _rms(y, gf_ref[...]) if final else y


def _ffn(x, g, wg, wu, wd, gf, bm, final):
    m = x.shape[0]
    return pl.pallas_call(
        functools.partial(_ffn_kernel, final=final),
        grid=(m // bm,),
        in_specs=[pl.BlockSpec((bm, D_MODEL), lambda i: (i, 0)),
                  _resident((1, D_MODEL)),
                  _resident(wg.shape), _resident(wu.shape), _resident(wd.shape),
                  _resident((1, D_MODEL))],
        out_specs=pl.BlockSpec((bm, D_MODEL), lambda i: (i, 0)),
        out_shape=jax.ShapeDtypeStruct((m, D_MODEL), F32),
        scratch_shapes=[pltpu.VMEM((bm, D_MODEL), BF16), pltpu.VMEM((bm, D_MODEL), F32)],
        compiler_params=_params(1),
        name="ffn",
    )(x, g.reshape(1, D_MODEL), wg, wu, wd, gf.reshape(1, D_MODEL))


def _ret_sample_kernel(q_ref, k_ref, v_ref, rg_ref, s0_ref, gam_ref, *rest, nbb, aliased):
    ro_ref, s1_ref = rest[1:] if aliased else rest
    for h in range(RET_HEADS):
        lo = (h % 2) * RET_DK
        qt = q_ref[h // 2, :, lo:lo + RET_DK].T
        kt = k_ref[h // 2, :, lo:lo + RET_DK].T
        for b in range(nbb):
            s1 = s0_ref[b, h] * gam_ref[h] + kt[:, b:b + 1] * v_ref[h, b:b + 1, :]
            s1_ref[b, h] = s1
            o = jnp.sum(s1 * qt[:, b:b + 1], axis=0, keepdims=True) * (RET_DK ** -0.5)
            ro_ref[h, b:b + 1, :] = _group_norm(o) * _silu(rg_ref[h, b:b + 1, :])


def _ret_sample(proj, state, layer, s_prev, nbb):
    n = proj.shape[1]
    gam = jnp.exp(jnp.log1p(-jnp.exp2(-5.0 - jnp.arange(RET_HEADS, dtype=F32))))
    gam = jnp.broadcast_to(gam[:, None, None], (RET_HEADS, 1, RET_DV))
    st_spec = pl.BlockSpec((None, nbb, RET_HEADS, RET_DK, RET_DV), lambda i: (layer, i, 0, 0, 0))
    aliased = s_prev is not None
    in_specs = [pl.BlockSpec((2, nbb, COL), lambda i: (RQ // 2, i, 0)),
                pl.BlockSpec((2, nbb, COL), lambda i: (RK // 2, i, 0)),
                pl.BlockSpec((4, nbb, COL), lambda i: (RV // 4, i, 0)),
                pl.BlockSpec((4, nbb, COL), lambda i: (RG // 4, i, 0)),
                st_spec, _resident(gam.shape)]
    args = [proj, proj, proj, proj, state, gam]
    if aliased:
        in_specs.append(pl.BlockSpec(memory_space=pl.ANY))
        args.append(s_prev)
    return pl.pallas_call(
        functools.partial(_ret_sample_kernel, nbb=nbb, aliased=aliased),
        grid=(n // nbb,),
        in_specs=in_specs,
        out_specs=[pl.BlockSpec((4, nbb, COL), lambda i: (0, i, 0)), st_spec],
        out_shape=[jax.ShapeDtypeStruct((4, n, COL), F32),
                   jax.ShapeDtypeStruct(state.shape, F32)],
        input_output_aliases={6: 1} if aliased else {},
        compiler_params=_params(1),
        name="ret_sample",
    )(*args)


def _attn_sample_kernel(a0_ref, a1_ref, a2_ref, c0_ref, c1_ref, c2_ref, b0_ref, b1_ref, b2_ref,
                        ao_ref, *, nbb):
    qkv, caches, biases = (a0_ref, a1_ref, a2_ref), (c0_ref, c1_ref, c2_ref), (b0_ref, b1_ref, b2_ref)
    scale = HEAD_DIM ** -0.5
    shape = (2 * HEADS_PER_GROUP, ATT_OUT)
    own = (lax.broadcasted_iota(jnp.int32, shape, 0)
           == lax.broadcasted_iota(jnp.int32, shape, 1) // HEAD_DIM)
    for b in range(nbb):
        n = pl.program_id(0) * nbb + b
        ps, pns, ls, lses = [], [], [], []
        for g in range(N_GROUPS):
            qbd = jnp.where(own, qkv[g][0, 0, pl.ds(n, 1), :], 0.0)
            s = _dot(qbd.astype(BF16), caches[g][b, 0].astype(BF16)) * scale + biases[g][...]
            sn = jnp.sum(qbd * qkv[g][1, 0, pl.ds(n, 1), :], axis=-1, keepdims=True) * scale
            m = jnp.maximum(jnp.max(s, axis=-1, keepdims=True), sn)
            p = jnp.exp(s - m)
            pn = jnp.exp(sn - m)
            l = jnp.sum(p, axis=-1, keepdims=True) + pn
            ps.append(p), pns.append(pn), ls.append(l), lses.append(m + jnp.log(l))
        top = functools.reduce(jnp.maximum, lses)
        es = [jnp.exp(x - top) for x in lses]
        den = functools.reduce(jnp.add, es)
        acc = jnp.zeros(shape, F32)
        for g in range(N_GROUPS):
            og = _dot_nt(ps[g].astype(BF16), caches[g][b, 1].astype(BF16))
            og = og + pns[g] * qkv[g][2, 0, pl.ds(n, 1), :]
            acc = acc + og * (es[g] / (den * ls[g]))
        ao_ref[pl.ds(n, 1), :] = jnp.sum(jnp.where(own, acc, 0.0), axis=0, keepdims=True)


def _attn_sample(qkv, caches, layer, nbb):
    n = qkv[0].shape[2]
    views, cache_specs, biases = [], [], []
    slopes = _alibi_slopes()
    for g, (window, dil) in enumerate(ATT_GROUPS):
        depth, n_c, wlen = caches[g].shape[:3]
        assert n_c == n and wlen == window == ATT_BLOCK * dil, (caches[g].shape, window, dil)
        views.append(caches[g].transpose(0, 1, 3, 4, 5, 2).reshape(depth, n, 2, ATT_OUT, wlen))
        cache_specs.append(pl.BlockSpec((None, nbb, 2, ATT_OUT, wlen),
                                        lambda i: (layer, i, 0, 0, 0)))
        back = wlen - jnp.arange(wlen)
        pen = -slopes[g][:, None] * back.astype(F32)[None, :]
        pen = jnp.where((back % dil == 0)[None, :], pen, MASKED)
        biases.append(jnp.pad(pen, ((0, HEADS_PER_GROUP), (0, 0))))
    return pl.pallas_call(
        functools.partial(_attn_sample_kernel, nbb=nbb),
        grid=(n // nbb,),
        in_specs=[_resident(a.shape) for a in qkv] + cache_specs + [
            _resident(b.shape) for b in biases],
        out_specs=pl.BlockSpec((n, ATT_OUT), lambda i: (0, 0)),
        out_shape=jax.ShapeDtypeStruct((n, ATT_OUT), F32),
        compiler_params=_params(1),
        name="attn_sample",
    )(*qkv, *views, *biases)


def _prep_weights(w_in, w_ret_branch, w_att_branch, w_out, w_gate_up, w_down):
    depth = w_in.shape[0]
    w3 = w_in.astype(BF16).reshape(depth, D_MODEL, N_COL_BLOCKS, COL).transpose(0, 2, 1, 3)
    w3 = jnp.concatenate([w3[:, :12], w3[:, 21:29], w3[:, 12:21]], axis=1)
    wr = w_ret_branch.astype(BF16).reshape(depth, RET_HEADS, RET_DV, D_MODEL)
    wa = w_att_branch.astype(BF16)
    wo = w_out.astype(BF16).reshape(depth, D_MODEL // COL, COL, D_MODEL)
    wgu = w_gate_up.astype(BF16).reshape(depth, D_MODEL, 2, FF_BLOCKS, COL).transpose(0, 2, 3, 1, 4)
    wd = w_down.astype(BF16).reshape(depth, FF_BLOCKS, COL, D_MODEL)
    return w3, wr, wa, wo, wgu[:, 0], wgu[:, 1], wd


def _kv_rows(qkv, batch, keep):
    _, dil, rows, _ = qkv.shape
    per_seq = rows // batch
    n = keep // dil
    if per_seq == n:
        a = qkv[1:].reshape(2, dil, batch, n, COL).transpose(2, 0, 1, 3, 4)
    else:
        a = jnp.stack([qkv[1:, :, (b + 1) * per_seq - n:(b + 1) * per_seq] for b in range(batch)])
    a = a.astype(F32).reshape(batch, 2, dil, n, HEADS_PER_GROUP, HEAD_DIM)
    return a.transpose(0, 3, 2, 1, 4, 5).reshape(batch, keep, 2, HEADS_PER_GROUP, HEAD_DIM)


def _block_rows(m, target):
    bm = min(m, target)
    assert m % bm == 0, (m, bm)
    return bm


def kernel(x_prompt, x_sample, state_ret, cache_kv_w128, cache_kv_w512, cache_kv_w2048, norm_mix,
           w_in, w_ret_branch, w_att_branch, w_out, norm_ffn, w_gate_up, w_down, norm_final):
    depth = w_in.shape[0]
    batch, seq, _ = x_prompt.shape
    n_dec, dec_seq, _ = x_sample.shape
    assert dec_seq == 1 and seq % (ATT_BLOCK * ATT_GROUPS[-1][1]) == 0 and seq % RET_CHUNK == 0
    w3, wr, wa, wo, wg, wu, wd = _prep_weights(w_in, w_ret_branch, w_att_branch, w_out,
                                               w_gate_up, w_down)
    caches = (cache_kv_w128, cache_kv_w512, cache_kv_w2048)

    m = batch * seq
    bm = _block_rows(m, 512)
    x = x_prompt.reshape(m, D_MODEL)
    p_ret, p_kv = [], [[] for _ in ATT_GROUPS]
    for l in range(depth):
        proj, *qkv = _in_proj(x, norm_mix[l], w3[l], bm, BF16, DILATIONS)
        ro, s_fin = _ret_prompt(proj.reshape(N_MAIN_BLOCKS, batch, seq, COL))
        p_ret.append(s_fin)
        outs, lses = [], []
        for g, (window, _) in enumerate(ATT_GROUPS):
            o, lse = _attn_prompt(qkv[g], g, batch)
            outs.append(o), lses.append(lse)
            p_kv[g].append(_kv_rows(qkv[g], batch, min(window, seq)))
        x = _mix_out(outs + lses, ro.reshape(4, m, COL), proj, x, wr[l], wa[l], wo[l], bm)
        x = _ffn(x, norm_ffn[l], wg[l], wu[l], wd[l], norm_final, bm, final=(l == depth - 1))
    y_prompt = x.reshape(batch, seq, D_MODEL)

    bs = _block_rows(n_dec, 512)
    nbb_ret, nbb_att = 8, 2
    assert n_dec % nbb_ret == 0 and n_dec % nbb_att == 0
    x = x_sample.reshape(n_dec, D_MODEL)
    s_ret, s_kv = None, [[] for _ in ATT_GROUPS]
    for l in range(depth):
        proj, *qkv = _in_proj(x, norm_mix[l], w3[l], bs, F32, (1,) * N_GROUPS)
        ro, s_ret = _ret_sample(proj, state_ret, l, s_ret, nbb_ret)
        ao = _attn_sample(qkv, caches, l, nbb_att)
        for g in range(N_GROUPS):
            s_kv[g].append(_kv_rows(qkv[g], n_dec, 1))
        x = _mix_out([ao], ro, proj, x, wr[l], wa[l], wo[l], bs)
        x = _ffn(x, norm_ffn[l], wg[l], wu[l], wd[l], norm_final, bs, final=(l == depth - 1))
    y_sample = x.reshape(n_dec, 1, D_MODEL)

    return (y_prompt, y_sample, jnp.stack(p_ret),
            jnp.stack(p_kv[0]), jnp.stack(p_kv[1]), jnp.stack(p_kv[2]),
            s_ret, jnp.stack(s_kv[0]), jnp.stack(s_kv[1]), jnp.stack(s_kv[2]))
```

```python
import functools

import jax
import jax.numpy as jnp
from jax import lax
from jax.experimental import pallas as pl
from jax.experimental.pallas import tpu as pltpu

F32 = jnp.float32
BF16 = jnp.bfloat16

D_MODEL = 1024
RET_HEADS = 4
RET_DK = 128
RET_DV = 256
RET_CHUNK = 128
ATT_GROUPS = ((128, 1), (512, 4), (2048, 16))
N_GROUPS = 3
HEADS_PER_GROUP = 4
ATT_HEADS = N_GROUPS * HEADS_PER_GROUP
HEAD_DIM = 64
ATT_BLOCK = 128
ATT_OUT = HEADS_PER_GROUP * HEAD_DIM
D_FF = 2816
D_IN = 7424
RMS_EPS = 1e-6
GN_EPS = 1e-5

LANES = 128
COL = 256
N_COL_BLOCKS = D_IN // COL
FF_BLOCKS = D_FF // COL
RQ, RK, RV, RG, GR, GA, AQ, AK, AV = 0, 2, 4, 8, 12, 16, 20, 23, 26
N_MAIN_BLOCKS = AQ
DILATIONS = tuple(d for _, d in ATT_GROUPS)
ATT_UNITS_PER_STEP = 4
MASKED = -1e30
VMEM_LIMIT = 56 * 1024 * 1024


def _params(n_axes):
    return pltpu.CompilerParams(dimension_semantics=("arbitrary",) * n_axes,
                                vmem_limit_bytes=VMEM_LIMIT)


def _dot(a, b):
    return jnp.dot(a, b, preferred_element_type=F32)


def _dot_nt(a, b):
    return lax.dot_general(a, b, (((1,), (1,)), ((), ())), preferred_element_type=F32)


def _dot_tn(a, b):
    return lax.dot_general(a, b, (((0,), (0,)), ((), ())), preferred_element_type=F32)


def _rms(x, g):
    return x * lax.rsqrt(jnp.mean(x * x, axis=-1, keepdims=True) + RMS_EPS) * g


def _silu(x):
    return x * jax.nn.sigmoid(x)


def _group_norm(o):
    mu = jnp.mean(o, axis=-1, keepdims=True)
    d = o - mu
    var = jnp.mean(d * d, axis=-1, keepdims=True)
    return d * lax.rsqrt(var + GN_EPS)


def _resident(shape):
    zeros = (0,) * len(shape)
    return pl.BlockSpec(shape, lambda *_: zeros, pipeline_mode=pl.Buffered(1))


def _inproj_kernel(x_ref, g_ref, w_ref, main_ref, a0_ref, a1_ref, a2_ref, xn_ref, hn_ref, hd_ref,
                   *, dils):
    bm = x_ref.shape[0]
    xn = _rms(x_ref[...], g_ref[...])
    hn_ref[...] = xn.astype(BF16)
    if any(d > 1 for d in dils):
        for c in range(D_MODEL // LANES):
            xn_ref[c] = xn[:, c * LANES:(c + 1) * LANES]

    for j in range(N_MAIN_BLOCKS):
        main_ref[j] = _dot(hn_ref[...], w_ref[j]).astype(main_ref.dtype)
    for g, (a_ref, d) in enumerate(zip((a0_ref, a1_ref, a2_ref), dils)):
        if d == 1:
            lhs_ref = hn_ref
        else:
            lhs_ref = hd_ref
            n = bm // d
            for r in range(d):
                for c in range(D_MODEL // LANES):
                    hd_ref[r * n:(r + 1) * n, c * LANES:(c + 1) * LANES] = (
                        xn_ref[c, pl.ds(r, n, stride=d), :].astype(BF16))
        for c, col in enumerate((AQ, AK, AV)):
            out = _dot(lhs_ref[...], w_ref[col + g])
            a_ref[c] = out.reshape(d, bm // d, COL).astype(a_ref.dtype)


def _in_proj(x, g, w3, bm, out_dtype, dils):
    m = x.shape[0]
    att_specs = [pl.BlockSpec((3, d, bm // d, COL), lambda i: (0, 0, i, 0)) for d in dils]
    att_shapes = [jax.ShapeDtypeStruct((3, d, m // d, COL), out_dtype) for d in dils]
    return pl.pallas_call(
        functools.partial(_inproj_kernel, dils=dils),
        grid=(m // bm,),
        in_specs=[pl.BlockSpec((bm, D_MODEL), lambda i: (i, 0)),
                  _resident((1, D_MODEL)),
                  _resident((N_COL_BLOCKS, D_MODEL, COL))],
        out_specs=[pl.BlockSpec((N_MAIN_BLOCKS, bm, COL), lambda i: (0, i, 0))] + att_specs,
        out_shape=[jax.ShapeDtypeStruct((N_MAIN_BLOCKS, m, COL), out_dtype)] + att_shapes,
        scratch_shapes=[pltpu.VMEM((D_MODEL // LANES, bm, LANES), F32),
                        pltpu.VMEM((bm, D_MODEL), BF16), pltpu.VMEM((bm, D_MODEL), BF16)],
        compiler_params=_params(1),
        name="in_proj",
    )(x, g.reshape(1, D_MODEL), w3)


def _ret_prompt_kernel(q_ref, k_ref, v_ref, rg_ref, dmat_ref, inner_ref, tail_ref, gc_ref,
                       ro_ref, s_ref, *, batch):
    @pl.when(pl.program_id(0) == 0)
    def _():
        s_ref[...] = jnp.zeros_like(s_ref)

    for b in range(batch):
        for h in range(RET_HEADS):
            lo = (h % 2) * RET_DK
            q = q_ref[h // 2, b, :, lo:lo + RET_DK]
            k = k_ref[h // 2, b, :, lo:lo + RET_DK]
            v = v_ref[h, b]
            s0 = s_ref[b, h]
            scores = _dot_nt(q, k) * dmat_ref[h]
            o = _dot(scores.astype(BF16), v) + _dot(q, s0.astype(BF16)) * inner_ref[h]
            kt = (k.astype(F32) * tail_ref[h]).astype(BF16)
            s_ref[b, h] = s0 * gc_ref[h] + _dot_tn(kt, v)
            y = _group_norm(o) * _silu(rg_ref[h, b].astype(F32))
            ro_ref[h, b] = y.astype(ro_ref.dtype)


def _ret_tables(chunk):
    log_gamma = jnp.log1p(-jnp.exp2(-5.0 - jnp.arange(RET_HEADS, dtype=F32)))
    scale = RET_DK ** -0.5
    idx = jnp.arange(chunk, dtype=F32)
    diff = idx[:, None] - idx[None, :]
    dmat = jnp.where(diff >= 0, jnp.exp(jnp.maximum(diff, 0.0)[None] * log_gamma[:, None, None]), 0.0) * scale
    inner = jnp.exp((idx + 1.0)[None, :] * log_gamma[:, None]) * scale
    tail = jnp.exp((chunk - 1.0 - idx)[None, :] * log_gamma[:, None])
    gc = jnp.exp(chunk * log_gamma)
    inner = jnp.broadcast_to(inner[:, :, None], (RET_HEADS, chunk, RET_DV))
    tail = jnp.broadcast_to(tail[:, :, None], (RET_HEADS, chunk, RET_DK))
    gc = jnp.broadcast_to(gc[:, None, None], (RET_HEADS, 1, RET_DV))
    return dmat, inner, tail, gc


def _ret_prompt(proj4):
    _, b, t, _ = proj4.shape
    c = RET_CHUNK
    dmat, inner, tail, gc = _ret_tables(c)
    return pl.pallas_call(
        functools.partial(_ret_prompt_kernel, batch=b),
        grid=(t // c,),
        in_specs=[pl.BlockSpec((2, b, c, COL), lambda i: (RQ // 2, 0, i, 0)),
                  pl.BlockSpec((2, b, c, COL), lambda i: (RK // 2, 0, i, 0)),
                  pl.BlockSpec((4, b, c, COL), lambda i: (RV // 4, 0, i, 0)),
                  pl.BlockSpec((4, b, c, COL), lambda i: (RG // 4, 0, i, 0)),
                  _resident(dmat.shape), _resident(inner.shape), _resident(tail.shape),
                  _resident(gc.shape)],
        out_specs=[pl.BlockSpec((4, b, c, COL), lambda i: (0, 0, i, 0)),
                   pl.BlockSpec((b, RET_HEADS, RET_DK, RET_DV), lambda i: (0, 0, 0, 0))],
        out_shape=[jax.ShapeDtypeStruct((4, b, t, COL), BF16),
                   jax.ShapeDtypeStruct((b, RET_HEADS, RET_DK, RET_DV), F32)],
        compiler_params=_params(1),
        name="ret_prompt",
    )(proj4, proj4, proj4, proj4, dmat, inner, tail, gc)


def _attn_prompt_kernel(q_ref, kc_ref, kp_ref, vc_ref, vp_ref, bias_ref, o_ref, lse_ref):
    nres, rows_per_step, _ = q_ref.shape
    first_step = jnp.where(pl.program_id(2) == 0, 0, 1)
    lane_head = lax.broadcasted_iota(jnp.int32, (ATT_BLOCK, ATT_OUT), 1) // HEAD_DIM
    scale = jnp.asarray(HEAD_DIM ** -0.5, BF16)
    for r in range(nres):
        for j in range(rows_per_step // ATT_BLOCK):
            blk = slice(j * ATT_BLOCK, (j + 1) * ATT_BLOCK)
            q = q_ref[r, blk, :] * scale
            qs = jnp.concatenate([jnp.where(lane_head == h, q, jnp.zeros_like(q))
                                  for h in range(HEADS_PER_GROUP)], axis=0)
            if j == 0:
                k_prev, v_prev, bias = kp_ref[r], vp_ref[r], bias_ref[first_step]
            else:
                before = slice((j - 1) * ATT_BLOCK, j * ATT_BLOCK)
                k_prev, v_prev, bias = kc_ref[r, before, :], vc_ref[r, before, :], bias_ref[1]
            kcat = jnp.concatenate([k_prev, kc_ref[r, blk, :]], axis=0)
            vcat = jnp.concatenate([v_prev, vc_ref[r, blk, :]], axis=0)
            s = _dot_nt(qs, kcat) + bias
            m = jnp.max(s, axis=-1, keepdims=True)
            p = jnp.exp(s - m)
            l = jnp.sum(p, axis=-1, keepdims=True)
            on = _dot(p.astype(BF16), vcat) / l
            lse = m + jnp.log(l)
            o = jnp.zeros((ATT_BLOCK, ATT_OUT), F32)
            ls = jnp.zeros((ATT_BLOCK, ATT_OUT), F32)
            for h in range(HEADS_PER_GROUP):
                rows = slice(h * ATT_BLOCK, (h + 1) * ATT_BLOCK)
                o = jnp.where(lane_head == h, on[rows], o)
                ls = jnp.where(lane_head == h, lse[rows], ls)
            o_ref[r, blk, :] = o.astype(o_ref.dtype)
            lse_ref[r, blk, :] = ls


def _alibi_slopes():
    return jnp.exp2(-8.0 * (jnp.arange(ATT_HEADS, dtype=F32) + 1.0) / ATT_HEADS).reshape(
        N_GROUPS, HEADS_PER_GROUP)


def _attn_prompt_bias(g):
    window, dilation = ATT_GROUPS[g]
    steps = window // dilation
    qi = jnp.arange(ATT_BLOCK)[:, None]
    kj = jnp.arange(2 * ATT_BLOCK)[None, :]
    dist = qi + ATT_BLOCK - kj
    valid = (dist >= 0) & (dist <= steps)
    pen = -_alibi_slopes()[g][:, None, None] * (dist * dilation).astype(F32)[None]
    later = jnp.where(valid[None], pen, MASKED)
    first = jnp.where((valid & (kj >= ATT_BLOCK))[None], pen, MASKED)
    return jnp.stack([first, later]).reshape(2, HEADS_PER_GROUP * ATT_BLOCK, 2 * ATT_BLOCK)


def _attn_prompt(qkv, g, batch):
    _, dil, rows, _ = qkv.shape
    nres = min(dil, ATT_UNITS_PER_STEP)
    nblk = ATT_UNITS_PER_STEP // nres
    per_seq = rows // batch // ATT_BLOCK
    assert dil % nres == 0 and per_seq % nblk == 0
    steps = per_seq // nblk
    bias = _attn_prompt_bias(g)

    def cur(c):
        return pl.BlockSpec((None, nres, nblk * ATT_BLOCK, COL),
                            lambda b, r, n: (c, r, b * steps + n, 0))

    def prev(c):
        return pl.BlockSpec((None, nres, ATT_BLOCK, COL),
                            lambda b, r, n: (c, r, b * per_seq + jnp.maximum(n * nblk - 1, 0), 0))

    out_spec = pl.BlockSpec((nres, nblk * ATT_BLOCK, COL), lambda b, r, n: (r, b * steps + n, 0))
    return pl.pallas_call(
        _attn_prompt_kernel,
        grid=(batch, dil // nres, steps),
        in_specs=[cur(0), cur(1), prev(1), cur(2), prev(2), _resident(bias.shape)],
        out_specs=[out_spec, out_spec],
        out_shape=[jax.ShapeDtypeStruct((dil, rows, COL), BF16),
                   jax.ShapeDtypeStruct((dil, rows, COL), F32)],
        compiler_params=_params(3),
        name=f"attn_prompt_g{g}",
    )(qkv, qkv, qkv, qkv, qkv, bias)


def _token_order(ref, scr_ref):
    d, n, _ = ref.shape
    if d == 1:
        return ref[0].astype(F32)
    for r in range(d):
        part = ref[r].astype(F32)
        for c in range(COL // LANES):
            scr_ref[c, pl.ds(r, n, stride=d), :] = part[:, c * LANES:(c + 1) * LANES]
    return jnp.concatenate([scr_ref[c] for c in range(COL // LANES)], axis=1)


def _mix_out_kernel(*refs, n_att):
    att = refs[:n_att]
    ro_ref, gr_ref, ga_ref, x_ref, wr_ref, wa_ref, wo_ref, o_ref = refs[n_att:n_att + 8]
    scratch = refs[n_att + 8:]
    if n_att == 1:
        ao = att[0][...]
    else:
        vals = [_token_order(r, s) for r, s in zip(att, scratch)]
        outs, lses = vals[:N_GROUPS], vals[N_GROUPS:]
        top = functools.reduce(jnp.maximum, lses)
        es = [jnp.exp(x - top) for x in lses]
        den = functools.reduce(jnp.add, es)
        ao = functools.reduce(jnp.add, [e * o for e, o in zip(es, outs)]) / den
    def token_major(ref):
        return jnp.concatenate([ref[c] for c in range(ref.shape[0])], axis=1)

    a_br = _dot(ao.astype(BF16), wa_ref[...])
    r_br = _dot(token_major(ro_ref).astype(BF16), wr_ref[...])
    merged = (jax.nn.sigmoid(token_major(gr_ref).astype(F32)) * r_br
              + jax.nn.sigmoid(token_major(ga_ref).astype(F32)) * a_br)
    o_ref[...] = x_ref[...] + _dot(merged.astype(BF16), wo_ref[...])


def _mix_out(att, ro, proj, x, wr, wa, wo, bm):
    m = x.shape[0]
    if len(att) == 1:
        att_specs = [pl.BlockSpec((bm, COL), lambda i: (i, 0))]
    else:
        att_specs = [pl.BlockSpec((a.shape[0], bm // a.shape[0], COL), lambda i: (0, i, 0))
                     for a in att]
    scratch = [pltpu.VMEM((COL // LANES, bm, LANES), F32) for a in att if a.ndim == 3]
    return pl.pallas_call(
        functools.partial(_mix_out_kernel, n_att=len(att)),
        grid=(m // bm,),
        scratch_shapes=scratch,
        in_specs=att_specs + [
            pl.BlockSpec((4, bm, COL), lambda i: (0, i, 0)),
            pl.BlockSpec((4, bm, COL), lambda i: (GR // 4, i, 0)),
            pl.BlockSpec((4, bm, COL), lambda i: (GA // 4, i, 0)),
            pl.BlockSpec((bm, D_MODEL), lambda i: (i, 0)),
            _resident(wr.shape), _resident(wa.shape), _resident(wo.shape)],
        out_specs=pl.BlockSpec((bm, D_MODEL), lambda i: (i, 0)),
        out_shape=jax.ShapeDtypeStruct((m, D_MODEL), F32),
        compiler_params=_params(1),
        name="mix_out",
    )(*att, ro, proj, proj, x, wr, wa, wo)


def _ffn_kernel(x_ref, g_ref, wg_ref, wu_ref, wd_ref, gf_ref, o_ref, hn_ref, act_ref, *, final):
    x = x_ref[...]
    hn_ref[...] = _rms(x, g_ref[...]).astype(BF16)
    for c in range(FF_BLOCKS):
        act = _silu(_dot(hn_ref[...], wg_ref[c])) * _dot(hn_ref[...], wu_ref[c])
        act_ref[:, c * COL:(c + 1) * COL] = act.astype(BF16)
    y = x + _dot(act_ref[...], wd_ref[...])
    o_ref[...] = _rms(y, gf_ref[...]) if final else y


def _ffn(x, g, wg, wu, wd, gf, bm, final):
    m = x.shape[0]
    return pl.pallas_call(
        functools.partial(_ffn_kernel, final=final),
        grid=(m // bm,),
        in_specs=[pl.BlockSpec((bm, D_MODEL), lambda i: (i, 0)),
                  _resident((1, D_MODEL)),
                  _resident(wg.shape), _resident(wu.shape), _resident(wd.shape),
                  _resident((1, D_MODEL))],
        out_specs=pl.BlockSpec((bm, D_MODEL), lambda i: (i, 0)),
        out_shape=jax.ShapeDtypeStruct((m, D_MODEL), F32),
        scratch_shapes=[pltpu.VMEM((bm, D_MODEL), BF16), pltpu.VMEM((bm, D_FF), BF16)],
        compiler_params=_params(1),
        name="ffn",
    )(x, g.reshape(1, D_MODEL), wg, wu, wd, gf.reshape(1, D_MODEL))


def _ret_sample_kernel(q_ref, k_ref, v_ref, rg_ref, s0_ref, gam_ref, *rest, nbb, aliased):
    ro_ref, s1_ref = rest[1:] if aliased else rest
    for h in range(RET_HEADS):
        lo = (h % 2) * RET_DK
        qt = q_ref[h // 2, :, lo:lo + RET_DK].T
        kt = k_ref[h // 2, :, lo:lo + RET_DK].T
        for b in range(nbb):
            s1 = s0_ref[b, h] * gam_ref[h] + kt[:, b:b + 1] * v_ref[h, b:b + 1, :]
            s1_ref[b, h] = s1
            ro_ref[h, b:b + 1, :] = jnp.sum(s1 * qt[:, b:b + 1], axis=0, keepdims=True)
    for h in range(RET_HEADS):
        o = ro_ref[h] * (RET_DK ** -0.5)
        ro_ref[h] = _group_norm(o) * _silu(rg_ref[h])


def _ret_sample(proj, state, layer, s_prev, nbb):
    n = proj.shape[1]
    gam = jnp.exp(jnp.log1p(-jnp.exp2(-5.0 - jnp.arange(RET_HEADS, dtype=F32))))
    gam = jnp.broadcast_to(gam[:, None, None], (RET_HEADS, 1, RET_DV))
    st_spec = pl.BlockSpec((None, nbb, RET_HEADS, RET_DK, RET_DV), lambda i: (layer, i, 0, 0, 0))
    aliased = s_prev is not None
    in_specs = [pl.BlockSpec((2, nbb, COL), lambda i: (RQ // 2, i, 0)),
                pl.BlockSpec((2, nbb, COL), lambda i: (RK // 2, i, 0)),
                pl.BlockSpec((4, nbb, COL), lambda i: (RV // 4, i, 0)),
                pl.BlockSpec((4, nbb, COL), lambda i: (RG // 4, i, 0)),
                st_spec, _resident(gam.shape)]
    args = [proj, proj, proj, proj, state, gam]
    if aliased:
        in_specs.append(pl.BlockSpec(memory_space=pl.ANY))
        args.append(s_prev)
    return pl.pallas_call(
        functools.partial(_ret_sample_kernel, nbb=nbb, aliased=aliased),
        grid=(n // nbb,),
        in_specs=in_specs,
        out_specs=[pl.BlockSpec((4, nbb, COL), lambda i: (0, i, 0)), st_spec],
        out_shape=[jax.ShapeDtypeStruct((4, n, COL), F32),
                   jax.ShapeDtypeStruct(state.shape, F32)],
        input_output_aliases={6: 1} if aliased else {},
        compiler_params=_params(1),
        name="ret_sample",
    )(*args)


def _attn_sample_kernel(a0_ref, a1_ref, a2_ref, c0_ref, c1_ref, c2_ref, b0_ref, b1_ref, b2_ref,
                        ao_ref, *, nbb):
    qkv, caches, biases = (a0_ref, a1_ref, a2_ref), (c0_ref, c1_ref, c2_ref), (b0_ref, b1_ref, b2_ref)
    scale = HEAD_DIM ** -0.5
    shape = (2 * HEADS_PER_GROUP, ATT_OUT)
    own = (lax.broadcasted_iota(jnp.int32, shape, 0)
           == lax.broadcasted_iota(jnp.int32, shape, 1) // HEAD_DIM)
    for b in range(nbb):
        n = pl.program_id(0) * nbb + b
        ps, pns, ls, lses = [], [], [], []
        for g in range(N_GROUPS):
            qbd = jnp.where(own, qkv[g][0, 0, pl.ds(n, 1), :], 0.0)
            s = _dot(qbd.astype(BF16), caches[g][b, 0].astype(BF16)) * scale + biases[g][...]
            sn = jnp.sum(qbd * qkv[g][1, 0, pl.ds(n, 1), :], axis=-1, keepdims=True) * scale
            m = jnp.maximum(jnp.max(s, axis=-1, keepdims=True), sn)
            p = jnp.exp(s - m)
            pn = jnp.exp(sn - m)
            l = jnp.sum(p, axis=-1, keepdims=True) + pn
            ps.append(p), pns.append(pn), ls.append(l), lses.append(m + jnp.log(l))
        top = functools.reduce(jnp.maximum, lses)
        es = [jnp.exp(x - top) for x in lses]
        den = functools.reduce(jnp.add, es)
        acc = jnp.zeros(shape, F32)
        for g in range(N_GROUPS):
            og = _dot_nt(ps[g].astype(BF16), caches[g][b, 1].astype(BF16))
            og = og + pns[g] * qkv[g][2, 0, pl.ds(n, 1), :]
            acc = acc + og * (es[g] / (den * ls[g]))
        ao_ref[pl.ds(n, 1), :] = jnp.sum(jnp.where(own, acc, 0.0), axis=0, keepdims=True)


def _attn_sample(qkv, caches, layer, nbb):
    n = qkv[0].shape[2]
    views, cache_specs, biases = [], [], []
    slopes = _alibi_slopes()
    for g, (window, dil) in enumerate(ATT_GROUPS):
        depth, n_c, wlen = caches[g].shape[:3]
        assert n_c == n and wlen == window == ATT_BLOCK * dil, (caches[g].shape, window, dil)
        views.append(caches[g].transpose(0, 1, 3, 4, 5, 2).reshape(depth, n, 2, ATT_OUT, wlen))
        cache_specs.append(pl.BlockSpec((None, nbb, 2, ATT_OUT, wlen),
                                        lambda i: (layer, i, 0, 0, 0)))
        back = wlen - jnp.arange(wlen)
        pen = -slopes[g][:, None] * back.astype(F32)[None, :]
        pen = jnp.where((back % dil == 0)[None, :], pen, MASKED)
        biases.append(jnp.pad(pen, ((0, HEADS_PER_GROUP), (0, 0))))
    return pl.pallas_call(
        functools.partial(_attn_sample_kernel, nbb=nbb),
        grid=(n // nbb,),
        in_specs=[_resident(a.shape) for a in qkv] + cache_specs + [
            _resident(b.shape) for b in biases],
        out_specs=pl.BlockSpec((n, ATT_OUT), lambda i: (0, 0)),
        out_shape=jax.ShapeDtypeStruct((n, ATT_OUT), F32),
        compiler_params=_params(1),
        name="attn_sample",
    )(*qkv, *views, *biases)


def _prep_weights(w_in, w_ret_branch, w_att_branch, w_out, w_gate_up, w_down):
    depth = w_in.shape[0]
    w3 = w_in.astype(BF16).reshape(depth, D_MODEL, N_COL_BLOCKS, COL).transpose(0, 2, 1, 3)
    w3 = jnp.concatenate([w3[:, :12], w3[:, 21:29], w3[:, 12:21]], axis=1)
    wr = w_ret_branch.astype(BF16)
    wa = w_att_branch.astype(BF16)
    wo = w_out.astype(BF16)
    wgu = w_gate_up.astype(BF16).reshape(depth, D_MODEL, 2, FF_BLOCKS, COL).transpose(0, 2, 3, 1, 4)
    wd = w_down.astype(BF16)
    return w3, wr, wa, wo, wgu[:, 0], wgu[:, 1], wd


def _kv_rows(qkv, batch, keep):
    _, dil, rows, _ = qkv.shape
    per_seq = rows // batch
    n = keep // dil
    if per_seq == n:
        a = qkv[1:].reshape(2, dil, batch, n, COL).transpose(2, 0, 1, 3, 4)
    else:
        a = jnp.stack([qkv[1:, :, (b + 1) * per_seq - n:(b + 1) * per_seq] for b in range(batch)])
    a = a.astype(F32).reshape(batch, 2, dil, n, HEADS_PER_GROUP, HEAD_DIM)
    return a.transpose(0, 3, 2, 1, 4, 5).reshape(batch, keep, 2, HEADS_PER_GROUP, HEAD_DIM)


def _block_rows(m, target):
    bm = min(m, target)
    assert m % bm == 0, (m, bm)
    return bm


def kernel(x_prompt, x_sample, state_ret, cache_kv_w128, cache_kv_w512, cache_kv_w2048, norm_mix,
           w_in, w_ret_branch, w_att_branch, w_out, norm_ffn, w_gate_up, w_down, norm_final):
    depth = w_in.shape[0]
    batch, seq, _ = x_prompt.shape
    n_dec, dec_seq, _ = x_sample.shape
    assert dec_seq == 1 and seq % (ATT_BLOCK * ATT_GROUPS[-1][1]) == 0 and seq % RET_CHUNK == 0
    w3, wr, wa, wo, wg, wu, wd = _prep_weights(w_in, w_ret_branch, w_att_branch, w_out,
                                               w_gate_up, w_down)
    caches = (cache_kv_w128, cache_kv_w512, cache_kv_w2048)

    m = batch * seq
    bm = _block_rows(m, 512)
    x = x_prompt.reshape(m, D_MODEL)
    p_ret, p_kv = [], [[] for _ in ATT_GROUPS]
    for l in range(depth):
        proj, *qkv = _in_proj(x, norm_mix[l], w3[l], bm, BF16, DILATIONS)
        ro, s_fin = _ret_prompt(proj.reshape(N_MAIN_BLOCKS, batch, seq, COL))
        p_ret.append(s_fin)
        outs, lses = [], []
        for g, (window, _) in enumerate(ATT_GROUPS):
            o, lse = _attn_prompt(qkv[g], g, batch)
            outs.append(o), lses.append(lse)
            p_kv[g].append(_kv_rows(qkv[g], batch, min(window, seq)))
        x = _mix_out(outs + lses, ro.reshape(4, m, COL), proj, x, wr[l], wa[l], wo[l], bm)
        x = _ffn(x, norm_ffn[l], wg[l], wu[l], wd[l], norm_final, bm, final=(l == depth - 1))
    y_prompt = x.reshape(batch, seq, D_MODEL)

    bs = _block_rows(n_dec, 512)
    nbb_ret, nbb_att = 8, 2
    assert n_dec % nbb_ret == 0 and n_dec % nbb_att == 0
    x = x_sample.reshape(n_dec, D_MODEL)
    s_ret, s_kv = None, [[] for _ in ATT_GROUPS]
    for l in range(depth):
        proj, *qkv = _in_proj(x, norm_mix[l], w3[l], bs, F32, (1,) * N_GROUPS)
        ro, s_ret = _ret_sample(proj, state_ret, l, s_ret, nbb_ret)
        ao = _attn_sample(qkv, caches, l, nbb_att)
        for g in range(N_GROUPS):
            s_kv[g].append(_kv_rows(qkv[g], n_dec, 1))
        x = _mix_out([ao], ro, proj, x, wr[l], wa[l], wo[l], bs)
        x = _ffn(x, norm_ffn[l], wg[l], wu[l], wd[l], norm_final, bs, final=(l == depth - 1))
    y_sample = x.reshape(n_dec, 1, D_MODEL)

    return (y_prompt, y_sample, jnp.stack(p_ret),
            jnp.stack(p_kv[0]), jnp.stack(p_kv[1]), jnp.stack(p_kv[2]),
            s_ret, jnp.stack(s_kv[0]), jnp.stack(s_kv[1]), jnp.stack(s_kv[2]))
```

```python
import functools

import jax
import jax.numpy as jnp
from jax import lax
from jax.experimental import pallas as pl
from jax.experimental.pallas import tpu as pltpu

F32 = jnp.float32
BF16 = jnp.bfloat16

D_MODEL = 1024
RET_HEADS = 4
RET_DK = 128
RET_DV = 256
RET_CHUNK = 128
ATT_GROUPS = ((128, 1), (512, 4), (2048, 16))
N_GROUPS = 3
HEADS_PER_GROUP = 4
ATT_HEADS = N_GROUPS * HEADS_PER_GROUP
HEAD_DIM = 64
ATT_BLOCK = 128
ATT_OUT = HEADS_PER_GROUP * HEAD_DIM
D_FF = 2816
D_IN = 7424
RMS_EPS = 1e-6
GN_EPS = 1e-5

LANES = 128
COL = 256
N_COL_BLOCKS = D_IN // COL
FF_BLOCKS = D_FF // COL
RQ, RK, RV, RG, GR, GA = 0, 2, 4, 8, 12, 16
N_MAIN_BLOCKS = 20
ATT_SRC_BLOCK = 12
MAIN_SRC_BLOCKS = tuple(range(12)) + tuple(range(21, 29))
DILATIONS = tuple(d for _, d in ATT_GROUPS)
ATT_UNITS_PER_STEP = 4
RET_CHUNKS_PER_STEP = 2
MASKED = -1e30
VMEM_LIMIT = 56 * 1024 * 1024


def _params(n_axes):
    return pltpu.CompilerParams(dimension_semantics=("arbitrary",) * n_axes,
                                vmem_limit_bytes=VMEM_LIMIT)


def _dot(a, b):
    return jnp.dot(a, b, preferred_element_type=F32)


def _dot_nt(a, b):
    return lax.dot_general(a, b, (((1,), (1,)), ((), ())), preferred_element_type=F32)


def _dot_tn(a, b):
    return lax.dot_general(a, b, (((0,), (0,)), ((), ())), preferred_element_type=F32)


def _rms(x, g):
    return x * lax.rsqrt(jnp.mean(x * x, axis=-1, keepdims=True) + RMS_EPS) * g


def _silu(x):
    return x * jax.nn.sigmoid(x)


def _group_norm(o):
    mu = jnp.mean(o, axis=-1, keepdims=True)
    d = o - mu
    var = jnp.mean(d * d, axis=-1, keepdims=True)
    return d * lax.rsqrt(var + GN_EPS)


def _resident(shape):
    zeros = (0,) * len(shape)
    return pl.BlockSpec(shape, lambda *_: zeros, pipeline_mode=pl.Buffered(1))


def _inproj_kernel(x_ref, g_ref, w_ref, main_ref, a0_ref, a1_ref, a2_ref, xn_ref, hn_ref, hd_ref,
                   *, dils):
    bm = x_ref.shape[0]
    xn = _rms(x_ref[...], g_ref[...])
    hn_ref[...] = xn.astype(BF16)
    if any(d > 1 for d in dils):
        for c in range(D_MODEL // LANES):
            xn_ref[c] = xn[:, c * LANES:(c + 1) * LANES]

    def w_block(j):
        return w_ref[:, j * COL:(j + 1) * COL]

    for j in range(N_MAIN_BLOCKS):
        main_ref[j] = _dot(hn_ref[...], w_block(MAIN_SRC_BLOCKS[j])).astype(main_ref.dtype)
    for g, (a_ref, d) in enumerate(zip((a0_ref, a1_ref, a2_ref), dils)):
        if d == 1:
            lhs_ref = hn_ref
        else:
            lhs_ref = hd_ref
            n = bm // d
            for r in range(d):
                for c in range(D_MODEL // LANES):
                    hd_ref[r * n:(r + 1) * n, c * LANES:(c + 1) * LANES] = (
                        xn_ref[c, pl.ds(r, n, stride=d), :].astype(BF16))
        for c in range(3):
            out = _dot(lhs_ref[...], w_block(ATT_SRC_BLOCK + c * N_GROUPS + g))
            a_ref[c] = out.reshape(d, bm // d, COL).astype(a_ref.dtype)


def _in_proj(x, g, w, bm, out_dtype, dils):
    m = x.shape[0]
    att_specs = [pl.BlockSpec((3, d, bm // d, COL), lambda i: (0, 0, i, 0)) for d in dils]
    att_shapes = [jax.ShapeDtypeStruct((3, d, m // d, COL), out_dtype) for d in dils]
    return pl.pallas_call(
        functools.partial(_inproj_kernel, dils=dils),
        grid=(m // bm,),
        in_specs=[pl.BlockSpec((bm, D_MODEL), lambda i: (i, 0)),
                  _resident((1, D_MODEL)),
                  _resident((D_MODEL, D_IN))],
        out_specs=[pl.BlockSpec((N_MAIN_BLOCKS, bm, COL), lambda i: (0, i, 0))] + att_specs,
        out_shape=[jax.ShapeDtypeStruct((N_MAIN_BLOCKS, m, COL), out_dtype)] + att_shapes,
        scratch_shapes=[pltpu.VMEM((D_MODEL // LANES, bm, LANES), F32),
                        pltpu.VMEM((bm, D_MODEL), BF16), pltpu.VMEM((bm, D_MODEL), BF16)],
        compiler_params=_params(1),
        name="in_proj",
    )(x, g.reshape(1, D_MODEL), w)


def _ret_prompt_kernel(q_ref, k_ref, v_ref, rg_ref, dmat_ref, inner_ref, tail_ref, gc_ref,
                       ro_ref, s_ref, *, batch):
    @pl.when(pl.program_id(0) == 0)
    def _():
        s_ref[...] = jnp.zeros_like(s_ref)

    chunk = dmat_ref.shape[1]
    for c in range(q_ref.shape[2] // chunk):
        rows = slice(c * chunk, (c + 1) * chunk)
        for b in range(batch):
            for h in range(RET_HEADS):
                lo = (h % 2) * RET_DK
                q = q_ref[h // 2, b, rows, lo:lo + RET_DK]
                k = k_ref[h // 2, b, rows, lo:lo + RET_DK]
                v = v_ref[h, b, rows, :]
                s0 = s_ref[b, h]
                scores = _dot_nt(q, k) * dmat_ref[h]
                o = _dot(scores.astype(BF16), v) + _dot(q, s0.astype(BF16)) * inner_ref[h]
                kt = (k.astype(F32) * tail_ref[h]).astype(BF16)
                s_ref[b, h] = s0 * gc_ref[h] + _dot_tn(kt, v)
                y = _group_norm(o) * _silu(rg_ref[h, b, rows, :].astype(F32))
                ro_ref[h, b, rows, :] = y.astype(ro_ref.dtype)


def _ret_tables(chunk):
    log_gamma = jnp.log1p(-jnp.exp2(-5.0 - jnp.arange(RET_HEADS, dtype=F32)))
    scale = RET_DK ** -0.5
    idx = jnp.arange(chunk, dtype=F32)
    diff = idx[:, None] - idx[None, :]
    dmat = jnp.where(diff >= 0, jnp.exp(jnp.maximum(diff, 0.0)[None] * log_gamma[:, None, None]), 0.0) * scale
    inner = jnp.exp((idx + 1.0)[None, :] * log_gamma[:, None]) * scale
    tail = jnp.exp((chunk - 1.0 - idx)[None, :] * log_gamma[:, None])
    gc = jnp.exp(chunk * log_gamma)
    inner = jnp.broadcast_to(inner[:, :, None], (RET_HEADS, chunk, RET_DV))
    tail = jnp.broadcast_to(tail[:, :, None], (RET_HEADS, chunk, RET_DK))
    gc = jnp.broadcast_to(gc[:, None, None], (RET_HEADS, 1, RET_DV))
    return dmat, inner, tail, gc


def _ret_prompt(proj4):
    _, b, t, _ = proj4.shape
    dmat, inner, tail, gc = _ret_tables(RET_CHUNK)
    c = RET_CHUNK * RET_CHUNKS_PER_STEP
    assert t % c == 0
    return pl.pallas_call(
        functools.partial(_ret_prompt_kernel, batch=b),
        grid=(t // c,),
        in_specs=[pl.BlockSpec((2, b, c, COL), lambda i: (RQ // 2, 0, i, 0)),
                  pl.BlockSpec((2, b, c, COL), lambda i: (RK // 2, 0, i, 0)),
                  pl.BlockSpec((4, b, c, COL), lambda i: (RV // 4, 0, i, 0)),
                  pl.BlockSpec((4, b, c, COL), lambda i: (RG // 4, 0, i, 0)),
                  _resident(dmat.shape), _resident(inner.shape), _resident(tail.shape),
                  _resident(gc.shape)],
        out_specs=[pl.BlockSpec((4, b, c, COL), lambda i: (0, 0, i, 0)),
                   pl.BlockSpec((b, RET_HEADS, RET_DK, RET_DV), lambda i: (0, 0, 0, 0))],
        out_shape=[jax.ShapeDtypeStruct((4, b, t, COL), BF16),
                   jax.ShapeDtypeStruct((b, RET_HEADS, RET_DK, RET_DV), F32)],
        compiler_params=_params(1),
        name="ret_prompt",
    )(proj4, proj4, proj4, proj4, dmat, inner, tail, gc)


def _attn_prompt_kernel(q_ref, kc_ref, kp_ref, vc_ref, vp_ref, bias_ref, o_ref, lse_ref):
    nres, rows_per_step, _ = q_ref.shape
    first_step = jnp.where(pl.program_id(2) == 0, 0, 1)
    lane_head = lax.broadcasted_iota(jnp.int32, (ATT_BLOCK, ATT_OUT), 1) // HEAD_DIM
    scale = jnp.asarray(HEAD_DIM ** -0.5, BF16)
    for r in range(nres):
        for j in range(rows_per_step // ATT_BLOCK):
            blk = slice(j * ATT_BLOCK, (j + 1) * ATT_BLOCK)
            q = q_ref[r, blk, :] * scale
            qs = jnp.concatenate([jnp.where(lane_head == h, q, jnp.zeros_like(q))
                                  for h in range(HEADS_PER_GROUP)], axis=0)
            if j == 0:
                k_prev, v_prev, bias = kp_ref[r], vp_ref[r], bias_ref[first_step]
            else:
                before = slice((j - 1) * ATT_BLOCK, j * ATT_BLOCK)
                k_prev, v_prev, bias = kc_ref[r, before, :], vc_ref[r, before, :], bias_ref[1]
            kcat = jnp.concatenate([k_prev, kc_ref[r, blk, :]], axis=0)
            vcat = jnp.concatenate([v_prev, vc_ref[r, blk, :]], axis=0)
            s = _dot_nt(qs, kcat) + bias
            m = jnp.max(s, axis=-1, keepdims=True)
            p = jnp.exp(s - m)
            l = jnp.sum(p, axis=-1, keepdims=True)
            on = _dot(p.astype(BF16), vcat) / l
            lse = m + jnp.log(l)
            o = jnp.zeros((ATT_BLOCK, ATT_OUT), F32)
            ls = jnp.zeros((ATT_BLOCK, ATT_OUT), F32)
            for h in range(HEADS_PER_GROUP):
                rows = slice(h * ATT_BLOCK, (h + 1) * ATT_BLOCK)
                o = jnp.where(lane_head == h, on[rows], o)
                ls = jnp.where(lane_head == h, lse[rows], ls)
            o_ref[r, blk, :] = o.astype(o_ref.dtype)
            lse_ref[r, blk, :] = ls


def _alibi_slopes():
    return jnp.exp2(-8.0 * (jnp.arange(ATT_HEADS, dtype=F32) + 1.0) / ATT_HEADS).reshape(
        N_GROUPS, HEADS_PER_GROUP)


def _attn_prompt_bias(g):
    window, dilation = ATT_GROUPS[g]
    steps = window // dilation
    qi = jnp.arange(ATT_BLOCK)[:, None]
    kj = jnp.arange(2 * ATT_BLOCK)[None, :]
    dist = qi + ATT_BLOCK - kj
    valid = (dist >= 0) & (dist <= steps)
    pen = -_alibi_slopes()[g][:, None, None] * (dist * dilation).astype(F32)[None]
    later = jnp.where(valid[None], pen, MASKED)
    first = jnp.where((valid & (kj >= ATT_BLOCK))[None], pen, MASKED)
    return jnp.stack([first, later]).reshape(2, HEADS_PER_GROUP * ATT_BLOCK, 2 * ATT_BLOCK)


def _attn_prompt(qkv, g, batch):
    _, dil, rows, _ = qkv.shape
    nres = min(dil, ATT_UNITS_PER_STEP)
    nblk = ATT_UNITS_PER_STEP // nres
    per_seq = rows // batch // ATT_BLOCK
    assert dil % nres == 0 and per_seq % nblk == 0
    steps = per_seq // nblk
    bias = _attn_prompt_bias(g)

    def cur(c):
        return pl.BlockSpec((None, nres, nblk * ATT_BLOCK, COL),
                            lambda b, r, n: (c, r, b * steps + n, 0))

    def prev(c):
        return pl.BlockSpec((None, nres, ATT_BLOCK, COL),
                            lambda b, r, n: (c, r, b * per_seq + jnp.maximum(n * nblk - 1, 0), 0))

    out_spec = pl.BlockSpec((nres, nblk * ATT_BLOCK, COL), lambda b, r, n: (r, b * steps + n, 0))
    return pl.pallas_call(
        _attn_prompt_kernel,
        grid=(batch, dil // nres, steps),
        in_specs=[cur(0), cur(1), prev(1), cur(2), prev(2), _resident(bias.shape)],
        out_specs=[out_spec, out_spec],
        out_shape=[jax.ShapeDtypeStruct((dil, rows, COL), BF16),
                   jax.ShapeDtypeStruct((dil, rows, COL), F32)],
        compiler_params=_params(3),
        name=f"attn_prompt_g{g}",
    )(qkv, qkv, qkv, qkv, qkv, bias)


def _token_order(ref, scr_ref):
    d, n, _ = ref.shape
    if d == 1:
        return ref[0].astype(F32)
    for r in range(d):
        part = ref[r].astype(F32)
        for c in range(COL // LANES):
            scr_ref[c, pl.ds(r, n, stride=d), :] = part[:, c * LANES:(c + 1) * LANES]
    return jnp.concatenate([scr_ref[c] for c in range(COL // LANES)], axis=1)


def _mix_out_kernel(*refs, n_att):
    att = refs[:n_att]
    ro_ref, gr_ref, ga_ref, x_ref, wr_ref, wa_ref, wo_ref, o_ref = refs[n_att:n_att + 8]
    scratch = refs[n_att + 8:]
    if n_att == 1:
        ao = att[0][...]
    else:
        vals = [_token_order(r, s) for r, s in zip(att, scratch)]
        outs, lses = vals[:N_GROUPS], vals[N_GROUPS:]
        top = functools.reduce(jnp.maximum, lses)
        es = [jnp.exp(x - top) for x in lses]
        den = functools.reduce(jnp.add, es)
        ao = functools.reduce(jnp.add, [e * o for e, o in zip(es, outs)]) / den
    bm = x_ref.shape[0]
    n_parts = 2 if bm % 32 == 0 else 1
    for part in range(n_parts):
        rows = slice(part * bm // n_parts, (part + 1) * bm // n_parts)

        def token_major(ref):
            return jnp.concatenate([ref[c, rows, :] for c in range(ref.shape[0])], axis=1)

        a_br = _dot(ao[rows].astype(BF16), wa_ref[...])
        r_br = _dot(token_major(ro_ref).astype(BF16), wr_ref[...])
        merged = (jax.nn.sigmoid(token_major(gr_ref).astype(F32)) * r_br
                  + jax.nn.sigmoid(token_major(ga_ref).astype(F32)) * a_br)
        o_ref[rows, :] = x_ref[rows, :] + _dot(merged.astype(BF16), wo_ref[...])


def _mix_out(att, ro, proj, x, wr, wa, wo, bm):
    m = x.shape[0]
    if len(att) == 1:
        att_specs = [pl.BlockSpec((bm, COL), lambda i: (i, 0))]
    else:
        att_specs = [pl.BlockSpec((a.shape[0], bm // a.shape[0], COL), lambda i: (0, i, 0))
                     for a in att]
    scratch = [pltpu.VMEM((COL // LANES, bm, LANES), F32) for a in att if a.ndim == 3]
    return pl.pallas_call(
        functools.partial(_mix_out_kernel, n_att=len(att)),
        grid=(m // bm,),
        scratch_shapes=scratch,
        in_specs=att_specs + [
            pl.BlockSpec((4, bm, COL), lambda i: (0, i, 0)),
            pl.BlockSpec((4, bm, COL), lambda i: (GR // 4, i, 0)),
            pl.BlockSpec((4, bm, COL), lambda i: (GA // 4, i, 0)),
            pl.BlockSpec((bm, D_MODEL), lambda i: (i, 0)),
            _resident(wr.shape), _resident(wa.shape), _resident(wo.shape)],
        out_specs=pl.BlockSpec((bm, D_MODEL), lambda i: (i, 0)),
        out_shape=jax.ShapeDtypeStruct((m, D_MODEL), F32),
        compiler_params=_params(1),
        name="mix_out",
    )(*att, ro, proj, proj, x, wr, wa, wo)


def _sample_attention_row(qkv, kv, biases, n):
    scale = HEAD_DIM ** -0.5
    shape = (2 * HEADS_PER_GROUP, ATT_OUT)
    own = (lax.broadcasted_iota(jnp.int32, shape, 0)
           == lax.broadcasted_iota(jnp.int32, shape, 1) // HEAD_DIM)
    ps, pns, ls, lses = [], [], [], []
    for g in range(N_GROUPS):
        qbd = jnp.where(own, qkv[g][0, 0, pl.ds(n, 1), :], 0.0)
        s = _dot(qbd.astype(BF16), kv[g][0].astype(BF16)) * scale + biases[g][...]
        sn = jnp.sum(qbd * qkv[g][1, 0, pl.ds(n, 1), :], axis=-1, keepdims=True) * scale
        m = jnp.maximum(jnp.max(s, axis=-1, keepdims=True), sn)
        p = jnp.exp(s - m)
        pn = jnp.exp(sn - m)
        l = jnp.sum(p, axis=-1, keepdims=True) + pn
        ps.append(p), pns.append(pn), ls.append(l), lses.append(m + jnp.log(l))
    top = functools.reduce(jnp.maximum, lses)
    es = [jnp.exp(x - top) for x in lses]
    den = functools.reduce(jnp.add, es)
    acc = jnp.zeros(shape, F32)
    for g in range(N_GROUPS):
        og = _dot_nt(ps[g].astype(BF16), kv[g][1].astype(BF16))
        og = og + pns[g] * qkv[g][2, 0, pl.ds(n, 1), :]
        acc = acc + og * (es[g] / (den * ls[g]))
    return jnp.sum(jnp.where(own, acc, 0.0), axis=0, keepdims=True)


def _ffn_kernel(*refs, final, hosted):
    x_ref, g_ref, wgu_ref, wd_ref, gf_ref = refs[:5]
    if hosted is None:
        o_ref, hn_ref, act_ref = refs[5:]
        seqs = 0
    else:
        layer, seqs = hosted
        qkv, caches, biases = refs[5:8], refs[8:11], refs[11:14]
        o_ref, ao_ref, hn_ref, act_ref = refs[14:18]
        bufs, sem = refs[18:21], refs[21]
        step, n_steps = pl.program_id(0), pl.num_programs(0)

        def copies(n):
            return [pltpu.make_async_copy(caches[g].at[layer, n], bufs[g].at[n % 2], sem.at[g, n % 2])
                    for g in range(N_GROUPS)]

        @pl.when(step == 0)
        def _():
            for c in copies(0):
                c.start()

        def attend(u):
            n = step * seqs + u
            for c in copies(n):
                c.wait()
            if u + 1 < seqs:
                for c in copies(n + 1):
                    c.start()
            else:
                @pl.when(step + 1 < n_steps)
                def _():
                    for c in copies(n + 1):
                        c.start()
            kv = [bufs[g].at[n % 2] for g in range(N_GROUPS)]
            ao_ref[pl.ds(n, 1), :] = _sample_attention_row(qkv, kv, biases, n)

    x = x_ref[...]
    hn_ref[...] = _rms(x, g_ref[...]).astype(BF16)
    attend_before = {-(-u * FF_BLOCKS // seqs): u for u in range(seqs)}
    for c in range(FF_BLOCKS):
        if c in attend_before:
            attend(attend_before[c])
        gate = _dot(hn_ref[...], wgu_ref[:, c * COL:(c + 1) * COL])
        up = _dot(hn_ref[...], wgu_ref[:, D_FF + c * COL:D_FF + (c + 1) * COL])
        act_ref[:, c * COL:(c + 1) * COL] = (_silu(gate) * up).astype(BF16)
    y = x + _dot(act_ref[...], wd_ref[...])
    o_ref[...] = _rms(y, gf_ref[...]) if final else y


def _sample_cache_views(caches):
    views, biases = [], []
    slopes = _alibi_slopes()
    for g, (window, dil) in enumerate(ATT_GROUPS):
        depth, n, wlen = caches[g].shape[:3]
        assert wlen == window == ATT_BLOCK * dil, (caches[g].shape, window, dil)
        views.append(caches[g].transpose(0, 1, 3, 4, 5, 2).reshape(depth, n, 2, ATT_OUT, wlen))
        back = wlen - jnp.arange(wlen)
        pen = -slopes[g][:, None] * back.astype(F32)[None, :]
        pen = jnp.where((back % dil == 0)[None, :], pen, MASKED)
        biases.append(jnp.pad(pen, ((0, HEADS_PER_GROUP), (0, 0))))
    return views, biases


def _ffn(x, g, wgu, wd, gf, bm, final, hosted=None):
    m = x.shape[0]
    steps = m // bm
    in_specs = [pl.BlockSpec((bm, D_MODEL), lambda i: (i, 0)),
                _resident((1, D_MODEL)),
                _resident(wgu.shape), _resident(wd.shape),
                _resident((1, D_MODEL))]
    args = [x, g.reshape(1, D_MODEL), wgu, wd, gf.reshape(1, D_MODEL)]
    out_specs = [pl.BlockSpec((bm, D_MODEL), lambda i: (i, 0))]
    out_shape = [jax.ShapeDtypeStruct((m, D_MODEL), F32)]
    scratch = [pltpu.VMEM((bm, D_MODEL), BF16), pltpu.VMEM((bm, D_FF), BF16)]
    kernel_hosted = None
    if hosted is not None:
        layer, qkv, views, biases = hosted
        n = qkv[0].shape[2]
        assert n % steps == 0 and n // steps <= FF_BLOCKS, (n, steps)
        kernel_hosted = (layer, n // steps)
        in_specs += ([_resident(a.shape) for a in qkv]
                     + [pl.BlockSpec(memory_space=pl.ANY) for _ in views]
                     + [_resident(b.shape) for b in biases])
        args += [*qkv, *views, *biases]
        out_specs.append(pl.BlockSpec((n, ATT_OUT), lambda i: (0, 0)))
        out_shape.append(jax.ShapeDtypeStruct((n, ATT_OUT), F32))
        scratch += [pltpu.VMEM((2,) + v.shape[2:], F32) for v in views]
        scratch.append(pltpu.SemaphoreType.DMA((N_GROUPS, 2)))
    out = pl.pallas_call(
        functools.partial(_ffn_kernel, final=final, hosted=kernel_hosted),
        grid=(steps,),
        in_specs=in_specs,
        out_specs=out_specs,
        out_shape=out_shape,
        scratch_shapes=scratch,
        compiler_params=_params(1),
        name="ffn",
    )(*args)
    return out if hosted is not None else out[0]


def _ret_sample_kernel(q_ref, k_ref, v_ref, rg_ref, s0_ref, gam_ref, *rest, nbb, aliased):
    ro_ref, s1_ref = rest[1:] if aliased else rest
    for h in range(RET_HEADS):
        lo = (h % 2) * RET_DK
        qt = q_ref[h // 2, :, lo:lo + RET_DK].T
        kt = k_ref[h // 2, :, lo:lo + RET_DK].T
        for b in range(nbb):
            s1 = s0_ref[b, h] * gam_ref[h] + kt[:, b:b + 1] * v_ref[h, b:b + 1, :]
            s1_ref[b, h] = s1
            ro_ref[h, b:b + 1, :] = jnp.sum(s1 * qt[:, b:b + 1], axis=0, keepdims=True)
    for h in range(RET_HEADS):
        o = ro_ref[h] * (RET_DK ** -0.5)
        ro_ref[h] = _group_norm(o) * _silu(rg_ref[h])


def _ret_sample(proj, state, layer, s_prev, nbb):
    n = proj.shape[1]
    gam = jnp.exp(jnp.log1p(-jnp.exp2(-5.0 - jnp.arange(RET_HEADS, dtype=F32))))
    gam = jnp.broadcast_to(gam[:, None, None], (RET_HEADS, 1, RET_DV))
    st_spec = pl.BlockSpec((None, nbb, RET_HEADS, RET_DK, RET_DV), lambda i: (layer, i, 0, 0, 0))
    aliased = s_prev is not None
    in_specs = [pl.BlockSpec((2, nbb, COL), lambda i: (RQ // 2, i, 0)),
                pl.BlockSpec((2, nbb, COL), lambda i: (RK // 2, i, 0)),
                pl.BlockSpec((4, nbb, COL), lambda i: (RV // 4, i, 0)),
                pl.BlockSpec((4, nbb, COL), lambda i: (RG // 4, i, 0)),
                st_spec, _resident(gam.shape)]
    args = [proj, proj, proj, proj, state, gam]
    if aliased:
        in_specs.append(pl.BlockSpec(memory_space=pl.ANY))
        args.append(s_prev)
    return pl.pallas_call(
        functools.partial(_ret_sample_kernel, nbb=nbb, aliased=aliased),
        grid=(n // nbb,),
        in_specs=in_specs,
        out_specs=[pl.BlockSpec((4, nbb, COL), lambda i: (0, i, 0)), st_spec],
        out_shape=[jax.ShapeDtypeStruct((4, n, COL), F32),
                   jax.ShapeDtypeStruct(state.shape, F32)],
        input_output_aliases={6: 1} if aliased else {},
        compiler_params=_params(1),
        name="ret_sample",
    )(*args)


def _prep_weights(w_in, w_ret_branch, w_att_branch, w_out, w_gate_up, w_down):
    return tuple(w.astype(BF16) for w in (w_in, w_ret_branch, w_att_branch, w_out, w_gate_up, w_down))


def _kv_rows(qkv, batch, keep):
    _, dil, rows, _ = qkv.shape
    per_seq = rows // batch
    n = keep // dil
    if per_seq == n:
        a = qkv[1:].reshape(2, dil, batch, n, COL).transpose(2, 0, 1, 3, 4)
    else:
        a = jnp.stack([qkv[1:, :, (b + 1) * per_seq - n:(b + 1) * per_seq] for b in range(batch)])
    a = a.astype(F32).reshape(batch, 2, dil, n, HEADS_PER_GROUP, HEAD_DIM)
    return a.transpose(0, 3, 2, 1, 4, 5).reshape(batch, keep, 2, HEADS_PER_GROUP, HEAD_DIM)


def _block_rows(m, target):
    bm = min(m, target)
    assert m % bm == 0, (m, bm)
    return bm


def kernel(x_prompt, x_sample, state_ret, cache_kv_w128, cache_kv_w512, cache_kv_w2048, norm_mix,
           w_in, w_ret_branch, w_att_branch, w_out, norm_ffn, w_gate_up, w_down, norm_final):
    depth = w_in.shape[0]
    batch, seq, _ = x_prompt.shape
    n_dec, dec_seq, _ = x_sample.shape
    assert dec_seq == 1 and seq % (ATT_BLOCK * ATT_GROUPS[-1][1]) == 0 and seq % RET_CHUNK == 0
    wi, wr, wa, wo, wgu, wd = _prep_weights(w_in, w_ret_branch, w_att_branch, w_out,
                                            w_gate_up, w_down)
    cache_views, cache_biases = _sample_cache_views((cache_kv_w128, cache_kv_w512, cache_kv_w2048))

    m = batch * seq
    bm = _block_rows(m, 512)
    bs = _block_rows(n_dec, 512)
    nbb_ret = 8
    assert n_dec % nbb_ret == 0
    xp = x_prompt.reshape(m, D_MODEL)
    xs = x_sample.reshape(n_dec, D_MODEL)
    p_ret, p_kv = [], [[] for _ in ATT_GROUPS]
    s_ret, s_kv = None, [[] for _ in ATT_GROUPS]
    for l in range(depth):
        last = l == depth - 1
        proj_s, *qkv_s = _in_proj(xs, norm_mix[l], wi[l], bs, F32, (1,) * N_GROUPS)
        ro_s, s_ret = _ret_sample(proj_s, state_ret, l, s_ret, nbb_ret)
        for g in range(N_GROUPS):
            s_kv[g].append(_kv_rows(qkv_s[g], n_dec, 1))
        proj, *qkv = _in_proj(xp, norm_mix[l], wi[l], bm, BF16, DILATIONS)
        ro, s_fin = _ret_prompt(proj.reshape(N_MAIN_BLOCKS, batch, seq, COL))
        p_ret.append(s_fin)
        outs, lses = [], []
        for g, (window, _) in enumerate(ATT_GROUPS):
            o, lse = _attn_prompt(qkv[g], g, batch)
            outs.append(o), lses.append(lse)
            p_kv[g].append(_kv_rows(qkv[g], batch, min(window, seq)))
        xp = _mix_out(outs + lses, ro.reshape(4, m, COL), proj, xp, wr[l], wa[l], wo[l], bm)
        xp, ao_s = _ffn(xp, norm_ffn[l], wgu[l], wd[l], norm_final, bm, last,
                        hosted=(l, qkv_s, cache_views, cache_biases))
        xs = _mix_out([ao_s], ro_s, proj_s, xs, wr[l], wa[l], wo[l], bs)
        xs = _ffn(xs, norm_ffn[l], wgu[l], wd[l], norm_final, bs, last)
    y_prompt = xp.reshape(batch, seq, D_MODEL)
    y_sample = xs.reshape(n_dec, 1, D_MODEL)

    return (y_prompt, y_sample, jnp.stack(p_ret),
            jnp.stack(p_kv[0]), jnp.stack(p_kv[1]), jnp.stack(p_kv[2]),
            s_ret, jnp.stack(s_kv[0]), jnp.stack(s_kv[1]), jnp.stack(s_kv[2]))
```

```python
import functools

import jax
import jax.numpy as jnp
from jax import lax
from jax.experimental import pallas as pl
from jax.experimental.pallas import tpu as pltpu

F32 = jnp.float32
BF16 = jnp.bfloat16

D_MODEL = 1024
RET_HEADS = 4
RET_DK = 128
RET_DV = 256
RET_CHUNK = 128
ATT_GROUPS = ((128, 1), (512, 4), (2048, 16))
N_GROUPS = 3
HEADS_PER_GROUP = 4
ATT_HEADS = N_GROUPS * HEADS_PER_GROUP
HEAD_DIM = 64
ATT_BLOCK = 128
ATT_OUT = HEADS_PER_GROUP * HEAD_DIM
D_FF = 2816
D_IN = 7424
RMS_EPS = 1e-6
GN_EPS = 1e-5

LANES = 128
COL = 256
N_COL_BLOCKS = D_IN // COL
FF_BLOCKS = D_FF // COL
RQ, RK, RV, RG, GR, GA = 0, 2, 4, 8, 12, 16
N_MAIN_BLOCKS = 20
ATT_SRC_BLOCK = 12
MAIN_SRC_BLOCKS = tuple(range(12)) + tuple(range(21, 29))
DILATIONS = tuple(d for _, d in ATT_GROUPS)
ATT_UNITS_PER_STEP = 4
RET_CHUNKS_PER_STEP = 2
MASKED = -1e30
VMEM_LIMIT = 56 * 1024 * 1024


def _params(n_axes):
    return pltpu.CompilerParams(dimension_semantics=("arbitrary",) * n_axes,
                                vmem_limit_bytes=VMEM_LIMIT)


def _dot(a, b):
    return jnp.dot(a, b, preferred_element_type=F32)


def _dot_nt(a, b):
    return lax.dot_general(a, b, (((1,), (1,)), ((), ())), preferred_element_type=F32)


def _dot_tn(a, b):
    return lax.dot_general(a, b, (((0,), (0,)), ((), ())), preferred_element_type=F32)


def _rms(x, g):
    return x * lax.rsqrt(jnp.mean(x * x, axis=-1, keepdims=True) + RMS_EPS) * g


def _silu(x):
    return x * jax.nn.sigmoid(x)


def _group_norm(o):
    mu = jnp.mean(o, axis=-1, keepdims=True)
    d = o - mu
    var = jnp.mean(d * d, axis=-1, keepdims=True)
    return d * lax.rsqrt(var + GN_EPS)


def _resident(shape):
    zeros = (0,) * len(shape)
    return pl.BlockSpec(shape, lambda *_: zeros, pipeline_mode=pl.Buffered(1))


def _inproj_kernel(x_ref, g_ref, w_ref, main_ref, a0_ref, a1_ref, a2_ref, xn_ref, hn_ref, hd_ref,
                   *, dils):
    bm = x_ref.shape[0]
    xn = _rms(x_ref[...], g_ref[...])
    hn_ref[...] = xn.astype(BF16)
    if any(d > 1 for d in dils):
        for c in range(D_MODEL // LANES):
            xn_ref[c] = xn[:, c * LANES:(c + 1) * LANES]

    def w_block(j):
        return w_ref[:, j * COL:(j + 1) * COL]

    for j in range(N_MAIN_BLOCKS):
        main_ref[j] = _dot(hn_ref[...], w_block(MAIN_SRC_BLOCKS[j])).astype(main_ref.dtype)
    for g, (a_ref, d) in enumerate(zip((a0_ref, a1_ref, a2_ref), dils)):
        if d == 1:
            lhs_ref = hn_ref
        else:
            lhs_ref = hd_ref
            n = bm // d
            for r in range(d):
                for c in range(D_MODEL // LANES):
                    hd_ref[r * n:(r + 1) * n, c * LANES:(c + 1) * LANES] = (
                        xn_ref[c, pl.ds(r, n, stride=d), :].astype(BF16))
        for c in range(3):
            out = _dot(lhs_ref[...], w_block(ATT_SRC_BLOCK + c * N_GROUPS + g))
            a_ref[c] = out.reshape(d, bm // d, COL).astype(a_ref.dtype)


def _in_proj(x, g, w, bm, out_dtype, dils):
    m = x.shape[0]
    att_specs = [pl.BlockSpec((3, d, bm // d, COL), lambda i: (0, 0, i, 0)) for d in dils]
    att_shapes = [jax.ShapeDtypeStruct((3, d, m // d, COL), out_dtype) for d in dils]
    return pl.pallas_call(
        functools.partial(_inproj_kernel, dils=dils),
        grid=(m // bm,),
        in_specs=[pl.BlockSpec((bm, D_MODEL), lambda i: (i, 0)),
                  _resident((1, D_MODEL)),
                  _resident((D_MODEL, D_IN))],
        out_specs=[pl.BlockSpec((N_MAIN_BLOCKS, bm, COL), lambda i: (0, i, 0))] + att_specs,
        out_shape=[jax.ShapeDtypeStruct((N_MAIN_BLOCKS, m, COL), out_dtype)] + att_shapes,
        scratch_shapes=[pltpu.VMEM((D_MODEL // LANES, bm, LANES), F32),
                        pltpu.VMEM((bm, D_MODEL), BF16), pltpu.VMEM((bm, D_MODEL), BF16)],
        compiler_params=_params(1),
        name="in_proj",
    )(x, g.reshape(1, D_MODEL), w)


def _ret_prompt_kernel(q_ref, k_ref, v_ref, rg_ref, dmat_ref, inner_ref, tail_ref, gc_ref,
                       ro_ref, s_ref, *, batch):
    @pl.when(pl.program_id(0) == 0)
    def _():
        s_ref[...] = jnp.zeros_like(s_ref)

    chunk = dmat_ref.shape[1]
    for c in range(q_ref.shape[2] // chunk):
        rows = slice(c * chunk, (c + 1) * chunk)
        for b in range(batch):
            for h in range(RET_HEADS):
                lo = (h % 2) * RET_DK
                q = q_ref[h // 2, b, rows, lo:lo + RET_DK]
                k = k_ref[h // 2, b, rows, lo:lo + RET_DK]
                v = v_ref[h, b, rows, :]
                s0 = s_ref[b, h]
                scores = _dot_nt(q, k) * dmat_ref[h]
                o = _dot(scores.astype(BF16), v) + _dot(q, s0.astype(BF16)) * inner_ref[h]
                kt = (k.astype(F32) * tail_ref[h]).astype(BF16)
                s_ref[b, h] = s0 * gc_ref[h] + _dot_tn(kt, v)
                y = _group_norm(o) * _silu(rg_ref[h, b, rows, :].astype(F32))
                ro_ref[h, b, rows, :] = y.astype(ro_ref.dtype)


def _ret_tables(chunk):
    log_gamma = jnp.log1p(-jnp.exp2(-5.0 - jnp.arange(RET_HEADS, dtype=F32)))
    scale = RET_DK ** -0.5
    idx = jnp.arange(chunk, dtype=F32)
    diff = idx[:, None] - idx[None, :]
    dmat = jnp.where(diff >= 0, jnp.exp(jnp.maximum(diff, 0.0)[None] * log_gamma[:, None, None]), 0.0) * scale
    inner = jnp.exp((idx + 1.0)[None, :] * log_gamma[:, None]) * scale
    tail = jnp.exp((chunk - 1.0 - idx)[None, :] * log_gamma[:, None])
    gc = jnp.exp(chunk * log_gamma)
    inner = jnp.broadcast_to(inner[:, :, None], (RET_HEADS, chunk, RET_DV))
    tail = jnp.broadcast_to(tail[:, :, None], (RET_HEADS, chunk, RET_DK))
    gc = jnp.broadcast_to(gc[:, None, None], (RET_HEADS, 1, RET_DV))
    return dmat, inner, tail, gc


def _ret_prompt(proj4):
    _, b, t, _ = proj4.shape
    dmat, inner, tail, gc = _ret_tables(RET_CHUNK)
    c = RET_CHUNK * RET_CHUNKS_PER_STEP
    assert t % c == 0
    return pl.pallas_call(
        functools.partial(_ret_prompt_kernel, batch=b),
        grid=(t // c,),
        in_specs=[pl.BlockSpec((2, b, c, COL), lambda i: (RQ // 2, 0, i, 0)),
                  pl.BlockSpec((2, b, c, COL), lambda i: (RK // 2, 0, i, 0)),
                  pl.BlockSpec((4, b, c, COL), lambda i: (RV // 4, 0, i, 0)),
                  pl.BlockSpec((4, b, c, COL), lambda i: (RG // 4, 0, i, 0)),
                  _resident(dmat.shape), _resident(inner.shape), _resident(tail.shape),
                  _resident(gc.shape)],
        out_specs=[pl.BlockSpec((4, b, c, COL), lambda i: (0, 0, i, 0)),
                   pl.BlockSpec((b, RET_HEADS, RET_DK, RET_DV), lambda i: (0, 0, 0, 0))],
        out_shape=[jax.ShapeDtypeStruct((4, b, t, COL), BF16),
                   jax.ShapeDtypeStruct((b, RET_HEADS, RET_DK, RET_DV), F32)],
        compiler_params=_params(1),
        name="ret_prompt",
    )(proj4, proj4, proj4, proj4, dmat, inner, tail, gc)


def _attn_prompt_kernel(q_ref, kc_ref, kp_ref, vc_ref, vp_ref, bias_ref, o_ref, lse_ref):
    nres, rows_per_step, _ = q_ref.shape
    first_step = jnp.where(pl.program_id(2) == 0, 0, 1)
    lane_head = lax.broadcasted_iota(jnp.int32, (ATT_BLOCK, ATT_OUT), 1) // HEAD_DIM
    scale = jnp.asarray(HEAD_DIM ** -0.5, BF16)
    for r in range(nres):
        for j in range(rows_per_step // ATT_BLOCK):
            blk = slice(j * ATT_BLOCK, (j + 1) * ATT_BLOCK)
            q = q_ref[r, blk, :] * scale
            qs = jnp.concatenate([jnp.where(lane_head == h, q, jnp.zeros_like(q))
                                  for h in range(HEADS_PER_GROUP)], axis=0)
            if j == 0:
                k_prev, v_prev, bias = kp_ref[r], vp_ref[r], bias_ref[first_step]
            else:
                before = slice((j - 1) * ATT_BLOCK, j * ATT_BLOCK)
                k_prev, v_prev, bias = kc_ref[r, before, :], vc_ref[r, before, :], bias_ref[1]
            kcat = jnp.concatenate([k_prev, kc_ref[r, blk, :]], axis=0)
            vcat = jnp.concatenate([v_prev, vc_ref[r, blk, :]], axis=0)
            s = _dot_nt(qs, kcat) + bias
            m = jnp.max(s, axis=-1, keepdims=True)
            p = jnp.exp(s - m)
            l = jnp.sum(p, axis=-1, keepdims=True)
            on = _dot(p.astype(BF16), vcat) / l
            lse = m + jnp.log(l)
            o = jnp.zeros((ATT_BLOCK, ATT_OUT), F32)
            ls = jnp.zeros((ATT_BLOCK, ATT_OUT), F32)
            for h in range(HEADS_PER_GROUP):
                rows = slice(h * ATT_BLOCK, (h + 1) * ATT_BLOCK)
                o = jnp.where(lane_head == h, on[rows], o)
                ls = jnp.where(lane_head == h, lse[rows], ls)
            o_ref[r, blk, :] = o.astype(o_ref.dtype)
            lse_ref[r, blk, :] = ls


def _alibi_slopes():
    return jnp.exp2(-8.0 * (jnp.arange(ATT_HEADS, dtype=F32) + 1.0) / ATT_HEADS).reshape(
        N_GROUPS, HEADS_PER_GROUP)


def _attn_prompt_bias(g):
    window, dilation = ATT_GROUPS[g]
    steps = window // dilation
    qi = jnp.arange(ATT_BLOCK)[:, None]
    kj = jnp.arange(2 * ATT_BLOCK)[None, :]
    dist = qi + ATT_BLOCK - kj
    valid = (dist >= 0) & (dist <= steps)
    pen = -_alibi_slopes()[g][:, None, None] * (dist * dilation).astype(F32)[None]
    later = jnp.where(valid[None], pen, MASKED)
    first = jnp.where((valid & (kj >= ATT_BLOCK))[None], pen, MASKED)
    return jnp.stack([first, later]).reshape(2, HEADS_PER_GROUP * ATT_BLOCK, 2 * ATT_BLOCK)


def _attn_prompt(qkv, g, batch):
    _, dil, rows, _ = qkv.shape
    nres = min(dil, ATT_UNITS_PER_STEP)
    nblk = ATT_UNITS_PER_STEP // nres
    per_seq = rows // batch // ATT_BLOCK
    assert dil % nres == 0 and per_seq % nblk == 0
    steps = per_seq // nblk
    bias = _attn_prompt_bias(g)

    def cur(c):
        return pl.BlockSpec((None, nres, nblk * ATT_BLOCK, COL),
                            lambda b, r, n: (c, r, b * steps + n, 0))

    def prev(c):
        return pl.BlockSpec((None, nres, ATT_BLOCK, COL),
                            lambda b, r, n: (c, r, b * per_seq + jnp.maximum(n * nblk - 1, 0), 0))

    out_spec = pl.BlockSpec((nres, nblk * ATT_BLOCK, COL), lambda b, r, n: (r, b * steps + n, 0))
    return pl.pallas_call(
        _attn_prompt_kernel,
        grid=(batch, dil // nres, steps),
        in_specs=[cur(0), cur(1), prev(1), cur(2), prev(2), _resident(bias.shape)],
        out_specs=[out_spec, out_spec],
        out_shape=[jax.ShapeDtypeStruct((dil, rows, COL), BF16),
                   jax.ShapeDtypeStruct((dil, rows, COL), F32)],
        compiler_params=_params(3),
        name=f"attn_prompt_g{g}",
    )(qkv, qkv, qkv, qkv, qkv, bias)


def _token_order(ref, scr_ref):
    d, n, _ = ref.shape
    if d == 1:
        return ref[0].astype(F32)
    for r in range(d):
        part = ref[r].astype(F32)
        for c in range(COL // LANES):
            scr_ref[c, pl.ds(r, n, stride=d), :] = part[:, c * LANES:(c + 1) * LANES]
    return jnp.concatenate([scr_ref[c] for c in range(COL // LANES)], axis=1)


def _mix_out_kernel(*refs, n_att):
    att = refs[:n_att]
    ro_ref, gr_ref, ga_ref, x_ref, wr_ref, wa_ref, wo_ref, o_ref = refs[n_att:n_att + 8]
    scratch = refs[n_att + 8:]
    if n_att == 2:
        row_ref, col_ref = att
        ao = row_ref[...] + jnp.concatenate([col_ref[t].T for t in range(col_ref.shape[0])], axis=0)
    else:
        vals = [_token_order(r, s) for r, s in zip(att, scratch)]
        outs, lses = vals[:N_GROUPS], vals[N_GROUPS:]
        top = functools.reduce(jnp.maximum, lses)
        es = [jnp.exp(x - top) for x in lses]
        den = functools.reduce(jnp.add, es)
        ao = functools.reduce(jnp.add, [e * o for e, o in zip(es, outs)]) / den
    bm = x_ref.shape[0]
    n_parts = 2 if bm % 32 == 0 else 1
    for part in range(n_parts):
        rows = slice(part * bm // n_parts, (part + 1) * bm // n_parts)

        def token_major(ref):
            return jnp.concatenate([ref[c, rows, :] for c in range(ref.shape[0])], axis=1)

        a_br = _dot(ao[rows].astype(BF16), wa_ref[...])
        r_br = _dot(token_major(ro_ref).astype(BF16), wr_ref[...])
        merged = (jax.nn.sigmoid(token_major(gr_ref).astype(F32)) * r_br
                  + jax.nn.sigmoid(token_major(ga_ref).astype(F32)) * a_br)
        o_ref[rows, :] = x_ref[rows, :] + _dot(merged.astype(BF16), wo_ref[...])


def _mix_out(att, ro, proj, x, wr, wa, wo, bm):
    m = x.shape[0]
    if len(att) == 2:
        assert bm % LANES == 0, bm
        att_specs = [pl.BlockSpec((bm, COL), lambda i: (i, 0)),
                     pl.BlockSpec((bm // LANES, COL, LANES), lambda i: (i, 0, 0))]
        scratch = []
    else:
        att_specs = [pl.BlockSpec((a.shape[0], bm // a.shape[0], COL), lambda i: (0, i, 0))
                     for a in att]
        scratch = [pltpu.VMEM((COL // LANES, bm, LANES), F32) for a in att]
    return pl.pallas_call(
        functools.partial(_mix_out_kernel, n_att=len(att)),
        grid=(m // bm,),
        scratch_shapes=scratch,
        in_specs=att_specs + [
            pl.BlockSpec((4, bm, COL), lambda i: (0, i, 0)),
            pl.BlockSpec((4, bm, COL), lambda i: (GR // 4, i, 0)),
            pl.BlockSpec((4, bm, COL), lambda i: (GA // 4, i, 0)),
            pl.BlockSpec((bm, D_MODEL), lambda i: (i, 0)),
            _resident(wr.shape), _resident(wa.shape), _resident(wo.shape)],
        out_specs=pl.BlockSpec((bm, D_MODEL), lambda i: (i, 0)),
        out_shape=jax.ShapeDtypeStruct((m, D_MODEL), F32),
        compiler_params=_params(1),
        name="mix_out",
    )(*att, ro, proj, proj, x, wr, wa, wo)


def _own_head():
    shape = (2 * HEADS_PER_GROUP, ATT_OUT)
    return (lax.broadcasted_iota(jnp.int32, shape, 0)
            == lax.broadcasted_iota(jnp.int32, shape, 1) // HEAD_DIM)


def _sample_scores(qkv, kv, biases, n):
    scale = HEAD_DIM ** -0.5
    stats = []
    for g in range(N_GROUPS):
        qbd = jnp.where(_own_head(), qkv[g][0, 0, pl.ds(n, 1), :], 0.0)
        s = _dot(qbd.astype(BF16), kv[g][0].astype(BF16)) * scale + biases[g][...]
        sn = jnp.sum(qbd * qkv[g][1, 0, pl.ds(n, 1), :], axis=-1, keepdims=True) * scale
        m = jnp.maximum(jnp.max(s, axis=-1, keepdims=True), sn)
        p = jnp.exp(s - m)
        pn = jnp.exp(sn - m)
        l = jnp.sum(p, axis=-1, keepdims=True) + pn
        stats.append((p, pn, l, m + jnp.log(l)))
    return stats


def _sample_values(qkv, kv, n, stats):
    def per_row(a):
        return jnp.concatenate([jnp.broadcast_to(a[h:h + 1, :], (HEAD_DIM, a.shape[1]))
                                for h in range(HEADS_PER_GROUP)], axis=0)

    lses = [s[3] for s in stats]
    top = functools.reduce(jnp.maximum, lses)
    es = [jnp.exp(x - top) for x in lses]
    den = functools.reduce(jnp.add, es)
    col = jnp.zeros((ATT_OUT, 1), F32)
    row = jnp.zeros((2 * HEADS_PER_GROUP, ATT_OUT), F32)
    for g, (p, pn, l, _) in enumerate(stats):
        w = es[g] / (den * l)
        col = col + jnp.sum(kv[g][1] * per_row(p), axis=-1, keepdims=True) * per_row(w)
        row = row + (pn * w) * qkv[g][2, 0, pl.ds(n, 1), :]
    return col, jnp.sum(jnp.where(_own_head(), row, 0.0), axis=0, keepdims=True)


def _ffn_kernel(*refs, final, hosted):
    x_ref, g_ref, wgu_ref, wd_ref, gf_ref = refs[:5]
    if hosted is None:
        o_ref, hn_ref, act_ref = refs[5:]
        seqs = 0
    else:
        layer, seqs = hosted
        qkv, caches, biases = refs[5:8], refs[8:11], refs[11:14]
        o_ref, ao_row_ref, ao_col_ref, hn_ref, act_ref = refs[14:19]
        bufs, sem = refs[19:22], refs[22]
        step, n_steps = pl.program_id(0), pl.num_programs(0)

        def copies(n):
            return [pltpu.make_async_copy(caches[g].at[layer, n], bufs[g].at[n % 2], sem.at[g, n % 2])
                    for g in range(N_GROUPS)]

        @pl.when(step == 0)
        def _():
            for c in copies(0):
                c.start()
            ao_col_ref[...] = jnp.zeros_like(ao_col_ref)

        def fetch(u):
            n = step * seqs + u
            for c in copies(n):
                c.wait()
            if u + 1 < seqs:
                for c in copies(n + 1):
                    c.start()
            else:
                @pl.when(step + 1 < n_steps)
                def _():
                    for c in copies(n + 1):
                        c.start()

        def window(u):
            n = step * seqs + u
            return n, [bufs[g].at[n % 2] for g in range(N_GROUPS)]

    first = [-(-u * FF_BLOCKS // seqs) for u in range(seqs)]
    stats = {}
    x = x_ref[...]
    hn_ref[...] = _rms(x, g_ref[...]).astype(BF16)
    for c in range(FF_BLOCKS):
        if c in first:
            fetch(first.index(c))
        gate = _dot(hn_ref[...], wgu_ref[:, c * COL:(c + 1) * COL])
        up = _dot(hn_ref[...], wgu_ref[:, D_FF + c * COL:D_FF + (c + 1) * COL])
        act_ref[:, c * COL:(c + 1) * COL] = (_silu(gate) * up).astype(BF16)
        if c in first:
            u = first.index(c)
            n, kv = window(u)
            stats[u] = _sample_scores(qkv, kv, biases, n)
        for u in [u for u in stats if c == min(first[u] + 1, (first + [FF_BLOCKS])[u + 1] - 1)]:
            n, kv = window(u)
            col, row = _sample_values(qkv, kv, n, stats.pop(u))
            ao_row_ref[pl.ds(n, 1), :] = row
            lane = lax.broadcasted_iota(jnp.int32, (ATT_OUT, LANES), 1)
            tile = ao_col_ref[n // LANES]
            ao_col_ref[n // LANES] = jnp.where(lane == n % LANES, col, tile)
    y = x + _dot(act_ref[...], wd_ref[...])
    o_ref[...] = _rms(y, gf_ref[...]) if final else y


def _sample_cache_views(caches):
    views, biases = [], []
    slopes = _alibi_slopes()
    for g, (window, dil) in enumerate(ATT_GROUPS):
        depth, n, wlen = caches[g].shape[:3]
        assert wlen == window == ATT_BLOCK * dil, (caches[g].shape, window, dil)
        views.append(caches[g].transpose(0, 1, 3, 4, 5, 2).reshape(depth, n, 2, ATT_OUT, wlen))
        back = wlen - jnp.arange(wlen)
        pen = -slopes[g][:, None] * back.astype(F32)[None, :]
        pen = jnp.where((back % dil == 0)[None, :], pen, MASKED)
        biases.append(jnp.pad(pen, ((0, HEADS_PER_GROUP), (0, 0))))
    return views, biases


def _ffn(x, g, wgu, wd, gf, bm, final, hosted=None):
    m = x.shape[0]
    steps = m // bm
    in_specs = [pl.BlockSpec((bm, D_MODEL), lambda i: (i, 0)),
                _resident((1, D_MODEL)),
                _resident(wgu.shape), _resident(wd.shape),
                _resident((1, D_MODEL))]
    args = [x, g.reshape(1, D_MODEL), wgu, wd, gf.reshape(1, D_MODEL)]
    out_specs = [pl.BlockSpec((bm, D_MODEL), lambda i: (i, 0))]
    out_shape = [jax.ShapeDtypeStruct((m, D_MODEL), F32)]
    scratch = [pltpu.VMEM((bm, D_MODEL), BF16), pltpu.VMEM((bm, D_FF), BF16)]
    kernel_hosted = None
    if hosted is not None:
        layer, qkv, views, biases = hosted
        n = qkv[0].shape[2]
        assert n % steps == 0 and n // steps <= FF_BLOCKS, (n, steps)
        kernel_hosted = (layer, n // steps)
        in_specs += ([_resident(a.shape) for a in qkv]
                     + [pl.BlockSpec(memory_space=pl.ANY) for _ in views]
                     + [_resident(b.shape) for b in biases])
        args += [*qkv, *views, *biases]
        assert n % LANES == 0, n
        out_specs += [pl.BlockSpec((n, ATT_OUT), lambda i: (0, 0)),
                      pl.BlockSpec((n // LANES, ATT_OUT, LANES), lambda i: (0, 0, 0))]
        out_shape += [jax.ShapeDtypeStruct((n, ATT_OUT), F32),
                      jax.ShapeDtypeStruct((n // LANES, ATT_OUT, LANES), F32)]
        scratch += [pltpu.VMEM((2,) + v.shape[2:], F32) for v in views]
        scratch.append(pltpu.SemaphoreType.DMA((N_GROUPS, 2)))
    out = pl.pallas_call(
        functools.partial(_ffn_kernel, final=final, hosted=kernel_hosted),
        grid=(steps,),
        in_specs=in_specs,
        out_specs=out_specs,
        out_shape=out_shape,
        scratch_shapes=scratch,
        compiler_params=_params(1),
        name="ffn",
    )(*args)
    return out if hosted is not None else out[0]


def _ret_sample_kernel(q_ref, k_ref, v_ref, rg_ref, s0_ref, gam_ref, *rest, nbb, aliased):
    ro_ref, s1_ref = rest[1:] if aliased else rest
    for h in range(RET_HEADS):
        lo = (h % 2) * RET_DK
        qt = q_ref[h // 2, :, lo:lo + RET_DK].T
        kt = k_ref[h // 2, :, lo:lo + RET_DK].T
        for b in range(nbb):
            s1 = s0_ref[b, h] * gam_ref[h] + kt[:, b:b + 1] * v_ref[h, b:b + 1, :]
            s1_ref[b, h] = s1
            ro_ref[h, b:b + 1, :] = jnp.sum(s1 * qt[:, b:b + 1], axis=0, keepdims=True)
    for h in range(RET_HEADS):
        o = ro_ref[h] * (RET_DK ** -0.5)
        ro_ref[h] = _group_norm(o) * _silu(rg_ref[h])


def _ret_sample(proj, state, layer, s_prev, nbb):
    n = proj.shape[1]
    gam = jnp.exp(jnp.log1p(-jnp.exp2(-5.0 - jnp.arange(RET_HEADS, dtype=F32))))
    gam = jnp.broadcast_to(gam[:, None, None], (RET_HEADS, 1, RET_DV))
    st_spec = pl.BlockSpec((None, nbb, RET_HEADS, RET_DK, RET_DV), lambda i: (layer, i, 0, 0, 0))
    aliased = s_prev is not None
    in_specs = [pl.BlockSpec((2, nbb, COL), lambda i: (RQ // 2, i, 0)),
                pl.BlockSpec((2, nbb, COL), lambda i: (RK // 2, i, 0)),
                pl.BlockSpec((4, nbb, COL), lambda i: (RV // 4, i, 0)),
                pl.BlockSpec((4, nbb, COL), lambda i: (RG // 4, i, 0)),
                st_spec, _resident(gam.shape)]
    args = [proj, proj, proj, proj, state, gam]
    if aliased:
        in_specs.append(pl.BlockSpec(memory_space=pl.ANY))
        args.append(s_prev)
    return pl.pallas_call(
        functools.partial(_ret_sample_kernel, nbb=nbb, aliased=aliased),
        grid=(n // nbb,),
        in_specs=in_specs,
        out_specs=[pl.BlockSpec((4, nbb, COL), lambda i: (0, i, 0)), st_spec],
        out_shape=[jax.ShapeDtypeStruct((4, n, COL), F32),
                   jax.ShapeDtypeStruct(state.shape, F32)],
        input_output_aliases={6: 1} if aliased else {},
        compiler_params=_params(1),
        name="ret_sample",
    )(*args)


def _prep_weights(w_in, w_ret_branch, w_att_branch, w_out, w_gate_up, w_down):
    return tuple(w.astype(BF16) for w in (w_in, w_ret_branch, w_att_branch, w_out, w_gate_up, w_down))


def _kv_rows(qkv, batch, keep):
    _, dil, rows, _ = qkv.shape
    per_seq = rows // batch
    n = keep // dil
    if per_seq == n:
        a = qkv[1:].reshape(2, dil, batch, n, COL).transpose(2, 0, 1, 3, 4)
    else:
        a = jnp.stack([qkv[1:, :, (b + 1) * per_seq - n:(b + 1) * per_seq] for b in range(batch)])
    a = a.astype(F32).reshape(batch, 2, dil, n, HEADS_PER_GROUP, HEAD_DIM)
    return a.transpose(0, 3, 2, 1, 4, 5).reshape(batch, keep, 2, HEADS_PER_GROUP, HEAD_DIM)


def _block_rows(m, target):
    bm = min(m, target)
    assert m % bm == 0, (m, bm)
    return bm


def kernel(x_prompt, x_sample, state_ret, cache_kv_w128, cache_kv_w512, cache_kv_w2048, norm_mix,
           w_in, w_ret_branch, w_att_branch, w_out, norm_ffn, w_gate_up, w_down, norm_final):
    depth = w_in.shape[0]
    batch, seq, _ = x_prompt.shape
    n_dec, dec_seq, _ = x_sample.shape
    assert dec_seq == 1 and seq % (ATT_BLOCK * ATT_GROUPS[-1][1]) == 0 and seq % RET_CHUNK == 0
    wi, wr, wa, wo, wgu, wd = _prep_weights(w_in, w_ret_branch, w_att_branch, w_out,
                                            w_gate_up, w_down)
    cache_views, cache_biases = _sample_cache_views((cache_kv_w128, cache_kv_w512, cache_kv_w2048))

    m = batch * seq
    bm = _block_rows(m, 512)
    bs = _block_rows(n_dec, 512)
    nbb_ret = 8
    assert n_dec % nbb_ret == 0
    xp = x_prompt.reshape(m, D_MODEL)
    xs = x_sample.reshape(n_dec, D_MODEL)
    p_ret, p_kv = [], [[] for _ in ATT_GROUPS]
    s_ret, s_kv = None, [[] for _ in ATT_GROUPS]
    for l in range(depth):
        last = l == depth - 1
        proj_s, *qkv_s = _in_proj(xs, norm_mix[l], wi[l], bs, F32, (1,) * N_GROUPS)
        ro_s, s_ret = _ret_sample(proj_s, state_ret, l, s_ret, nbb_ret)
        for g in range(N_GROUPS):
            s_kv[g].append(_kv_rows(qkv_s[g], n_dec, 1))
        proj, *qkv = _in_proj(xp, norm_mix[l], wi[l], bm, BF16, DILATIONS)
        ro, s_fin = _ret_prompt(proj.reshape(N_MAIN_BLOCKS, batch, seq, COL))
        p_ret.append(s_fin)
        outs, lses = [], []
        for g, (window, _) in enumerate(ATT_GROUPS):
            o, lse = _attn_prompt(qkv[g], g, batch)
            outs.append(o), lses.append(lse)
            p_kv[g].append(_kv_rows(qkv[g], batch, min(window, seq)))
        xp = _mix_out(outs + lses, ro.reshape(4, m, COL), proj, xp, wr[l], wa[l], wo[l], bm)
        xp, *ao_s = _ffn(xp, norm_ffn[l], wgu[l], wd[l], norm_final, bm, last,
                         hosted=(l, qkv_s, cache_views, cache_biases))
        xs = _mix_out(ao_s, ro_s, proj_s, xs, wr[l], wa[l], wo[l], bs)
        xs = _ffn(xs, norm_ffn[l], wgu[l], wd[l], norm_final, bs, last)
    y_prompt = xp.reshape(batch, seq, D_MODEL)
    y_sample = xs.reshape(n_dec, 1, D_MODEL)

    return (y_prompt, y_sample, jnp.stack(p_ret),
            jnp.stack(p_kv[0]), jnp.stack(p_kv[1]), jnp.stack(p_kv[2]),
            s_ret, jnp.stack(s_kv[0]), jnp.stack(s_kv[1]), jnp.stack(s_kv[2]))
```

```python
import functools

import jax
import jax.numpy as jnp
from jax import lax
from jax.experimental import pallas as pl
from jax.experimental.pallas import tpu as pltpu

F32 = jnp.float32
BF16 = jnp.bfloat16

D_MODEL = 1024
RET_HEADS = 4
RET_DK = 128
RET_DV = 256
RET_CHUNK = 128
ATT_GROUPS = ((128, 1), (512, 4), (2048, 16))
N_GROUPS = 3
HEADS_PER_GROUP = 4
ATT_HEADS = N_GROUPS * HEADS_PER_GROUP
HEAD_DIM = 64
ATT_BLOCK = 128
ATT_OUT = HEADS_PER_GROUP * HEAD_DIM
D_FF = 2816
D_IN = 7424
RMS_EPS = 1e-6
GN_EPS = 1e-5

LANES = 128
COL = 256
N_COL_BLOCKS = D_IN // COL
FF_BLOCKS = D_FF // COL
RQ, RK, RV, RG, GR, GA = 0, 2, 4, 8, 12, 16
N_MAIN_BLOCKS = 20
ATT_SRC_BLOCK = 12
MAIN_SRC_BLOCKS = tuple(range(12)) + tuple(range(21, 29))
DILATIONS = tuple(d for _, d in ATT_GROUPS)
ATT_UNITS_PER_STEP = 4
RET_CHUNKS_PER_STEP = 2
CACHE_SLOTS = 3
CACHE_COPY_ELEMS = HEAD_DIM * 2048
MASKED = -1e30
VMEM_LIMIT = 56 * 1024 * 1024


def _params(n_axes):
    return pltpu.CompilerParams(dimension_semantics=("arbitrary",) * n_axes,
                                vmem_limit_bytes=VMEM_LIMIT)


def _dot(a, b):
    return jnp.dot(a, b, preferred_element_type=F32)


def _dot_nt(a, b):
    return lax.dot_general(a, b, (((1,), (1,)), ((), ())), preferred_element_type=F32)


def _dot_tn(a, b):
    return lax.dot_general(a, b, (((0,), (0,)), ((), ())), preferred_element_type=F32)


def _rms(x, g):
    return x * lax.rsqrt(jnp.mean(x * x, axis=-1, keepdims=True) + RMS_EPS) * g


def _silu(x):
    return x * jax.nn.sigmoid(x)


def _group_norm(o):
    mu = jnp.mean(o, axis=-1, keepdims=True)
    d = o - mu
    var = jnp.mean(d * d, axis=-1, keepdims=True)
    return d * lax.rsqrt(var + GN_EPS)


def _resident(shape):
    zeros = (0,) * len(shape)
    return pl.BlockSpec(shape, lambda *_: zeros, pipeline_mode=pl.Buffered(1))


def _inproj_kernel(x_ref, g_ref, w_ref, main_ref, a0_ref, a1_ref, a2_ref, xn_ref, hn_ref, hd_ref,
                   *, dils):
    bm = x_ref.shape[0]
    xn = _rms(x_ref[...], g_ref[...])
    hn_ref[...] = xn.astype(BF16)
    if any(d > 1 for d in dils):
        for c in range(D_MODEL // LANES):
            xn_ref[c] = xn[:, c * LANES:(c + 1) * LANES]

    def w_block(j):
        return w_ref[:, j * COL:(j + 1) * COL]

    for j in range(N_MAIN_BLOCKS):
        main_ref[j] = _dot(hn_ref[...], w_block(MAIN_SRC_BLOCKS[j])).astype(main_ref.dtype)
    for g, (a_ref, d) in enumerate(zip((a0_ref, a1_ref, a2_ref), dils)):
        if d == 1:
            lhs_ref = hn_ref
        else:
            lhs_ref = hd_ref
            n = bm // d
            for r in range(d):
                for c in range(D_MODEL // LANES):
                    hd_ref[r * n:(r + 1) * n, c * LANES:(c + 1) * LANES] = (
                        xn_ref[c, pl.ds(r, n, stride=d), :].astype(BF16))
        for c in range(3):
            out = _dot(lhs_ref[...], w_block(ATT_SRC_BLOCK + c * N_GROUPS + g))
            a_ref[c] = out.reshape(d, bm // d, COL).astype(a_ref.dtype)


def _in_proj(x, g, w, bm, out_dtype, dils):
    m = x.shape[0]
    att_specs = [pl.BlockSpec((3, d, bm // d, COL), lambda i: (0, 0, i, 0)) for d in dils]
    att_shapes = [jax.ShapeDtypeStruct((3, d, m // d, COL), out_dtype) for d in dils]
    return pl.pallas_call(
        functools.partial(_inproj_kernel, dils=dils),
        grid=(m // bm,),
        in_specs=[pl.BlockSpec((bm, D_MODEL), lambda i: (i, 0)),
                  _resident((1, D_MODEL)),
                  _resident((D_MODEL, D_IN))],
        out_specs=[pl.BlockSpec((N_MAIN_BLOCKS, bm, COL), lambda i: (0, i, 0))] + att_specs,
        out_shape=[jax.ShapeDtypeStruct((N_MAIN_BLOCKS, m, COL), out_dtype)] + att_shapes,
        scratch_shapes=[pltpu.VMEM((D_MODEL // LANES, bm, LANES), F32),
                        pltpu.VMEM((bm, D_MODEL), BF16), pltpu.VMEM((bm, D_MODEL), BF16)],
        compiler_params=_params(1),
        name="in_proj",
    )(x, g.reshape(1, D_MODEL), w)


def _ret_prompt_kernel(q_ref, k_ref, v_ref, rg_ref, dmat_ref, inner_ref, tail_ref, gc_ref,
                       ro_ref, s_ref, *, batch):
    @pl.when(pl.program_id(0) == 0)
    def _():
        s_ref[...] = jnp.zeros_like(s_ref)

    chunk = dmat_ref.shape[1]
    for c in range(q_ref.shape[2] // chunk):
        rows = slice(c * chunk, (c + 1) * chunk)
        for b in range(batch):
            for h in range(RET_HEADS):
                lo = (h % 2) * RET_DK
                q = q_ref[h // 2, b, rows, lo:lo + RET_DK]
                k = k_ref[h // 2, b, rows, lo:lo + RET_DK]
                v = v_ref[h, b, rows, :]
                s0 = s_ref[b, h]
                scores = _dot_nt(q, k) * dmat_ref[h]
                o = _dot(scores.astype(BF16), v) + _dot(q, s0.astype(BF16)) * inner_ref[h]
                kt = (k.astype(F32) * tail_ref[h]).astype(BF16)
                s_ref[b, h] = s0 * gc_ref[h] + _dot_tn(kt, v)
                y = _group_norm(o) * _silu(rg_ref[h, b, rows, :].astype(F32))
                ro_ref[h, b, rows, :] = y.astype(ro_ref.dtype)


def _ret_tables(chunk):
    log_gamma = jnp.log1p(-jnp.exp2(-5.0 - jnp.arange(RET_HEADS, dtype=F32)))
    scale = RET_DK ** -0.5
    idx = jnp.arange(chunk, dtype=F32)
    diff = idx[:, None] - idx[None, :]
    dmat = jnp.where(diff >= 0, jnp.exp(jnp.maximum(diff, 0.0)[None] * log_gamma[:, None, None]), 0.0) * scale
    inner = jnp.exp((idx + 1.0)[None, :] * log_gamma[:, None]) * scale
    tail = jnp.exp((chunk - 1.0 - idx)[None, :] * log_gamma[:, None])
    gc = jnp.exp(chunk * log_gamma)
    inner = jnp.broadcast_to(inner[:, :, None], (RET_HEADS, chunk, RET_DV))
    tail = jnp.broadcast_to(tail[:, :, None], (RET_HEADS, chunk, RET_DK))
    gc = jnp.broadcast_to(gc[:, None, None], (RET_HEADS, 1, RET_DV))
    return dmat, inner, tail, gc


def _ret_prompt(proj4):
    _, b, t, _ = proj4.shape
    dmat, inner, tail, gc = _ret_tables(RET_CHUNK)
    c = RET_CHUNK * RET_CHUNKS_PER_STEP
    assert t % c == 0
    return pl.pallas_call(
        functools.partial(_ret_prompt_kernel, batch=b),
        grid=(t // c,),
        in_specs=[pl.BlockSpec((2, b, c, COL), lambda i: (RQ // 2, 0, i, 0)),
                  pl.BlockSpec((2, b, c, COL), lambda i: (RK // 2, 0, i, 0)),
                  pl.BlockSpec((4, b, c, COL), lambda i: (RV // 4, 0, i, 0)),
                  pl.BlockSpec((4, b, c, COL), lambda i: (RG // 4, 0, i, 0)),
                  _resident(dmat.shape), _resident(inner.shape), _resident(tail.shape),
                  _resident(gc.shape)],
        out_specs=[pl.BlockSpec((4, b, c, COL), lambda i: (0, 0, i, 0)),
                   pl.BlockSpec((b, RET_HEADS, RET_DK, RET_DV), lambda i: (0, 0, 0, 0))],
        out_shape=[jax.ShapeDtypeStruct((4, b, t, COL), BF16),
                   jax.ShapeDtypeStruct((b, RET_HEADS, RET_DK, RET_DV), F32)],
        compiler_params=_params(1),
        name="ret_prompt",
    )(proj4, proj4, proj4, proj4, dmat, inner, tail, gc)


def _attn_prompt_kernel(q_ref, kc_ref, kp_ref, vc_ref, vp_ref, bias_ref, o_ref, lse_ref):
    nres, rows_per_step, _ = q_ref.shape
    first_step = jnp.where(pl.program_id(2) == 0, 0, 1)
    lane_head = lax.broadcasted_iota(jnp.int32, (ATT_BLOCK, ATT_OUT), 1) // HEAD_DIM
    scale = jnp.asarray(HEAD_DIM ** -0.5, BF16)
    for r in range(nres):
        for j in range(rows_per_step // ATT_BLOCK):
            blk = slice(j * ATT_BLOCK, (j + 1) * ATT_BLOCK)
            q = q_ref[r, blk, :] * scale
            qs = jnp.concatenate([jnp.where(lane_head == h, q, jnp.zeros_like(q))
                                  for h in range(HEADS_PER_GROUP)], axis=0)
            if j == 0:
                k_prev, v_prev, bias = kp_ref[r], vp_ref[r], bias_ref[first_step]
            else:
                before = slice((j - 1) * ATT_BLOCK, j * ATT_BLOCK)
                k_prev, v_prev, bias = kc_ref[r, before, :], vc_ref[r, before, :], bias_ref[1]
            kcat = jnp.concatenate([k_prev, kc_ref[r, blk, :]], axis=0)
            vcat = jnp.concatenate([v_prev, vc_ref[r, blk, :]], axis=0)
            s = _dot_nt(qs, kcat) + bias
            m = jnp.max(s, axis=-1, keepdims=True)
            p = jnp.exp(s - m)
            l = jnp.sum(p, axis=-1, keepdims=True)
            on = _dot(p.astype(BF16), vcat) / l
            lse = m + jnp.log(l)
            o = jnp.zeros((ATT_BLOCK, ATT_OUT), F32)
            ls = jnp.zeros((ATT_BLOCK, ATT_OUT), F32)
            for h in range(HEADS_PER_GROUP):
                rows = slice(h * ATT_BLOCK, (h + 1) * ATT_BLOCK)
                o = jnp.where(lane_head == h, on[rows], o)
                ls = jnp.where(lane_head == h, lse[rows], ls)
            o_ref[r, blk, :] = o.astype(o_ref.dtype)
            lse_ref[r, blk, :] = ls


def _alibi_slopes():
    return jnp.exp2(-8.0 * (jnp.arange(ATT_HEADS, dtype=F32) + 1.0) / ATT_HEADS).reshape(
        N_GROUPS, HEADS_PER_GROUP)


def _attn_prompt_bias(g):
    window, dilation = ATT_GROUPS[g]
    steps = window // dilation
    qi = jnp.arange(ATT_BLOCK)[:, None]
    kj = jnp.arange(2 * ATT_BLOCK)[None, :]
    dist = qi + ATT_BLOCK - kj
    valid = (dist >= 0) & (dist <= steps)
    pen = -_alibi_slopes()[g][:, None, None] * (dist * dilation).astype(F32)[None]
    later = jnp.where(valid[None], pen, MASKED)
    first = jnp.where((valid & (kj >= ATT_BLOCK))[None], pen, MASKED)
    return jnp.stack([first, later]).reshape(2, HEADS_PER_GROUP * ATT_BLOCK, 2 * ATT_BLOCK)


def _attn_prompt(qkv, g, batch):
    _, dil, rows, _ = qkv.shape
    nres = min(dil, ATT_UNITS_PER_STEP)
    nblk = ATT_UNITS_PER_STEP // nres
    per_seq = rows // batch // ATT_BLOCK
    assert dil % nres == 0 and per_seq % nblk == 0
    steps = per_seq // nblk
    bias = _attn_prompt_bias(g)

    def cur(c):
        return pl.BlockSpec((None, nres, nblk * ATT_BLOCK, COL),
                            lambda b, r, n: (c, r, b * steps + n, 0))

    def prev(c):
        return pl.BlockSpec((None, nres, ATT_BLOCK, COL),
                            lambda b, r, n: (c, r, b * per_seq + jnp.maximum(n * nblk - 1, 0), 0))

    out_spec = pl.BlockSpec((nres, nblk * ATT_BLOCK, COL), lambda b, r, n: (r, b * steps + n, 0))
    return pl.pallas_call(
        _attn_prompt_kernel,
        grid=(batch, dil // nres, steps),
        in_specs=[cur(0), cur(1), prev(1), cur(2), prev(2), _resident(bias.shape)],
        out_specs=[out_spec, out_spec],
        out_shape=[jax.ShapeDtypeStruct((dil, rows, COL), BF16),
                   jax.ShapeDtypeStruct((dil, rows, COL), F32)],
        compiler_params=_params(3),
        name=f"attn_prompt_g{g}",
    )(qkv, qkv, qkv, qkv, qkv, bias)


def _token_order(ref, scr_ref):
    d, n, _ = ref.shape
    if d == 1:
        return ref[0].astype(F32)
    for r in range(d):
        part = ref[r].astype(F32)
        for c in range(COL // LANES):
            scr_ref[c, pl.ds(r, n, stride=d), :] = part[:, c * LANES:(c + 1) * LANES]
    return jnp.concatenate([scr_ref[c] for c in range(COL // LANES)], axis=1)


def _mix_out_kernel(*refs, n_att):
    att = refs[:n_att]
    ro_ref, gr_ref, ga_ref, x_ref, wr_ref, wa_ref, wo_ref, o_ref = refs[n_att:n_att + 8]
    scratch = refs[n_att + 8:]
    if n_att == 2:
        row_ref, col_ref = att
        ao = row_ref[...] + jnp.concatenate([col_ref[t].T for t in range(col_ref.shape[0])], axis=0)
    else:
        vals = [_token_order(r, s) for r, s in zip(att, scratch)]
        outs, lses = vals[:N_GROUPS], vals[N_GROUPS:]
        top = functools.reduce(jnp.maximum, lses)
        es = [jnp.exp(x - top) for x in lses]
        den = functools.reduce(jnp.add, es)
        ao = functools.reduce(jnp.add, [e * o for e, o in zip(es, outs)]) / den
    bm = x_ref.shape[0]
    n_parts = 2 if bm % 32 == 0 else 1
    for part in range(n_parts):
        rows = slice(part * bm // n_parts, (part + 1) * bm // n_parts)

        def token_major(ref):
            return jnp.concatenate([ref[c, rows, :] for c in range(ref.shape[0])], axis=1)

        a_br = _dot(ao[rows].astype(BF16), wa_ref[...])
        r_br = _dot(token_major(ro_ref).astype(BF16), wr_ref[...])
        merged = (jax.nn.sigmoid(token_major(gr_ref).astype(F32)) * r_br
                  + jax.nn.sigmoid(token_major(ga_ref).astype(F32)) * a_br)
        o_ref[rows, :] = x_ref[rows, :] + _dot(merged.astype(BF16), wo_ref[...])


def _mix_out(att, ro, proj, x, wr, wa, wo, bm):
    m = x.shape[0]
    if len(att) == 2:
        assert bm % LANES == 0, bm
        att_specs = [pl.BlockSpec((bm, COL), lambda i: (i, 0)),
                     pl.BlockSpec((bm // LANES, COL, LANES), lambda i: (i, 0, 0))]
        scratch = []
    else:
        att_specs = [pl.BlockSpec((a.shape[0], bm // a.shape[0], COL), lambda i: (0, i, 0))
                     for a in att]
        scratch = [pltpu.VMEM((COL // LANES, bm, LANES), F32) for a in att]
    return pl.pallas_call(
        functools.partial(_mix_out_kernel, n_att=len(att)),
        grid=(m // bm,),
        scratch_shapes=scratch,
        in_specs=att_specs + [
            pl.BlockSpec((4, bm, COL), lambda i: (0, i, 0)),
            pl.BlockSpec((4, bm, COL), lambda i: (GR // 4, i, 0)),
            pl.BlockSpec((4, bm, COL), lambda i: (GA // 4, i, 0)),
            pl.BlockSpec((bm, D_MODEL), lambda i: (i, 0)),
            _resident(wr.shape), _resident(wa.shape), _resident(wo.shape)],
        out_specs=pl.BlockSpec((bm, D_MODEL), lambda i: (i, 0)),
        out_shape=jax.ShapeDtypeStruct((m, D_MODEL), F32),
        compiler_params=_params(1),
        name="mix_out",
    )(*att, ro, proj, proj, x, wr, wa, wo)


def _own_head():
    shape = (2 * HEADS_PER_GROUP, ATT_OUT)
    return (lax.broadcasted_iota(jnp.int32, shape, 0)
            == lax.broadcasted_iota(jnp.int32, shape, 1) // HEAD_DIM)


def _sample_scores(qkv, kv, biases, n):
    scale = HEAD_DIM ** -0.5
    stats = []
    for g in range(N_GROUPS):
        qbd = jnp.where(_own_head(), qkv[g][0, 0, pl.ds(n, 1), :], 0.0)
        s = _dot(qbd.astype(BF16), kv[g][0].astype(BF16)) * scale + biases[g][...]
        sn = jnp.sum(qbd * qkv[g][1, 0, pl.ds(n, 1), :], axis=-1, keepdims=True) * scale
        m = jnp.maximum(jnp.max(s, axis=-1, keepdims=True), sn)
        p = jnp.exp(s - m)
        pn = jnp.exp(sn - m)
        l = jnp.sum(p, axis=-1, keepdims=True) + pn
        stats.append((p, pn, l, m + jnp.log(l)))
    return stats


def _sample_values(qkv, kv, n, stats):
    def per_row(a):
        return jnp.concatenate([jnp.broadcast_to(a[h:h + 1, :], (HEAD_DIM, a.shape[1]))
                                for h in range(HEADS_PER_GROUP)], axis=0)

    lses = [s[3] for s in stats]
    top = functools.reduce(jnp.maximum, lses)
    es = [jnp.exp(x - top) for x in lses]
    den = functools.reduce(jnp.add, es)
    col = jnp.zeros((ATT_OUT, 1), F32)
    row = jnp.zeros((2 * HEADS_PER_GROUP, ATT_OUT), F32)
    for g, (p, pn, l, _) in enumerate(stats):
        w = es[g] / (den * l)
        col = col + jnp.sum(kv[g][1] * per_row(p), axis=-1, keepdims=True) * per_row(w)
        row = row + (pn * w) * qkv[g][2, 0, pl.ds(n, 1), :]
    return col, jnp.sum(jnp.where(_own_head(), row, 0.0), axis=0, keepdims=True)


def _ffn_kernel(*refs, final, hosted):
    x_ref, g_ref, wgu_ref, wd_ref, gf_ref = refs[:5]
    if hosted is None:
        o_ref, hn_ref, act_ref = refs[5:]
        seqs = 0
    else:
        layer, seqs = hosted
        qkv, caches, biases = refs[5:8], refs[8:11], refs[11:14]
        o_ref, ao_row_ref, ao_col_ref, hn_ref, act_ref = refs[14:19]
        bufs, sem = refs[19:22], refs[22]
        step, n_seqs = pl.program_id(0), pl.num_programs(0) * seqs
        n_slots = bufs[0].shape[0]

        def copies(n):
            slot = n % n_slots
            out = []
            for g in range(N_GROUPS):
                rows = min(ATT_OUT, CACHE_COPY_ELEMS // caches[g].shape[-1])
                for kv in range(2):
                    for r in range(0, ATT_OUT, rows):
                        out.append(pltpu.make_async_copy(
                            caches[g].at[layer, n, kv, pl.ds(r, rows)],
                            bufs[g].at[slot, kv, pl.ds(r, rows)], sem.at[g, slot]))
            return out

        @pl.when(step == 0)
        def _():
            for ahead in range(n_slots - 1):
                for c in copies(ahead):
                    c.start()
            ao_col_ref[...] = jnp.zeros_like(ao_col_ref)

        def fetch(u):
            n = step * seqs + u
            for c in copies(n):
                c.wait()

            @pl.when(n + n_slots - 1 < n_seqs)
            def _():
                for c in copies(n + n_slots - 1):
                    c.start()

        def window(u):
            n = step * seqs + u
            return n, [bufs[g].at[n % n_slots] for g in range(N_GROUPS)]

    first = [-(-u * FF_BLOCKS // seqs) for u in range(seqs)]
    stats = {}
    x = x_ref[...]
    hn_ref[...] = _rms(x, g_ref[...]).astype(BF16)
    for c in range(FF_BLOCKS):
        if c in first:
            fetch(first.index(c))
        gate = _dot(hn_ref[...], wgu_ref[:, c * COL:(c + 1) * COL])
        up = _dot(hn_ref[...], wgu_ref[:, D_FF + c * COL:D_FF + (c + 1) * COL])
        act_ref[:, c * COL:(c + 1) * COL] = (_silu(gate) * up).astype(BF16)
        if c in first:
            u = first.index(c)
            n, kv = window(u)
            stats[u] = _sample_scores(qkv, kv, biases, n)
        for u in [u for u in stats if c == min(first[u] + 1, (first + [FF_BLOCKS])[u + 1] - 1)]:
            n, kv = window(u)
            col, row = _sample_values(qkv, kv, n, stats.pop(u))
            ao_row_ref[pl.ds(n, 1), :] = row
            lane = lax.broadcasted_iota(jnp.int32, (ATT_OUT, LANES), 1)
            tile = ao_col_ref[n // LANES]
            ao_col_ref[n // LANES] = jnp.where(lane == n % LANES, col, tile)
    y = x + _dot(act_ref[...], wd_ref[...])
    o_ref[...] = _rms(y, gf_ref[...]) if final else y


def _sample_cache_views(caches):
    views, biases = [], []
    slopes = _alibi_slopes()
    for g, (window, dil) in enumerate(ATT_GROUPS):
        depth, n, wlen = caches[g].shape[:3]
        assert wlen == window == ATT_BLOCK * dil, (caches[g].shape, window, dil)
        views.append(caches[g].transpose(0, 1, 3, 4, 5, 2).reshape(depth, n, 2, ATT_OUT, wlen))
        back = wlen - jnp.arange(wlen)
        pen = -slopes[g][:, None] * back.astype(F32)[None, :]
        pen = jnp.where((back % dil == 0)[None, :], pen, MASKED)
        biases.append(jnp.pad(pen, ((0, HEADS_PER_GROUP), (0, 0))))
    return views, biases


def _ffn(x, g, wgu, wd, gf, bm, final, hosted=None):
    m = x.shape[0]
    steps = m // bm
    in_specs = [pl.BlockSpec((bm, D_MODEL), lambda i: (i, 0)),
                _resident((1, D_MODEL)),
                _resident(wgu.shape), _resident(wd.shape),
                _resident((1, D_MODEL))]
    args = [x, g.reshape(1, D_MODEL), wgu, wd, gf.reshape(1, D_MODEL)]
    out_specs = [pl.BlockSpec((bm, D_MODEL), lambda i: (i, 0))]
    out_shape = [jax.ShapeDtypeStruct((m, D_MODEL), F32)]
    scratch = [pltpu.VMEM((bm, D_MODEL), BF16), pltpu.VMEM((bm, D_FF), BF16)]
    kernel_hosted = None
    if hosted is not None:
        layer, qkv, views, biases = hosted
        n = qkv[0].shape[2]
        assert n % steps == 0 and n // steps <= FF_BLOCKS, (n, steps)
        kernel_hosted = (layer, n // steps)
        in_specs += ([_resident(a.shape) for a in qkv]
                     + [pl.BlockSpec(memory_space=pl.ANY) for _ in views]
                     + [_resident(b.shape) for b in biases])
        args += [*qkv, *views, *biases]
        assert n % LANES == 0, n
        out_specs += [pl.BlockSpec((n, ATT_OUT), lambda i: (0, 0)),
                      pl.BlockSpec((n // LANES, ATT_OUT, LANES), lambda i: (0, 0, 0))]
        out_shape += [jax.ShapeDtypeStruct((n, ATT_OUT), F32),
                      jax.ShapeDtypeStruct((n // LANES, ATT_OUT, LANES), F32)]
        scratch += [pltpu.VMEM((CACHE_SLOTS,) + v.shape[2:], F32) for v in views]
        scratch.append(pltpu.SemaphoreType.DMA((N_GROUPS, CACHE_SLOTS)))
    out = pl.pallas_call(
        functools.partial(_ffn_kernel, final=final, hosted=kernel_hosted),
        grid=(steps,),
        in_specs=in_specs,
        out_specs=out_specs,
        out_shape=out_shape,
        scratch_shapes=scratch,
        compiler_params=_params(1),
        name="ffn",
    )(*args)
    return out if hosted is not None else out[0]


def _ret_sample_kernel(q_ref, k_ref, v_ref, rg_ref, s0_ref, gam_ref, *rest, nbb, aliased):
    ro_ref, s1_ref = rest[1:] if aliased else rest
    for h in range(RET_HEADS):
        lo = (h % 2) * RET_DK
        qt = q_ref[h // 2, :, lo:lo + RET_DK].T
        kt = k_ref[h // 2, :, lo:lo + RET_DK].T
        for b in range(nbb):
            s1 = s0_ref[b, h] * gam_ref[h] + kt[:, b:b + 1] * v_ref[h, b:b + 1, :]
            s1_ref[b, h] = s1
            ro_ref[h, b:b + 1, :] = jnp.sum(s1 * qt[:, b:b + 1], axis=0, keepdims=True)
    for h in range(RET_HEADS):
        o = ro_ref[h] * (RET_DK ** -0.5)
        ro_ref[h] = _group_norm(o) * _silu(rg_ref[h])


def _ret_sample(proj, state, layer, s_prev, nbb):
    n = proj.shape[1]
    gam = jnp.exp(jnp.log1p(-jnp.exp2(-5.0 - jnp.arange(RET_HEADS, dtype=F32))))
    gam = jnp.broadcast_to(gam[:, None, None], (RET_HEADS, 1, RET_DV))
    st_spec = pl.BlockSpec((None, nbb, RET_HEADS, RET_DK, RET_DV), lambda i: (layer, i, 0, 0, 0))
    aliased = s_prev is not None
    in_specs = [pl.BlockSpec((2, nbb, COL), lambda i: (RQ // 2, i, 0)),
                pl.BlockSpec((2, nbb, COL), lambda i: (RK // 2, i, 0)),
                pl.BlockSpec((4, nbb, COL), lambda i: (RV // 4, i, 0)),
                pl.BlockSpec((4, nbb, COL), lambda i: (RG // 4, i, 0)),
                st_spec, _resident(gam.shape)]
    args = [proj, proj, proj, proj, state, gam]
    if aliased:
        in_specs.append(pl.BlockSpec(memory_space=pl.ANY))
        args.append(s_prev)
    return pl.pallas_call(
        functools.partial(_ret_sample_kernel, nbb=nbb, aliased=aliased),
        grid=(n // nbb,),
        in_specs=in_specs,
        out_specs=[pl.BlockSpec((4, nbb, COL), lambda i: (0, i, 0)), st_spec],
        out_shape=[jax.ShapeDtypeStruct((4, n, COL), F32),
                   jax.ShapeDtypeStruct(state.shape, F32)],
        input_output_aliases={6: 1} if aliased else {},
        compiler_params=_params(1),
        name="ret_sample",
    )(*args)


def _prep_weights(w_in, w_ret_branch, w_att_branch, w_out, w_gate_up, w_down):
    return tuple(w.astype(BF16) for w in (w_in, w_ret_branch, w_att_branch, w_out, w_gate_up, w_down))


def _kv_rows(qkv, batch, keep):
    _, dil, rows, _ = qkv.shape
    per_seq = rows // batch
    n = keep // dil
    if per_seq == n:
        a = qkv[1:].reshape(2, dil, batch, n, COL).transpose(2, 0, 1, 3, 4)
    else:
        a = jnp.stack([qkv[1:, :, (b + 1) * per_seq - n:(b + 1) * per_seq] for b in range(batch)])
    a = a.astype(F32).reshape(batch, 2, dil, n, HEADS_PER_GROUP, HEAD_DIM)
    return a.transpose(0, 3, 2, 1, 4, 5).reshape(batch, keep, 2, HEADS_PER_GROUP, HEAD_DIM)


def _block_rows(m, target):
    bm = min(m, target)
    assert m % bm == 0, (m, bm)
    return bm


def kernel(x_prompt, x_sample, state_ret, cache_kv_w128, cache_kv_w512, cache_kv_w2048, norm_mix,
           w_in, w_ret_branch, w_att_branch, w_out, norm_ffn, w_gate_up, w_down, norm_final):
    depth = w_in.shape[0]
    batch, seq, _ = x_prompt.shape
    n_dec, dec_seq, _ = x_sample.shape
    assert dec_seq == 1 and seq % (ATT_BLOCK * ATT_GROUPS[-1][1]) == 0 and seq % RET_CHUNK == 0
    wi, wr, wa, wo, wgu, wd = _prep_weights(w_in, w_ret_branch, w_att_branch, w_out,
                                            w_gate_up, w_down)
    cache_views, cache_biases = _sample_cache_views((cache_kv_w128, cache_kv_w512, cache_kv_w2048))

    m = batch * seq
    bm = _block_rows(m, 512)
    bs = _block_rows(n_dec, 512)
    nbb_ret = 8
    assert n_dec % nbb_ret == 0
    xp = x_prompt.reshape(m, D_MODEL)
    xs = x_sample.reshape(n_dec, D_MODEL)
    p_ret, p_kv = [], [[] for _ in ATT_GROUPS]
    s_ret, s_kv = None, [[] for _ in ATT_GROUPS]
    for l in range(depth):
        last = l == depth - 1
        proj_s, *qkv_s = _in_proj(xs, norm_mix[l], wi[l], bs, F32, (1,) * N_GROUPS)
        ro_s, s_ret = _ret_sample(proj_s, state_ret, l, s_ret, nbb_ret)
        for g in range(N_GROUPS):
            s_kv[g].append(_kv_rows(qkv_s[g], n_dec, 1))
        proj, *qkv = _in_proj(xp, norm_mix[l], wi[l], bm, BF16, DILATIONS)
        ro, s_fin = _ret_prompt(proj.reshape(N_MAIN_BLOCKS, batch, seq, COL))
        p_ret.append(s_fin)
        outs, lses = [], []
        for g, (window, _) in enumerate(ATT_GROUPS):
            o, lse = _attn_prompt(qkv[g], g, batch)
            outs.append(o), lses.append(lse)
            p_kv[g].append(_kv_rows(qkv[g], batch, min(window, seq)))
        xp = _mix_out(outs + lses, ro.reshape(4, m, COL), proj, xp, wr[l], wa[l], wo[l], bm)
        xp, *ao_s = _ffn(xp, norm_ffn[l], wgu[l], wd[l], norm_final, bm, last,
                         hosted=(l, qkv_s, cache_views, cache_biases))
        xs = _mix_out(ao_s, ro_s, proj_s, xs, wr[l], wa[l], wo[l], bs)
        xs = _ffn(xs, norm_ffn[l], wgu[l], wd[l], norm_final, bs, last)
    y_prompt = xp.reshape(batch, seq, D_MODEL)
    y_sample = xs.reshape(n_dec, 1, D_MODEL)

    return (y_prompt, y_sample, jnp.stack(p_ret),
            jnp.stack(p_kv[0]), jnp.stack(p_kv[1]), jnp.stack(p_kv[2]),
            s_ret, jnp.stack(s_kv[0]), jnp.stack(s_kv[1]), jnp.stack(s_kv[2]))
```

```python
import functools

import jax
import jax.numpy as jnp
from jax import lax
from jax.experimental import pallas as pl
from jax.experimental.pallas import tpu as pltpu

F32 = jnp.float32
BF16 = jnp.bfloat16

D_MODEL = 1024
RET_HEADS = 4
RET_DK = 128
RET_DV = 256
RET_CHUNK = 128
ATT_GROUPS = ((128, 1), (512, 4), (2048, 16))
N_GROUPS = 3
HEADS_PER_GROUP = 4
ATT_HEADS = N_GROUPS * HEADS_PER_GROUP
HEAD_DIM = 64
ATT_BLOCK = 128
ATT_OUT = HEADS_PER_GROUP * HEAD_DIM
D_FF = 2816
D_IN = 7424
RMS_EPS = 1e-6
GN_EPS = 1e-5

LANES = 128
COL = 256
N_COL_BLOCKS = D_IN // COL
FF_BLOCKS = D_FF // COL
RQ, RK, RV, RG, GR, GA = 0, 2, 4, 8, 12, 16
N_MAIN_BLOCKS = 20
ATT_SRC_BLOCK = 12
MAIN_SRC_BLOCKS = tuple(range(12)) + tuple(range(21, 29))
DILATIONS = tuple(d for _, d in ATT_GROUPS)
ATT_UNITS_PER_STEP = 8
RET_CHUNKS_PER_STEP = 2
CACHE_SLOTS = 3
CACHE_COPY_ELEMS = HEAD_DIM * 2048
MASKED = -1e30
VMEM_LIMIT = 56 * 1024 * 1024


def _params(n_axes):
    return pltpu.CompilerParams(dimension_semantics=("arbitrary",) * n_axes,
                                vmem_limit_bytes=VMEM_LIMIT)


def _dot(a, b):
    return jnp.dot(a, b, preferred_element_type=F32)


def _dot_nt(a, b):
    return lax.dot_general(a, b, (((1,), (1,)), ((), ())), preferred_element_type=F32)


def _dot_tn(a, b):
    return lax.dot_general(a, b, (((0,), (0,)), ((), ())), preferred_element_type=F32)


def _rms(x, g):
    return x * lax.rsqrt(jnp.mean(x * x, axis=-1, keepdims=True) + RMS_EPS) * g


def _silu(x):
    return x * jax.nn.sigmoid(x)


def _group_norm(o):
    mu = jnp.mean(o, axis=-1, keepdims=True)
    d = o - mu
    var = jnp.mean(d * d, axis=-1, keepdims=True)
    return d * lax.rsqrt(var + GN_EPS)


def _resident(shape):
    zeros = (0,) * len(shape)
    return pl.BlockSpec(shape, lambda *_: zeros, pipeline_mode=pl.Buffered(1))


def _inproj_kernel(x_ref, g_ref, w_ref, main_ref, a0_ref, a1_ref, a2_ref, *rest, dils, tails):
    tail_refs, (xn_ref, hn_ref, hd_ref) = rest[:-3], rest[-3:]
    bm = x_ref.shape[0]
    xn = _rms(x_ref[...], g_ref[...])
    hn_ref[...] = xn.astype(BF16)
    if any(d > 1 for d in dils):
        for c in range(D_MODEL // LANES):
            xn_ref[c] = xn[:, c * LANES:(c + 1) * LANES]

    def w_block(j):
        return w_ref[:, j * COL:(j + 1) * COL]

    for j in range(N_MAIN_BLOCKS):
        main_ref[j] = _dot(hn_ref[...], w_block(MAIN_SRC_BLOCKS[j])).astype(main_ref.dtype)
    for g, (a_ref, d) in enumerate(zip((a0_ref, a1_ref, a2_ref), dils)):
        if d == 1:
            lhs_ref = hn_ref
        else:
            lhs_ref = hd_ref
            n = bm // d
            for r in range(d):
                for c in range(D_MODEL // LANES):
                    hd_ref[r * n:(r + 1) * n, c * LANES:(c + 1) * LANES] = (
                        xn_ref[c, pl.ds(r, n, stride=d), :].astype(BF16))
        for c in range(3):
            out = _dot(lhs_ref[...], w_block(ATT_SRC_BLOCK + c * N_GROUPS + g))
            a_ref[c] = out.reshape(d, bm // d, COL).astype(a_ref.dtype)
    if tails is not None:
        per_seq, keeps = tails
        in_seq = pl.program_id(0) % per_seq
        for g, (t_ref, keep) in enumerate(zip(tail_refs, keeps)):
            rows = t_ref.shape[-1]

            @pl.when(in_seq >= per_seq - keep // rows)
            def _():
                for c in (1, 2):
                    out = _dot(hn_ref[bm - rows:, :], w_block(ATT_SRC_BLOCK + c * N_GROUPS + g))
                    t_ref[c - 1] = out.T


def _in_proj(x, g, w, bm, out_dtype, dils, tails=None):
    m = x.shape[0]
    att_specs = [pl.BlockSpec((3, d, bm // d, COL), lambda i: (0, 0, i, 0)) for d in dils]
    att_shapes = [jax.ShapeDtypeStruct((3, d, m // d, COL), out_dtype) for d in dils]
    kernel_tails = None
    if tails is not None:
        batch, keeps = tails
        per_seq = m // batch // bm
        kernel_tails = (per_seq, keeps)
        for keep in keeps:
            rows = min(keep, bm)
            n_tail = keep // rows
            assert keep % rows == 0 and n_tail <= per_seq, (keep, bm, per_seq)
            att_specs.append(pl.BlockSpec(
                (None, 2, ATT_OUT, rows),
                lambda i, n_tail=n_tail: (i // per_seq, 0, 0,
                                          jnp.maximum(i % per_seq - (per_seq - n_tail), 0))))
            att_shapes.append(jax.ShapeDtypeStruct((batch, 2, ATT_OUT, keep), F32))
    return pl.pallas_call(
        functools.partial(_inproj_kernel, dils=dils, tails=kernel_tails),
        grid=(m // bm,),
        in_specs=[pl.BlockSpec((bm, D_MODEL), lambda i: (i, 0)),
                  _resident((1, D_MODEL)),
                  _resident((D_MODEL, D_IN))],
        out_specs=[pl.BlockSpec((N_MAIN_BLOCKS, bm, COL), lambda i: (0, i, 0))] + att_specs,
        out_shape=[jax.ShapeDtypeStruct((N_MAIN_BLOCKS, m, COL), out_dtype)] + att_shapes,
        scratch_shapes=[pltpu.VMEM((D_MODEL // LANES, bm, LANES), F32),
                        pltpu.VMEM((bm, D_MODEL), BF16), pltpu.VMEM((bm, D_MODEL), BF16)],
        compiler_params=_params(1),
        name="in_proj",
    )(x, g.reshape(1, D_MODEL), w)


def _ret_prompt_kernel(q_ref, k_ref, v_ref, rg_ref, dmat_ref, inner_ref, tail_ref, gc_ref,
                       ro_ref, s_ref, *, batch):
    @pl.when(pl.program_id(0) == 0)
    def _():
        s_ref[...] = jnp.zeros_like(s_ref)

    chunk = dmat_ref.shape[1]
    for c in range(q_ref.shape[2] // chunk):
        rows = slice(c * chunk, (c + 1) * chunk)
        for b in range(batch):
            for h in range(RET_HEADS):
                lo = (h % 2) * RET_DK
                q = q_ref[h // 2, b, rows, lo:lo + RET_DK]
                k = k_ref[h // 2, b, rows, lo:lo + RET_DK]
                v = v_ref[h, b, rows, :]
                s0 = s_ref[b, h]
                scores = _dot_nt(q, k) * dmat_ref[h]
                o = _dot(scores.astype(BF16), v) + _dot(q, s0.astype(BF16)) * inner_ref[h]
                kt = (k.astype(F32) * tail_ref[h]).astype(BF16)
                s_ref[b, h] = s0 * gc_ref[h] + _dot_tn(kt, v)
                y = _group_norm(o) * _silu(rg_ref[h, b, rows, :].astype(F32))
                ro_ref[h, b, rows, :] = y.astype(ro_ref.dtype)


def _ret_tables(chunk):
    log_gamma = jnp.log1p(-jnp.exp2(-5.0 - jnp.arange(RET_HEADS, dtype=F32)))
    scale = RET_DK ** -0.5
    idx = jnp.arange(chunk, dtype=F32)
    diff = idx[:, None] - idx[None, :]
    dmat = jnp.where(diff >= 0, jnp.exp(jnp.maximum(diff, 0.0)[None] * log_gamma[:, None, None]), 0.0) * scale
    inner = jnp.exp((idx + 1.0)[None, :] * log_gamma[:, None]) * scale
    tail = jnp.exp((chunk - 1.0 - idx)[None, :] * log_gamma[:, None])
    gc = jnp.exp(chunk * log_gamma)
    inner = jnp.broadcast_to(inner[:, :, None], (RET_HEADS, chunk, RET_DV))
    tail = jnp.broadcast_to(tail[:, :, None], (RET_HEADS, chunk, RET_DK))
    gc = jnp.broadcast_to(gc[:, None, None], (RET_HEADS, 1, RET_DV))
    return dmat, inner, tail, gc


def _ret_prompt(proj4):
    _, b, t, _ = proj4.shape
    dmat, inner, tail, gc = _ret_tables(RET_CHUNK)
    c = RET_CHUNK * RET_CHUNKS_PER_STEP
    assert t % c == 0
    return pl.pallas_call(
        functools.partial(_ret_prompt_kernel, batch=b),
        grid=(t // c,),
        in_specs=[pl.BlockSpec((2, b, c, COL), lambda i: (RQ // 2, 0, i, 0)),
                  pl.BlockSpec((2, b, c, COL), lambda i: (RK // 2, 0, i, 0)),
                  pl.BlockSpec((4, b, c, COL), lambda i: (RV // 4, 0, i, 0)),
                  pl.BlockSpec((4, b, c, COL), lambda i: (RG // 4, 0, i, 0)),
                  _resident(dmat.shape), _resident(inner.shape), _resident(tail.shape),
                  _resident(gc.shape)],
        out_specs=[pl.BlockSpec((4, b, c, COL), lambda i: (0, 0, i, 0)),
                   pl.BlockSpec((b, RET_HEADS, RET_DK, RET_DV), lambda i: (0, 0, 0, 0))],
        out_shape=[jax.ShapeDtypeStruct((4, b, t, COL), BF16),
                   jax.ShapeDtypeStruct((b, RET_HEADS, RET_DK, RET_DV), F32)],
        compiler_params=_params(1),
        name="ret_prompt",
    )(proj4, proj4, proj4, proj4, dmat, inner, tail, gc)


def _attn_prompt_kernel(q_ref, kc_ref, kp_ref, vc_ref, vp_ref, bias_ref, o_ref, lse_ref):
    nres, rows_per_step, _ = q_ref.shape
    first_step = jnp.where(pl.program_id(2) == 0, 0, 1)
    lane_head = lax.broadcasted_iota(jnp.int32, (ATT_BLOCK, ATT_OUT), 1) // HEAD_DIM
    scale = jnp.asarray(HEAD_DIM ** -0.5, BF16)
    for r in range(nres):
        for j in range(rows_per_step // ATT_BLOCK):
            blk = slice(j * ATT_BLOCK, (j + 1) * ATT_BLOCK)
            q = q_ref[r, blk, :] * scale
            qs = jnp.concatenate([jnp.where(lane_head == h, q, jnp.zeros_like(q))
                                  for h in range(HEADS_PER_GROUP)], axis=0)
            if j == 0:
                k_prev, v_prev, bias = kp_ref[r], vp_ref[r], bias_ref[first_step]
            else:
                before = slice((j - 1) * ATT_BLOCK, j * ATT_BLOCK)
                k_prev, v_prev, bias = kc_ref[r, before, :], vc_ref[r, before, :], bias_ref[1]
            kcat = jnp.concatenate([k_prev, kc_ref[r, blk, :]], axis=0)
            vcat = jnp.concatenate([v_prev, vc_ref[r, blk, :]], axis=0)
            s = _dot_nt(qs, kcat) + bias
            m = jnp.max(s, axis=-1, keepdims=True)
            p = jnp.exp(s - m)
            l = jnp.sum(p, axis=-1, keepdims=True)
            on = _dot(p.astype(BF16), vcat) / l
            lse = m + jnp.log(l)
            o = jnp.zeros((ATT_BLOCK, ATT_OUT), F32)
            ls = jnp.zeros((ATT_BLOCK, ATT_OUT), F32)
            for h in range(HEADS_PER_GROUP):
                rows = slice(h * ATT_BLOCK, (h + 1) * ATT_BLOCK)
                o = jnp.where(lane_head == h, on[rows], o)
                ls = jnp.where(lane_head == h, lse[rows], ls)
            o_ref[r, blk, :] = o.astype(o_ref.dtype)
            lse_ref[r, blk, :] = ls


def _alibi_slopes():
    return jnp.exp2(-8.0 * (jnp.arange(ATT_HEADS, dtype=F32) + 1.0) / ATT_HEADS).reshape(
        N_GROUPS, HEADS_PER_GROUP)


def _attn_prompt_bias(g):
    window, dilation = ATT_GROUPS[g]
    steps = window // dilation
    qi = jnp.arange(ATT_BLOCK)[:, None]
    kj = jnp.arange(2 * ATT_BLOCK)[None, :]
    dist = qi + ATT_BLOCK - kj
    valid = (dist >= 0) & (dist <= steps)
    pen = -_alibi_slopes()[g][:, None, None] * (dist * dilation).astype(F32)[None]
    later = jnp.where(valid[None], pen, MASKED)
    first = jnp.where((valid & (kj >= ATT_BLOCK))[None], pen, MASKED)
    return jnp.stack([first, later]).reshape(2, HEADS_PER_GROUP * ATT_BLOCK, 2 * ATT_BLOCK)


def _attn_prompt(qkv, g, batch):
    _, dil, rows, _ = qkv.shape
    nres = min(dil, ATT_UNITS_PER_STEP)
    nblk = ATT_UNITS_PER_STEP // nres
    per_seq = rows // batch // ATT_BLOCK
    assert dil % nres == 0 and per_seq % nblk == 0
    steps = per_seq // nblk
    bias = _attn_prompt_bias(g)

    def cur(c):
        return pl.BlockSpec((None, nres, nblk * ATT_BLOCK, COL),
                            lambda b, r, n: (c, r, b * steps + n, 0))

    def prev(c):
        return pl.BlockSpec((None, nres, ATT_BLOCK, COL),
                            lambda b, r, n: (c, r, b * per_seq + jnp.maximum(n * nblk - 1, 0), 0))

    out_spec = pl.BlockSpec((nres, nblk * ATT_BLOCK, COL), lambda b, r, n: (r, b * steps + n, 0))
    return pl.pallas_call(
        _attn_prompt_kernel,
        grid=(batch, dil // nres, steps),
        in_specs=[cur(0), cur(1), prev(1), cur(2), prev(2), _resident(bias.shape)],
        out_specs=[out_spec, out_spec],
        out_shape=[jax.ShapeDtypeStruct((dil, rows, COL), BF16),
                   jax.ShapeDtypeStruct((dil, rows, COL), F32)],
        compiler_params=_params(3),
        name=f"attn_prompt_g{g}",
    )(qkv, qkv, qkv, qkv, qkv, bias)


def _token_order(ref, scr_ref):
    d, n, _ = ref.shape
    if d == 1:
        return ref[0].astype(F32)
    for r in range(d):
        part = ref[r].astype(F32)
        for c in range(COL // LANES):
            scr_ref[c, pl.ds(r, n, stride=d), :] = part[:, c * LANES:(c + 1) * LANES]
    return jnp.concatenate([scr_ref[c] for c in range(COL // LANES)], axis=1)


def _mix_out_kernel(*refs, n_att):
    att = refs[:n_att]
    ro_ref, gr_ref, ga_ref, x_ref, wr_ref, wa_ref, wo_ref, o_ref = refs[n_att:n_att + 8]
    scratch = refs[n_att + 8:]
    if n_att == 2:
        row_ref, col_ref = att
        ao = row_ref[...] + jnp.concatenate([col_ref[t].T for t in range(col_ref.shape[0])], axis=0)
    else:
        vals = [_token_order(r, s) for r, s in zip(att, scratch)]
        outs, lses = vals[:N_GROUPS], vals[N_GROUPS:]
        top = functools.reduce(jnp.maximum, lses)
        es = [jnp.exp(x - top) for x in lses]
        den = functools.reduce(jnp.add, es)
        ao = functools.reduce(jnp.add, [e * o for e, o in zip(es, outs)]) / den
    bm = x_ref.shape[0]
    n_parts = 2 if bm % 32 == 0 else 1
    for part in range(n_parts):
        rows = slice(part * bm // n_parts, (part + 1) * bm // n_parts)

        def token_major(ref):
            return jnp.concatenate([ref[c, rows, :] for c in range(ref.shape[0])], axis=1)

        a_br = _dot(ao[rows].astype(BF16), wa_ref[...])
        r_br = _dot(token_major(ro_ref).astype(BF16), wr_ref[...])
        merged = (jax.nn.sigmoid(token_major(gr_ref).astype(F32)) * r_br
                  + jax.nn.sigmoid(token_major(ga_ref).astype(F32)) * a_br)
        o_ref[rows, :] = x_ref[rows, :] + _dot(merged.astype(BF16), wo_ref[...])


def _mix_out(att, ro, proj, x, wr, wa, wo, bm):
    m = x.shape[0]
    if len(att) == 2:
        assert bm % LANES == 0, bm
        att_specs = [pl.BlockSpec((bm, COL), lambda i: (i, 0)),
                     pl.BlockSpec((bm // LANES, COL, LANES), lambda i: (i, 0, 0))]
        scratch = []
    else:
        att_specs = [pl.BlockSpec((a.shape[0], bm // a.shape[0], COL), lambda i: (0, i, 0))
                     for a in att]
        scratch = [pltpu.VMEM((COL // LANES, bm, LANES), F32) for a in att]
    return pl.pallas_call(
        functools.partial(_mix_out_kernel, n_att=len(att)),
        grid=(m // bm,),
        scratch_shapes=scratch,
        in_specs=att_specs + [
            pl.BlockSpec((4, bm, COL), lambda i: (0, i, 0)),
            pl.BlockSpec((4, bm, COL), lambda i: (GR // 4, i, 0)),
            pl.BlockSpec((4, bm, COL), lambda i: (GA // 4, i, 0)),
            pl.BlockSpec((bm, D_MODEL), lambda i: (i, 0)),
            _resident(wr.shape), _resident(wa.shape), _resident(wo.shape)],
        out_specs=pl.BlockSpec((bm, D_MODEL), lambda i: (i, 0)),
        out_shape=jax.ShapeDtypeStruct((m, D_MODEL), F32),
        compiler_params=_params(1),
        name="mix_out",
    )(*att, ro, proj, proj, x, wr, wa, wo)


def _own_head():
    shape = (2 * HEADS_PER_GROUP, ATT_OUT)
    return (lax.broadcasted_iota(jnp.int32, shape, 0)
            == lax.broadcasted_iota(jnp.int32, shape, 1) // HEAD_DIM)


def _sample_scores(qkv, kv, biases, n):
    scale = HEAD_DIM ** -0.5
    stats = []
    for g in range(N_GROUPS):
        qbd = jnp.where(_own_head(), qkv[g][0, 0, pl.ds(n, 1), :], 0.0)
        s = _dot(qbd.astype(BF16), kv[g][0].astype(BF16)) * scale + biases[g][...]
        sn = jnp.sum(qbd * qkv[g][1, 0, pl.ds(n, 1), :], axis=-1, keepdims=True) * scale
        m = jnp.maximum(jnp.max(s, axis=-1, keepdims=True), sn)
        p = jnp.exp(s - m)
        pn = jnp.exp(sn - m)
        l = jnp.sum(p, axis=-1, keepdims=True) + pn
        stats.append((p, pn, l, m + jnp.log(l)))
    return stats


def _sample_values(qkv, kv, n, stats):
    def per_row(a):
        return jnp.concatenate([jnp.broadcast_to(a[h:h + 1, :], (HEAD_DIM, a.shape[1]))
                                for h in range(HEADS_PER_GROUP)], axis=0)

    lses = [s[3] for s in stats]
    top = functools.reduce(jnp.maximum, lses)
    es = [jnp.exp(x - top) for x in lses]
    den = functools.reduce(jnp.add, es)
    col = jnp.zeros((ATT_OUT, 1), F32)
    row = jnp.zeros((2 * HEADS_PER_GROUP, ATT_OUT), F32)
    for g, (p, pn, l, _) in enumerate(stats):
        w = es[g] / (den * l)
        col = col + jnp.sum(kv[g][1] * per_row(p), axis=-1, keepdims=True) * per_row(w)
        row = row + (pn * w) * qkv[g][2, 0, pl.ds(n, 1), :]
    return col, jnp.sum(jnp.where(_own_head(), row, 0.0), axis=0, keepdims=True)


def _ffn_kernel(*refs, final, hosted):
    x_ref, g_ref, wgu_ref, wd_ref, gf_ref = refs[:5]
    if hosted is None:
        o_ref, hn_ref, act_ref = refs[5:]
        seqs = 0
    else:
        layer, seqs = hosted
        qkv, caches, biases = refs[5:8], refs[8:11], refs[11:14]
        o_ref, ao_row_ref, ao_col_ref, hn_ref, act_ref = refs[14:19]
        bufs, sem = refs[19:22], refs[22]
        step, n_seqs = pl.program_id(0), pl.num_programs(0) * seqs
        n_slots = bufs[0].shape[0]

        def copies(n):
            slot = n % n_slots
            out = []
            for g in range(N_GROUPS):
                rows = min(ATT_OUT, CACHE_COPY_ELEMS // caches[g].shape[-1])
                for kv in range(2):
                    for r in range(0, ATT_OUT, rows):
                        out.append(pltpu.make_async_copy(
                            caches[g].at[layer, n, kv, pl.ds(r, rows)],
                            bufs[g].at[slot, kv, pl.ds(r, rows)], sem.at[g, slot]))
            return out

        @pl.when(step == 0)
        def _():
            for ahead in range(n_slots - 1):
                for c in copies(ahead):
                    c.start()
            ao_col_ref[...] = jnp.zeros_like(ao_col_ref)

        def fetch(u):
            n = step * seqs + u
            for c in copies(n):
                c.wait()

            @pl.when(n + n_slots - 1 < n_seqs)
            def _():
                for c in copies(n + n_slots - 1):
                    c.start()

        def window(u):
            n = step * seqs + u
            return n, [bufs[g].at[n % n_slots] for g in range(N_GROUPS)]

    first = [-(-u * FF_BLOCKS // seqs) for u in range(seqs)]
    stats = {}
    x = x_ref[...]
    hn_ref[...] = _rms(x, g_ref[...]).astype(BF16)
    for c in range(FF_BLOCKS):
        if c in first:
            fetch(first.index(c))
        gate = _dot(hn_ref[...], wgu_ref[:, c * COL:(c + 1) * COL])
        up = _dot(hn_ref[...], wgu_ref[:, D_FF + c * COL:D_FF + (c + 1) * COL])
        act_ref[:, c * COL:(c + 1) * COL] = (_silu(gate) * up).astype(BF16)
        if c in first:
            u = first.index(c)
            n, kv = window(u)
            stats[u] = _sample_scores(qkv, kv, biases, n)
        for u in [u for u in stats if c == min(first[u] + 1, (first + [FF_BLOCKS])[u + 1] - 1)]:
            n, kv = window(u)
            col, row = _sample_values(qkv, kv, n, stats.pop(u))
            ao_row_ref[pl.ds(n, 1), :] = row
            lane = lax.broadcasted_iota(jnp.int32, (ATT_OUT, LANES), 1)
            tile = ao_col_ref[n // LANES]
            ao_col_ref[n // LANES] = jnp.where(lane == n % LANES, col, tile)
    y = x + _dot(act_ref[...], wd_ref[...])
    o_ref[...] = _rms(y, gf_ref[...]) if final else y


def _sample_cache_views(caches):
    views, biases = [], []
    slopes = _alibi_slopes()
    for g, (window, dil) in enumerate(ATT_GROUPS):
        depth, n, wlen = caches[g].shape[:3]
        assert wlen == window == ATT_BLOCK * dil, (caches[g].shape, window, dil)
        views.append(caches[g].transpose(0, 1, 3, 4, 5, 2).reshape(depth, n, 2, ATT_OUT, wlen))
        back = wlen - jnp.arange(wlen)
        pen = -slopes[g][:, None] * back.astype(F32)[None, :]
        pen = jnp.where((back % dil == 0)[None, :], pen, MASKED)
        biases.append(jnp.pad(pen, ((0, HEADS_PER_GROUP), (0, 0))))
    return views, biases


def _ffn(x, g, wgu, wd, gf, bm, final, hosted=None):
    m = x.shape[0]
    steps = m // bm
    in_specs = [pl.BlockSpec((bm, D_MODEL), lambda i: (i, 0)),
                _resident((1, D_MODEL)),
                _resident(wgu.shape), _resident(wd.shape),
                _resident((1, D_MODEL))]
    args = [x, g.reshape(1, D_MODEL), wgu, wd, gf.reshape(1, D_MODEL)]
    out_specs = [pl.BlockSpec((bm, D_MODEL), lambda i: (i, 0))]
    out_shape = [jax.ShapeDtypeStruct((m, D_MODEL), F32)]
    scratch = [pltpu.VMEM((bm, D_MODEL), BF16), pltpu.VMEM((bm, D_FF), BF16)]
    kernel_hosted = None
    if hosted is not None:
        layer, qkv, views, biases = hosted
        n = qkv[0].shape[2]
        assert n % steps == 0 and n // steps <= FF_BLOCKS, (n, steps)
        kernel_hosted = (layer, n // steps)
        in_specs += ([_resident(a.shape) for a in qkv]
                     + [pl.BlockSpec(memory_space=pl.ANY) for _ in views]
                     + [_resident(b.shape) for b in biases])
        args += [*qkv, *views, *biases]
        assert n % LANES == 0, n
        out_specs += [pl.BlockSpec((n, ATT_OUT), lambda i: (0, 0)),
                      pl.BlockSpec((n // LANES, ATT_OUT, LANES), lambda i: (0, 0, 0))]
        out_shape += [jax.ShapeDtypeStruct((n, ATT_OUT), F32),
                      jax.ShapeDtypeStruct((n // LANES, ATT_OUT, LANES), F32)]
        scratch += [pltpu.VMEM((CACHE_SLOTS,) + v.shape[2:], F32) for v in views]
        scratch.append(pltpu.SemaphoreType.DMA((N_GROUPS, CACHE_SLOTS)))
    out = pl.pallas_call(
        functools.partial(_ffn_kernel, final=final, hosted=kernel_hosted),
        grid=(steps,),
        in_specs=in_specs,
        out_specs=out_specs,
        out_shape=out_shape,
        scratch_shapes=scratch,
        compiler_params=_params(1),
        name="ffn",
    )(*args)
    return out if hosted is not None else out[0]


def _ret_sample_kernel(q_ref, k_ref, v_ref, rg_ref, s0_ref, gam_ref, *rest, nbb, aliased):
    ro_ref, s1_ref = rest[1:] if aliased else rest
    for h in range(RET_HEADS):
        lo = (h % 2) * RET_DK
        qt = q_ref[h // 2, :, lo:lo + RET_DK].T
        kt = k_ref[h // 2, :, lo:lo + RET_DK].T
        for b in range(nbb):
            s1 = s0_ref[b, h] * gam_ref[h] + kt[:, b:b + 1] * v_ref[h, b:b + 1, :]
            s1_ref[b, h] = s1
            ro_ref[h, b:b + 1, :] = jnp.sum(s1 * qt[:, b:b + 1], axis=0, keepdims=True)
    for h in range(RET_HEADS):
        o = ro_ref[h] * (RET_DK ** -0.5)
        ro_ref[h] = _group_norm(o) * _silu(rg_ref[h])


def _ret_sample(proj, state, layer, s_prev, nbb):
    n = proj.shape[1]
    gam = jnp.exp(jnp.log1p(-jnp.exp2(-5.0 - jnp.arange(RET_HEADS, dtype=F32))))
    gam = jnp.broadcast_to(gam[:, None, None], (RET_HEADS, 1, RET_DV))
    st_spec = pl.BlockSpec((None, nbb, RET_HEADS, RET_DK, RET_DV), lambda i: (layer, i, 0, 0, 0))
    aliased = s_prev is not None
    in_specs = [pl.BlockSpec((2, nbb, COL), lambda i: (RQ // 2, i, 0)),
                pl.BlockSpec((2, nbb, COL), lambda i: (RK // 2, i, 0)),
                pl.BlockSpec((4, nbb, COL), lambda i: (RV // 4, i, 0)),
                pl.BlockSpec((4, nbb, COL), lambda i: (RG // 4, i, 0)),
                st_spec, _resident(gam.shape)]
    args = [proj, proj, proj, proj, state, gam]
    if aliased:
        in_specs.append(pl.BlockSpec(memory_space=pl.ANY))
        args.append(s_prev)
    return pl.pallas_call(
        functools.partial(_ret_sample_kernel, nbb=nbb, aliased=aliased),
        grid=(n // nbb,),
        in_specs=in_specs,
        out_specs=[pl.BlockSpec((4, nbb, COL), lambda i: (0, i, 0)), st_spec],
        out_shape=[jax.ShapeDtypeStruct((4, n, COL), F32),
                   jax.ShapeDtypeStruct(state.shape, F32)],
        input_output_aliases={6: 1} if aliased else {},
        compiler_params=_params(1),
        name="ret_sample",
    )(*args)


def _prep_weights(w_in, w_ret_branch, w_att_branch, w_out, w_gate_up, w_down):
    return tuple(w.astype(BF16) for w in (w_in, w_ret_branch, w_att_branch, w_out, w_gate_up, w_down))


def _kv_rows(qkv, batch, keep):
    _, dil, rows, _ = qkv.shape
    per_seq = rows // batch
    n = keep // dil
    if per_seq == n:
        a = qkv[1:].reshape(2, dil, batch, n, COL).transpose(2, 0, 1, 3, 4)
    else:
        a = jnp.stack([qkv[1:, :, (b + 1) * per_seq - n:(b + 1) * per_seq] for b in range(batch)])
    a = a.astype(F32).reshape(batch, 2, dil, n, HEADS_PER_GROUP, HEAD_DIM)
    return a.transpose(0, 3, 2, 1, 4, 5).reshape(batch, keep, 2, HEADS_PER_GROUP, HEAD_DIM)


def _block_rows(m, target):
    bm = min(m, target)
    assert m % bm == 0, (m, bm)
    return bm


def kernel(x_prompt, x_sample, state_ret, cache_kv_w128, cache_kv_w512, cache_kv_w2048, norm_mix,
           w_in, w_ret_branch, w_att_branch, w_out, norm_ffn, w_gate_up, w_down, norm_final):
    depth = w_in.shape[0]
    batch, seq, _ = x_prompt.shape
    n_dec, dec_seq, _ = x_sample.shape
    assert dec_seq == 1 and seq % (ATT_BLOCK * ATT_GROUPS[-1][1]) == 0 and seq % RET_CHUNK == 0
    wi, wr, wa, wo, wgu, wd = _prep_weights(w_in, w_ret_branch, w_att_branch, w_out,
                                            w_gate_up, w_down)
    cache_views, cache_biases = _sample_cache_views((cache_kv_w128, cache_kv_w512, cache_kv_w2048))

    m = batch * seq
    bm = _block_rows(m, 512)
    bs = _block_rows(n_dec, 512)
    nbb_ret = 8
    assert n_dec % nbb_ret == 0
    xp = x_prompt.reshape(m, D_MODEL)
    xs = x_sample.reshape(n_dec, D_MODEL)
    p_ret, p_kv = [], [[] for _ in ATT_GROUPS]
    s_ret, s_kv = None, [[] for _ in ATT_GROUPS]
    for l in range(depth):
        last = l == depth - 1
        proj_s, *qkv_s = _in_proj(xs, norm_mix[l], wi[l], bs, F32, (1,) * N_GROUPS)
        ro_s, s_ret = _ret_sample(proj_s, state_ret, l, s_ret, nbb_ret)
        for g in range(N_GROUPS):
            s_kv[g].append(_kv_rows(qkv_s[g], n_dec, 1))
        keeps = tuple(min(window, seq) for window, _ in ATT_GROUPS)
        proj, *qkv = _in_proj(xp, norm_mix[l], wi[l], bm, BF16, DILATIONS, tails=(batch, keeps))
        qkv, kv_tails = qkv[:N_GROUPS], qkv[N_GROUPS:]
        ro, s_fin = _ret_prompt(proj.reshape(N_MAIN_BLOCKS, batch, seq, COL))
        p_ret.append(s_fin)
        outs, lses = [], []
        for g in range(N_GROUPS):
            o, lse = _attn_prompt(qkv[g], g, batch)
            outs.append(o), lses.append(lse)
            p_kv[g].append(kv_tails[g].reshape(batch, 2, HEADS_PER_GROUP, HEAD_DIM, keeps[g])
                           .transpose(0, 4, 1, 2, 3))
        xp = _mix_out(outs + lses, ro.reshape(4, m, COL), proj, xp, wr[l], wa[l], wo[l], bm)
        xp, *ao_s = _ffn(xp, norm_ffn[l], wgu[l], wd[l], norm_final, bm, last,
                         hosted=(l, qkv_s, cache_views, cache_biases))
        xs = _mix_out(ao_s, ro_s, proj_s, xs, wr[l], wa[l], wo[l], bs)
        xs = _ffn(xs, norm_ffn[l], wgu[l], wd[l], norm_final, bs, last)
    y_prompt = xp.reshape(batch, seq, D_MODEL)
    y_sample = xs.reshape(n_dec, 1, D_MODEL)

    return (y_prompt, y_sample, jnp.stack(p_ret),
            jnp.stack(p_kv[0]), jnp.stack(p_kv[1]), jnp.stack(p_kv[2]),
            s_ret, jnp.stack(s_kv[0]), jnp.stack(s_kv[1]), jnp.stack(s_kv[2]))
```

```python
import functools

import jax
import jax.numpy as jnp
from jax import lax
from jax.experimental import pallas as pl
from jax.experimental.pallas import tpu as pltpu

F32 = jnp.float32
BF16 = jnp.bfloat16

D_MODEL = 1024
RET_HEADS = 4
RET_DK = 128
RET_DV = 256
RET_CHUNK = 128
ATT_GROUPS = ((128, 1), (512, 4), (2048, 16))
N_GROUPS = 3
HEADS_PER_GROUP = 4
ATT_HEADS = N_GROUPS * HEADS_PER_GROUP
HEAD_DIM = 64
ATT_BLOCK = 128
ATT_OUT = HEADS_PER_GROUP * HEAD_DIM
D_FF = 2816
D_IN = 7424
RMS_EPS = 1e-6
GN_EPS = 1e-5

LANES = 128
COL = 256
N_COL_BLOCKS = D_IN // COL
FF_BLOCKS = D_FF // COL
RQ, RK, RV, RG, GR, GA = 0, 2, 4, 8, 12, 16
N_MAIN_BLOCKS = 20
ATT_SRC_BLOCK = 12
MAIN_SRC_BLOCKS = tuple(range(12)) + tuple(range(21, 29))
DILATIONS = tuple(d for _, d in ATT_GROUPS)
ATT_UNITS_PER_STEP = 8
RET_CHUNKS_PER_STEP = 2
CACHE_SLOTS = 3
CACHE_COPY_ELEMS = HEAD_DIM * 2048
MASKED = -1e30
VMEM_LIMIT = 56 * 1024 * 1024


def _params(n_axes):
    return pltpu.CompilerParams(dimension_semantics=("arbitrary",) * n_axes,
                                vmem_limit_bytes=VMEM_LIMIT)


def _dot(a, b):
    return jnp.dot(a, b, preferred_element_type=F32)


def _dot_nt(a, b):
    return lax.dot_general(a, b, (((1,), (1,)), ((), ())), preferred_element_type=F32)


def _dot_tn(a, b):
    return lax.dot_general(a, b, (((0,), (0,)), ((), ())), preferred_element_type=F32)


def _rms(x, g):
    return x * lax.rsqrt(jnp.mean(x * x, axis=-1, keepdims=True) + RMS_EPS) * g


def _silu(x):
    return x * jax.nn.sigmoid(x)


def _group_norm(o):
    mu = jnp.mean(o, axis=-1, keepdims=True)
    d = o - mu
    var = jnp.mean(d * d, axis=-1, keepdims=True)
    return d * lax.rsqrt(var + GN_EPS)


def _resident(shape, layer=None):
    if layer is None:
        zeros = (0,) * len(shape)
        return pl.BlockSpec(shape, lambda *_: zeros, pipeline_mode=pl.Buffered(1))
    index = (layer,) + (0,) * (len(shape) - 1)
    return pl.BlockSpec((None,) + tuple(shape[1:]), lambda *_: index, pipeline_mode=pl.Buffered(1))


def _retention_host(refs, layer, seqs):
    ps_ref, gam_ref, state_hbm, new_hbm, ro_ref, sin_ref, sout_ref, sem_in, sem_out = refs
    step, n_steps = pl.program_id(0), pl.num_programs(0)

    def loads(at_step):
        half = (at_step % 2) * seqs
        return [pltpu.make_async_copy(state_hbm.at[layer, at_step * seqs + u, h],
                                      sin_ref.at[half + u, h], sem_in.at[half + u])
                for u in range(seqs) for h in range(RET_HEADS)]

    def stores(at_step):
        return [pltpu.make_async_copy(sout_ref.at[u, h], new_hbm.at[layer, at_step * seqs + u, h],
                                      sem_out.at[u])
                for u in range(seqs) for h in range(RET_HEADS)]

    def columns(block, h):
        lo = (h % 2) * RET_DK
        rows = [ps_ref[block + h // 2, pl.ds(step * seqs + u, 1), :][:, lo:lo + RET_DK]
                for u in range(seqs)]
        return jnp.concatenate(rows + [jnp.zeros((8 - seqs, RET_DK), F32)], axis=0).T

    def run():
        @pl.when(step == 0)
        def _():
            for c in loads(0):
                c.start()

        for c in loads(step):
            c.wait()

        @pl.when(step + 1 < n_steps)
        def _():
            for c in loads(step + 1):
                c.start()

        @pl.when(step > 0)
        def _():
            for c in stores(step - 1):
                c.wait()

        half = (step % 2) * seqs
        for h in range(RET_HEADS):
            q_cols, k_cols = columns(RQ, h), columns(RK, h)
            for u in range(seqs):
                n = step * seqs + u
                s1 = (sin_ref[half + u, h] * gam_ref[h]
                      + k_cols[:, u:u + 1] * ps_ref[RV + h, pl.ds(n, 1), :])
                sout_ref[u, h] = s1
                ro_ref[h, pl.ds(n, 1), :] = jnp.sum(s1 * q_cols[:, u:u + 1], axis=0, keepdims=True)
        for c in stores(step):
            c.start()

    def finish():
        @pl.when(step == n_steps - 1)
        def _():
            for c in stores(step):
                c.wait()
            for h in range(RET_HEADS):
                o = ro_ref[h] * (RET_DK ** -0.5)
                ro_ref[h] = _group_norm(o) * _silu(ps_ref[RG + h])

    return run, finish


def _inproj_kernel(*refs, dils, tails, hosted):
    x_ref, g_ref, w_ref = refs[:3]
    refs = refs[3:]
    if hosted is not None:
        layer, seqs, aliased = hosted
        ps_ref, gam_ref, state_hbm = refs[:3]
        refs = refs[4:] if aliased else refs[3:]
    main_ref, a0_ref, a1_ref, a2_ref = refs[:4]
    n_tails = 0 if tails is None else len(tails[1])
    tail_refs = refs[4:4 + n_tails]
    refs = refs[4 + n_tails:]
    finish = None
    if hosted is not None:
        ro_ref, new_hbm = refs[:2]
        xn_ref, hn_ref, hd_ref, sin_ref, sout_ref, sem_in, sem_out = refs[2:]
        run, finish = _retention_host((ps_ref, gam_ref, state_hbm, new_hbm, ro_ref, sin_ref,
                                       sout_ref, sem_in, sem_out), layer, seqs)
        run()
    else:
        xn_ref, hn_ref, hd_ref = refs
    bm = x_ref.shape[0]
    xn = _rms(x_ref[...], g_ref[...])
    hn_ref[...] = xn.astype(BF16)
    if any(d > 1 for d in dils):
        for c in range(D_MODEL // LANES):
            xn_ref[c] = xn[:, c * LANES:(c + 1) * LANES]

    def w_block(j):
        return w_ref[:, j * COL:(j + 1) * COL]

    for j in range(N_MAIN_BLOCKS):
        main_ref[j] = _dot(hn_ref[...], w_block(MAIN_SRC_BLOCKS[j])).astype(main_ref.dtype)
    for g, (a_ref, d) in enumerate(zip((a0_ref, a1_ref, a2_ref), dils)):
        if d == 1:
            lhs_ref = hn_ref
        else:
            lhs_ref = hd_ref
            n = bm // d
            for r in range(d):
                for c in range(D_MODEL // LANES):
                    hd_ref[r * n:(r + 1) * n, c * LANES:(c + 1) * LANES] = (
                        xn_ref[c, pl.ds(r, n, stride=d), :].astype(BF16))
        for c in range(3):
            out = _dot(lhs_ref[...], w_block(ATT_SRC_BLOCK + c * N_GROUPS + g))
            a_ref[c] = out.reshape(d, bm // d, COL).astype(a_ref.dtype)
    if tails is not None:
        per_seq, keeps = tails
        in_seq = pl.program_id(0) % per_seq
        for g, (t_ref, keep) in enumerate(zip(tail_refs, keeps)):
            rows = t_ref.shape[-1]

            @pl.when(in_seq >= per_seq - keep // rows)
            def _():
                for c in (1, 2):
                    out = _dot(hn_ref[bm - rows:, :], w_block(ATT_SRC_BLOCK + c * N_GROUPS + g))
                    t_ref[c - 1] = out.T
    if finish is not None:
        finish()


def _in_proj(x, g, w, layer, bm, out_dtype, dils, tails=None, hosted=None):
    m = x.shape[0]
    steps = m // bm
    in_specs = [pl.BlockSpec((bm, D_MODEL), lambda i: (i, 0)),
                _resident((1, D_MODEL)),
                _resident(w.shape, layer)]
    args = [x, g.reshape(1, D_MODEL), w]
    host_specs, host_shapes, host_scratch, aliases, kernel_hosted = [], [], [], {}, None
    if hosted is not None:
        proj_s, state, new_state = hosted
        n = proj_s.shape[1]
        seqs = n // steps
        assert n == seqs * steps and 2 <= n and seqs <= 8, (n, steps)
        gam = jnp.exp(jnp.log1p(-jnp.exp2(-5.0 - jnp.arange(RET_HEADS, dtype=F32))))
        gam = jnp.broadcast_to(gam[:, None, None], (RET_HEADS, 1, RET_DV))
        in_specs += [pl.BlockSpec((RG + 4, n, COL), lambda i: (0, 0, 0), pipeline_mode=pl.Buffered(1)),
                     _resident(gam.shape), pl.BlockSpec(memory_space=pl.ANY)]
        args += [proj_s, gam, state]
        if new_state is not None:
            aliases = {len(args): 4 + (0 if tails is None else len(tails[1])) + 1}
            in_specs.append(pl.BlockSpec(memory_space=pl.ANY))
            args.append(new_state)
        kernel_hosted = (layer, seqs, new_state is not None)
        host_specs = [pl.BlockSpec((RET_HEADS, n, COL), lambda i: (0, 0, 0)),
                      pl.BlockSpec(memory_space=pl.ANY)]
        host_shapes = [jax.ShapeDtypeStruct((RET_HEADS, n, COL), F32),
                       jax.ShapeDtypeStruct(state.shape, F32)]
        host_scratch = [pltpu.VMEM((2 * seqs,) + state.shape[2:], F32),
                        pltpu.VMEM((seqs,) + state.shape[2:], F32),
                        pltpu.SemaphoreType.DMA((2 * seqs,)), pltpu.SemaphoreType.DMA((seqs,))]
    att_specs = [pl.BlockSpec((3, d, bm // d, COL), lambda i: (0, 0, i, 0)) for d in dils]
    att_shapes = [jax.ShapeDtypeStruct((3, d, m // d, COL), out_dtype) for d in dils]
    kernel_tails = None
    if tails is not None:
        batch, keeps = tails
        per_seq = m // batch // bm
        kernel_tails = (per_seq, keeps)
        for keep in keeps:
            rows = min(keep, bm)
            n_tail = keep // rows
            assert keep % rows == 0 and n_tail <= per_seq, (keep, bm, per_seq)
            att_specs.append(pl.BlockSpec(
                (None, 2, ATT_OUT, rows),
                lambda i, n_tail=n_tail: (i // per_seq, 0, 0,
                                          jnp.maximum(i % per_seq - (per_seq - n_tail), 0))))
            att_shapes.append(jax.ShapeDtypeStruct((batch, 2, ATT_OUT, keep), F32))
    return pl.pallas_call(
        functools.partial(_inproj_kernel, dils=dils, tails=kernel_tails, hosted=kernel_hosted),
        grid=(steps,),
        in_specs=in_specs,
        out_specs=([pl.BlockSpec((N_MAIN_BLOCKS, bm, COL), lambda i: (0, i, 0))] + att_specs
                   + host_specs),
        out_shape=([jax.ShapeDtypeStruct((N_MAIN_BLOCKS, m, COL), out_dtype)] + att_shapes
                   + host_shapes),
        scratch_shapes=[pltpu.VMEM((D_MODEL // LANES, bm, LANES), F32),
                        pltpu.VMEM((bm, D_MODEL), BF16), pltpu.VMEM((bm, D_MODEL), BF16)]
        + host_scratch,
        input_output_aliases=aliases,
        compiler_params=_params(1),
        name="in_proj",
    )(*args)


def _ret_prompt_kernel(q_ref, k_ref, v_ref, rg_ref, dmat_ref, inner_ref, tail_ref, gc_ref,
                       ro_ref, s_ref, *, batch):
    @pl.when(pl.program_id(0) == 0)
    def _():
        s_ref[...] = jnp.zeros_like(s_ref)

    chunk = dmat_ref.shape[1]
    for c in range(q_ref.shape[2] // chunk):
        rows = slice(c * chunk, (c + 1) * chunk)
        for b in range(batch):
            for h in range(RET_HEADS):
                lo = (h % 2) * RET_DK
                q = q_ref[h // 2, b, rows, lo:lo + RET_DK]
                k = k_ref[h // 2, b, rows, lo:lo + RET_DK]
                v = v_ref[h, b, rows, :]
                s0 = s_ref[b, h]
                scores = _dot_nt(q, k) * dmat_ref[h]
                o = _dot(scores.astype(BF16), v) + _dot(q, s0.astype(BF16)) * inner_ref[h]
                kt = (k.astype(F32) * tail_ref[h]).astype(BF16)
                s_ref[b, h] = s0 * gc_ref[h] + _dot_tn(kt, v)
                y = _group_norm(o) * _silu(rg_ref[h, b, rows, :].astype(F32))
                ro_ref[h, b, rows, :] = y.astype(ro_ref.dtype)


def _ret_tables(chunk):
    log_gamma = jnp.log1p(-jnp.exp2(-5.0 - jnp.arange(RET_HEADS, dtype=F32)))
    scale = RET_DK ** -0.5
    idx = jnp.arange(chunk, dtype=F32)
    diff = idx[:, None] - idx[None, :]
    dmat = jnp.where(diff >= 0, jnp.exp(jnp.maximum(diff, 0.0)[None] * log_gamma[:, None, None]), 0.0) * scale
    inner = jnp.exp((idx + 1.0)[None, :] * log_gamma[:, None]) * scale
    tail = jnp.exp((chunk - 1.0 - idx)[None, :] * log_gamma[:, None])
    gc = jnp.exp(chunk * log_gamma)
    inner = jnp.broadcast_to(inner[:, :, None], (RET_HEADS, chunk, RET_DV))
    tail = jnp.broadcast_to(tail[:, :, None], (RET_HEADS, chunk, RET_DK))
    gc = jnp.broadcast_to(gc[:, None, None], (RET_HEADS, 1, RET_DV))
    return dmat, inner, tail, gc


def _ret_prompt(proj4):
    _, b, t, _ = proj4.shape
    dmat, inner, tail, gc = _ret_tables(RET_CHUNK)
    c = RET_CHUNK * RET_CHUNKS_PER_STEP
    assert t % c == 0
    return pl.pallas_call(
        functools.partial(_ret_prompt_kernel, batch=b),
        grid=(t // c,),
        in_specs=[pl.BlockSpec((2, b, c, COL), lambda i: (RQ // 2, 0, i, 0)),
                  pl.BlockSpec((2, b, c, COL), lambda i: (RK // 2, 0, i, 0)),
                  pl.BlockSpec((4, b, c, COL), lambda i: (RV // 4, 0, i, 0)),
                  pl.BlockSpec((4, b, c, COL), lambda i: (RG // 4, 0, i, 0)),
                  _resident(dmat.shape), _resident(inner.shape), _resident(tail.shape),
                  _resident(gc.shape)],
        out_specs=[pl.BlockSpec((4, b, c, COL), lambda i: (0, 0, i, 0)),
                   pl.BlockSpec((b, RET_HEADS, RET_DK, RET_DV), lambda i: (0, 0, 0, 0))],
        out_shape=[jax.ShapeDtypeStruct((4, b, t, COL), BF16),
                   jax.ShapeDtypeStruct((b, RET_HEADS, RET_DK, RET_DV), F32)],
        compiler_params=_params(1),
        name="ret_prompt",
    )(proj4, proj4, proj4, proj4, dmat, inner, tail, gc)


def _attn_prompt_kernel(q_ref, kc_ref, kp_ref, vc_ref, vp_ref, bias_ref, o_ref, lse_ref):
    nres, rows_per_step, _ = q_ref.shape
    first_step = jnp.where(pl.program_id(2) == 0, 0, 1)
    lane_head = lax.broadcasted_iota(jnp.int32, (ATT_BLOCK, ATT_OUT), 1) // HEAD_DIM
    scale = jnp.asarray(HEAD_DIM ** -0.5, BF16)
    for r in range(nres):
        for j in range(rows_per_step // ATT_BLOCK):
            blk = slice(j * ATT_BLOCK, (j + 1) * ATT_BLOCK)
            q = q_ref[r, blk, :] * scale
            qs = jnp.concatenate([jnp.where(lane_head == h, q, jnp.zeros_like(q))
                                  for h in range(HEADS_PER_GROUP)], axis=0)
            if j == 0:
                k_prev, v_prev, bias = kp_ref[r], vp_ref[r], bias_ref[first_step]
            else:
                before = slice((j - 1) * ATT_BLOCK, j * ATT_BLOCK)
                k_prev, v_prev, bias = kc_ref[r, before, :], vc_ref[r, before, :], bias_ref[1]
            kcat = jnp.concatenate([k_prev, kc_ref[r, blk, :]], axis=0)
            vcat = jnp.concatenate([v_prev, vc_ref[r, blk, :]], axis=0)
            s = _dot_nt(qs, kcat) + bias
            m = jnp.max(s, axis=-1, keepdims=True)
            p = jnp.exp(s - m)
            l = jnp.sum(p, axis=-1, keepdims=True)
            on = _dot(p.astype(BF16), vcat) / l
            lse = m + jnp.log(l)
            o = jnp.zeros((ATT_BLOCK, ATT_OUT), F32)
            ls = jnp.zeros((ATT_BLOCK, ATT_OUT), F32)
            for h in range(HEADS_PER_GROUP):
                rows = slice(h * ATT_BLOCK, (h + 1) * ATT_BLOCK)
                o = jnp.where(lane_head == h, on[rows], o)
                ls = jnp.where(lane_head == h, lse[rows], ls)
            o_ref[r, blk, :] = o.astype(o_ref.dtype)
            lse_ref[r, blk, :] = ls


def _alibi_slopes():
    return jnp.exp2(-8.0 * (jnp.arange(ATT_HEADS, dtype=F32) + 1.0) / ATT_HEADS).reshape(
        N_GROUPS, HEADS_PER_GROUP)


def _attn_prompt_bias(g):
    window, dilation = ATT_GROUPS[g]
    steps = window // dilation
    qi = jnp.arange(ATT_BLOCK)[:, None]
    kj = jnp.arange(2 * ATT_BLOCK)[None, :]
    dist = qi + ATT_BLOCK - kj
    valid = (dist >= 0) & (dist <= steps)
    pen = -_alibi_slopes()[g][:, None, None] * (dist * dilation).astype(F32)[None]
    later = jnp.where(valid[None], pen, MASKED)
    first = jnp.where((valid & (kj >= ATT_BLOCK))[None], pen, MASKED)
    return jnp.stack([first, later]).reshape(2, HEADS_PER_GROUP * ATT_BLOCK, 2 * ATT_BLOCK)


def _attn_prompt(qkv, g, batch):
    _, dil, rows, _ = qkv.shape
    nres = min(dil, ATT_UNITS_PER_STEP)
    nblk = ATT_UNITS_PER_STEP // nres
    per_seq = rows // batch // ATT_BLOCK
    assert dil % nres == 0 and per_seq % nblk == 0
    steps = per_seq // nblk
    bias = _attn_prompt_bias(g)

    def cur(c):
        return pl.BlockSpec((None, nres, nblk * ATT_BLOCK, COL),
                            lambda b, r, n: (c, r, b * steps + n, 0))

    def prev(c):
        return pl.BlockSpec((None, nres, ATT_BLOCK, COL),
                            lambda b, r, n: (c, r, b * per_seq + jnp.maximum(n * nblk - 1, 0), 0))

    out_spec = pl.BlockSpec((nres, nblk * ATT_BLOCK, COL), lambda b, r, n: (r, b * steps + n, 0))
    return pl.pallas_call(
        _attn_prompt_kernel,
        grid=(batch, dil // nres, steps),
        in_specs=[cur(0), cur(1), prev(1), cur(2), prev(2), _resident(bias.shape)],
        out_specs=[out_spec, out_spec],
        out_shape=[jax.ShapeDtypeStruct((dil, rows, COL), BF16),
                   jax.ShapeDtypeStruct((dil, rows, COL), F32)],
        compiler_params=_params(3),
        name=f"attn_prompt_g{g}",
    )(qkv, qkv, qkv, qkv, qkv, bias)


def _token_order(ref, scr_ref):
    d, n, _ = ref.shape
    if d == 1:
        return ref[0].astype(F32)
    for r in range(d):
        part = ref[r].astype(F32)
        for c in range(COL // LANES):
            scr_ref[c, pl.ds(r, n, stride=d), :] = part[:, c * LANES:(c + 1) * LANES]
    return jnp.concatenate([scr_ref[c] for c in range(COL // LANES)], axis=1)


def _mix_out_kernel(*refs, n_att):
    att = refs[:n_att]
    ro_ref, gr_ref, ga_ref, x_ref, wr_ref, wa_ref, wo_ref, o_ref = refs[n_att:n_att + 8]
    scratch = refs[n_att + 8:]
    if n_att == 2:
        row_ref, col_ref = att
        ao = row_ref[...] + jnp.concatenate([col_ref[t].T for t in range(col_ref.shape[0])], axis=0)
    else:
        vals = [_token_order(r, s) for r, s in zip(att, scratch)]
        outs, lses = vals[:N_GROUPS], vals[N_GROUPS:]
        top = functools.reduce(jnp.maximum, lses)
        es = [jnp.exp(x - top) for x in lses]
        den = functools.reduce(jnp.add, es)
        ao = functools.reduce(jnp.add, [e * o for e, o in zip(es, outs)]) / den
    bm = x_ref.shape[0]
    n_parts = 2 if bm % 32 == 0 else 1
    for part in range(n_parts):
        rows = slice(part * bm // n_parts, (part + 1) * bm // n_parts)

        def token_major(ref):
            return jnp.concatenate([ref[c, rows, :] for c in range(ref.shape[0])], axis=1)

        a_br = _dot(ao[rows].astype(BF16), wa_ref[...])
        r_br = _dot(token_major(ro_ref).astype(BF16), wr_ref[...])
        merged = (jax.nn.sigmoid(token_major(gr_ref).astype(F32)) * r_br
                  + jax.nn.sigmoid(token_major(ga_ref).astype(F32)) * a_br)
        o_ref[rows, :] = x_ref[rows, :] + _dot(merged.astype(BF16), wo_ref[...])


def _mix_out(att, ro, proj, x, wr, wa, wo, layer, bm):
    m = x.shape[0]
    if len(att) == 2:
        assert bm % LANES == 0, bm
        att_specs = [pl.BlockSpec((bm, COL), lambda i: (i, 0)),
                     pl.BlockSpec((bm // LANES, COL, LANES), lambda i: (i, 0, 0))]
        scratch = []
    else:
        att_specs = [pl.BlockSpec((a.shape[0], bm // a.shape[0], COL), lambda i: (0, i, 0))
                     for a in att]
        scratch = [pltpu.VMEM((COL // LANES, bm, LANES), F32) for a in att]
    return pl.pallas_call(
        functools.partial(_mix_out_kernel, n_att=len(att)),
        grid=(m // bm,),
        scratch_shapes=scratch,
        in_specs=att_specs + [
            pl.BlockSpec((4, bm, COL), lambda i: (0, i, 0)),
            pl.BlockSpec((4, bm, COL), lambda i: (GR // 4, i, 0)),
            pl.BlockSpec((4, bm, COL), lambda i: (GA // 4, i, 0)),
            pl.BlockSpec((bm, D_MODEL), lambda i: (i, 0)),
            _resident(wr.shape, layer), _resident(wa.shape, layer), _resident(wo.shape, layer)],
        out_specs=pl.BlockSpec((bm, D_MODEL), lambda i: (i, 0)),
        out_shape=jax.ShapeDtypeStruct((m, D_MODEL), F32),
        compiler_params=_params(1),
        name="mix_out",
    )(*att, ro, proj, proj, x, wr, wa, wo)


def _own_head():
    shape = (2 * HEADS_PER_GROUP, ATT_OUT)
    return (lax.broadcasted_iota(jnp.int32, shape, 0)
            == lax.broadcasted_iota(jnp.int32, shape, 1) // HEAD_DIM)


def _sample_scores(qkv, kv, biases, n):
    scale = HEAD_DIM ** -0.5
    stats = []
    for g in range(N_GROUPS):
        qbd = jnp.where(_own_head(), qkv[g][0, 0, pl.ds(n, 1), :], 0.0)
        s = _dot(qbd.astype(BF16), kv[g][0].astype(BF16)) * scale + biases[g][...]
        sn = jnp.sum(qbd * qkv[g][1, 0, pl.ds(n, 1), :], axis=-1, keepdims=True) * scale
        m = jnp.maximum(jnp.max(s, axis=-1, keepdims=True), sn)
        p = jnp.exp(s - m)
        pn = jnp.exp(sn - m)
        l = jnp.sum(p, axis=-1, keepdims=True) + pn
        stats.append((p, pn, l, m + jnp.log(l)))
    return stats


def _sample_values(qkv, kv, n, stats):
    def per_row(a):
        return jnp.concatenate([jnp.broadcast_to(a[h:h + 1, :], (HEAD_DIM, a.shape[1]))
                                for h in range(HEADS_PER_GROUP)], axis=0)

    lses = [s[3] for s in stats]
    top = functools.reduce(jnp.maximum, lses)
    es = [jnp.exp(x - top) for x in lses]
    den = functools.reduce(jnp.add, es)
    col = jnp.zeros((ATT_OUT, 1), F32)
    row = jnp.zeros((2 * HEADS_PER_GROUP, ATT_OUT), F32)
    for g, (p, pn, l, _) in enumerate(stats):
        w = es[g] / (den * l)
        col = col + jnp.sum(kv[g][1] * per_row(p), axis=-1, keepdims=True) * per_row(w)
        row = row + (pn * w) * qkv[g][2, 0, pl.ds(n, 1), :]
    return col, jnp.sum(jnp.where(_own_head(), row, 0.0), axis=0, keepdims=True)


def _ffn_kernel(*refs, final, hosted):
    x_ref, g_ref, wgu_ref, wd_ref, gf_ref = refs[:5]
    if hosted is None:
        o_ref, hn_ref, act_ref = refs[5:]
        seqs = 0
    else:
        layer, seqs = hosted
        qkv, caches, biases = refs[5:8], refs[8:11], refs[11:14]
        o_ref, ao_row_ref, ao_col_ref, hn_ref, act_ref = refs[14:19]
        bufs, sem = refs[19:22], refs[22]
        step, n_seqs = pl.program_id(0), pl.num_programs(0) * seqs
        n_slots = bufs[0].shape[0]

        def copies(n):
            slot = n % n_slots
            out = []
            for g in range(N_GROUPS):
                rows = min(ATT_OUT, CACHE_COPY_ELEMS // caches[g].shape[-1])
                for kv in range(2):
                    for r in range(0, ATT_OUT, rows):
                        out.append(pltpu.make_async_copy(
                            caches[g].at[layer, n, kv, pl.ds(r, rows)],
                            bufs[g].at[slot, kv, pl.ds(r, rows)], sem.at[g, slot]))
            return out

        @pl.when(step == 0)
        def _():
            for ahead in range(n_slots - 1):
                for c in copies(ahead):
                    c.start()
            ao_col_ref[...] = jnp.zeros_like(ao_col_ref)

        def fetch(u):
            n = step * seqs + u
            for c in copies(n):
                c.wait()

            @pl.when(n + n_slots - 1 < n_seqs)
            def _():
                for c in copies(n + n_slots - 1):
                    c.start()

        def window(u):
            n = step * seqs + u
            return n, [bufs[g].at[n % n_slots] for g in range(N_GROUPS)]

    first = [-(-u * FF_BLOCKS // seqs) for u in range(seqs)]
    stats = {}
    x = x_ref[...]
    hn_ref[...] = _rms(x, g_ref[...]).astype(BF16)
    for c in range(FF_BLOCKS):
        if c in first:
            fetch(first.index(c))
        gate = _dot(hn_ref[...], wgu_ref[:, c * COL:(c + 1) * COL])
        up = _dot(hn_ref[...], wgu_ref[:, D_FF + c * COL:D_FF + (c + 1) * COL])
        act_ref[:, c * COL:(c + 1) * COL] = (_silu(gate) * up).astype(BF16)
        if c in first:
            u = first.index(c)
            n, kv = window(u)
            stats[u] = _sample_scores(qkv, kv, biases, n)
        for u in [u for u in stats if c == min(first[u] + 1, (first + [FF_BLOCKS])[u + 1] - 1)]:
            n, kv = window(u)
            col, row = _sample_values(qkv, kv, n, stats.pop(u))
            ao_row_ref[pl.ds(n, 1), :] = row
            lane = lax.broadcasted_iota(jnp.int32, (ATT_OUT, LANES), 1)
            tile = ao_col_ref[n // LANES]
            ao_col_ref[n // LANES] = jnp.where(lane == n % LANES, col, tile)
    y = x + _dot(act_ref[...], wd_ref[...])
    o_ref[...] = _rms(y, gf_ref[...]) if final else y


def _sample_cache_views(caches):
    views, biases = [], []
    slopes = _alibi_slopes()
    for g, (window, dil) in enumerate(ATT_GROUPS):
        depth, n, wlen = caches[g].shape[:3]
        assert wlen == window == ATT_BLOCK * dil, (caches[g].shape, window, dil)
        views.append(caches[g].transpose(0, 1, 3, 4, 5, 2).reshape(depth, n, 2, ATT_OUT, wlen))
        back = wlen - jnp.arange(wlen)
        pen = -slopes[g][:, None] * back.astype(F32)[None, :]
        pen = jnp.where((back % dil == 0)[None, :], pen, MASKED)
        biases.append(jnp.pad(pen, ((0, HEADS_PER_GROUP), (0, 0))))
    return views, biases


def _ffn(x, g, wgu, wd, layer, gf, bm, final, hosted=None):
    m = x.shape[0]
    steps = m // bm
    in_specs = [pl.BlockSpec((bm, D_MODEL), lambda i: (i, 0)),
                _resident((1, D_MODEL)),
                _resident(wgu.shape, layer), _resident(wd.shape, layer),
                _resident((1, D_MODEL))]
    args = [x, g.reshape(1, D_MODEL), wgu, wd, gf.reshape(1, D_MODEL)]
    out_specs = [pl.BlockSpec((bm, D_MODEL), lambda i: (i, 0))]
    out_shape = [jax.ShapeDtypeStruct((m, D_MODEL), F32)]
    scratch = [pltpu.VMEM((bm, D_MODEL), BF16), pltpu.VMEM((bm, D_FF), BF16)]
    kernel_hosted = None
    if hosted is not None:
        qkv, views, biases = hosted
        n = qkv[0].shape[2]
        assert n % steps == 0 and n // steps <= FF_BLOCKS, (n, steps)
        kernel_hosted = (layer, n // steps)
        in_specs += ([_resident(a.shape) for a in qkv]
                     + [pl.BlockSpec(memory_space=pl.ANY) for _ in views]
                     + [_resident(b.shape) for b in biases])
        args += [*qkv, *views, *biases]
        assert n % LANES == 0, n
        out_specs += [pl.BlockSpec((n, ATT_OUT), lambda i: (0, 0)),
                      pl.BlockSpec((n // LANES, ATT_OUT, LANES), lambda i: (0, 0, 0))]
        out_shape += [jax.ShapeDtypeStruct((n, ATT_OUT), F32),
                      jax.ShapeDtypeStruct((n // LANES, ATT_OUT, LANES), F32)]
        scratch += [pltpu.VMEM((CACHE_SLOTS,) + v.shape[2:], F32) for v in views]
        scratch.append(pltpu.SemaphoreType.DMA((N_GROUPS, CACHE_SLOTS)))
    out = pl.pallas_call(
        functools.partial(_ffn_kernel, final=final, hosted=kernel_hosted),
        grid=(steps,),
        in_specs=in_specs,
        out_specs=out_specs,
        out_shape=out_shape,
        scratch_shapes=scratch,
        compiler_params=_params(1),
        name="ffn",
    )(*args)
    return out if hosted is not None else out[0]


def _prep_weights(w_in, w_ret_branch, w_att_branch, w_out, w_gate_up, w_down):
    return tuple(w.astype(BF16) for w in (w_in, w_ret_branch, w_att_branch, w_out, w_gate_up, w_down))


def _new_kv_rows(qkv):
    n = qkv.shape[2]
    return jnp.stack([qkv[1, 0], qkv[2, 0]], axis=1).reshape(n, 1, 2, HEADS_PER_GROUP, HEAD_DIM)


def _block_rows(m, target):
    bm = min(m, target)
    assert m % bm == 0, (m, bm)
    return bm


def kernel(x_prompt, x_sample, state_ret, cache_kv_w128, cache_kv_w512, cache_kv_w2048, norm_mix,
           w_in, w_ret_branch, w_att_branch, w_out, norm_ffn, w_gate_up, w_down, norm_final):
    depth = w_in.shape[0]
    batch, seq, _ = x_prompt.shape
    n_dec, dec_seq, _ = x_sample.shape
    assert dec_seq == 1 and seq % (ATT_BLOCK * ATT_GROUPS[-1][1]) == 0 and seq % RET_CHUNK == 0
    wi, wr, wa, wo, wgu, wd = _prep_weights(w_in, w_ret_branch, w_att_branch, w_out,
                                            w_gate_up, w_down)
    cache_views, cache_biases = _sample_cache_views((cache_kv_w128, cache_kv_w512, cache_kv_w2048))

    m = batch * seq
    bm = _block_rows(m, 512)
    bs = _block_rows(n_dec, 512)
    xp = x_prompt.reshape(m, D_MODEL)
    xs = x_sample.reshape(n_dec, D_MODEL)
    p_ret, p_kv = [], [[] for _ in ATT_GROUPS]
    s_ret, s_kv = None, [[] for _ in ATT_GROUPS]
    for l in range(depth):
        last = l == depth - 1
        proj_s, *qkv_s = _in_proj(xs, norm_mix[l], wi, l, bs, F32, (1,) * N_GROUPS)
        for g in range(N_GROUPS):
            s_kv[g].append(_new_kv_rows(qkv_s[g]))
        keeps = tuple(min(window, seq) for window, _ in ATT_GROUPS)
        proj, *rest = _in_proj(xp, norm_mix[l], wi, l, bm, BF16, DILATIONS, tails=(batch, keeps),
                               hosted=(proj_s, state_ret, s_ret))
        qkv, kv_tails, (ro_s, s_ret) = rest[:N_GROUPS], rest[N_GROUPS:2 * N_GROUPS], rest[2 * N_GROUPS:]
        ro, s_fin = _ret_prompt(proj.reshape(N_MAIN_BLOCKS, batch, seq, COL))
        p_ret.append(s_fin)
        outs, lses = [], []
        for g in range(N_GROUPS):
            o, lse = _attn_prompt(qkv[g], g, batch)
            outs.append(o), lses.append(lse)
            p_kv[g].append(kv_tails[g].reshape(batch, 2, HEADS_PER_GROUP, HEAD_DIM, keeps[g])
                           .transpose(0, 4, 1, 2, 3))
        xp = _mix_out(outs + lses, ro.reshape(4, m, COL), proj, xp, wr, wa, wo, l, bm)
        xp, *ao_s = _ffn(xp, norm_ffn[l], wgu, wd, l, norm_final, bm, last,
                         hosted=(qkv_s, cache_views, cache_biases))
        xs = _mix_out(ao_s, ro_s, proj_s, xs, wr, wa, wo, l, bs)
        xs = _ffn(xs, norm_ffn[l], wgu, wd, l, norm_final, bs, last)
    y_prompt = xp.reshape(batch, seq, D_MODEL)
    y_sample = xs.reshape(n_dec, 1, D_MODEL)

    return (y_prompt, y_sample, jnp.stack(p_ret),
            jnp.stack(p_kv[0]), jnp.stack(p_kv[1]), jnp.stack(p_kv[2]),
            s_ret, jnp.stack(s_kv[0]), jnp.stack(s_kv[1]), jnp.stack(s_kv[2]))
```

```python
import functools

import jax
import jax.numpy as jnp
from jax import lax
from jax.experimental import pallas as pl
from jax.experimental.pallas import tpu as pltpu

F32 = jnp.float32
BF16 = jnp.bfloat16

D_MODEL = 1024
RET_HEADS = 4
RET_DK = 128
RET_DV = 256
RET_CHUNK = 128
ATT_GROUPS = ((128, 1), (512, 4), (2048, 16))
N_GROUPS = 3
HEADS_PER_GROUP = 4
ATT_HEADS = N_GROUPS * HEADS_PER_GROUP
HEAD_DIM = 64
ATT_BLOCK = 128
ATT_OUT = HEADS_PER_GROUP * HEAD_DIM
D_FF = 2816
D_IN = 7424
RMS_EPS = 1e-6
GN_EPS = 1e-5

LANES = 128
COL = 256
N_COL_BLOCKS = D_IN // COL
FF_BLOCKS = D_FF // COL
RQ, RK, RV, RG, GR, GA = 0, 2, 4, 8, 12, 16
N_MAIN_BLOCKS = 20
ATT_SRC_BLOCK = 12
MAIN_SRC_BLOCKS = tuple(range(12)) + tuple(range(21, 29))
DILATIONS = tuple(d for _, d in ATT_GROUPS)
ATT_UNITS_PER_STEP = 8
RET_CHUNKS_PER_STEP = 4
CACHE_SLOTS = 4
CACHE_COPY_ELEMS = HEAD_DIM * 2048
MASKED = -1e30
VMEM_LIMIT = 56 * 1024 * 1024


def _params(n_axes):
    return pltpu.CompilerParams(dimension_semantics=("arbitrary",) * n_axes,
                                vmem_limit_bytes=VMEM_LIMIT)


def _dot(a, b):
    return jnp.dot(a, b, preferred_element_type=F32)


def _dot_nt(a, b):
    return lax.dot_general(a, b, (((1,), (1,)), ((), ())), preferred_element_type=F32)


def _dot_tn(a, b):
    return lax.dot_general(a, b, (((0,), (0,)), ((), ())), preferred_element_type=F32)


def _rms(x, g):
    return x * lax.rsqrt(jnp.mean(x * x, axis=-1, keepdims=True) + RMS_EPS) * g


def _silu(x):
    return x * jax.nn.sigmoid(x)


def _group_norm(o):
    mu = jnp.mean(o, axis=-1, keepdims=True)
    d = o - mu
    var = jnp.mean(d * d, axis=-1, keepdims=True)
    return d * lax.rsqrt(var + GN_EPS)


def _resident(shape, layer=None):
    if layer is None:
        zeros = (0,) * len(shape)
        return pl.BlockSpec(shape, lambda *_: zeros, pipeline_mode=pl.Buffered(1))
    index = (layer,) + (0,) * (len(shape) - 1)
    return pl.BlockSpec((None,) + tuple(shape[1:]), lambda *_: index, pipeline_mode=pl.Buffered(1))


def _retention_host(refs, layer, seqs):
    ps_ref, gam_ref, state_hbm, new_hbm, ro_ref, sin_ref, sout_ref, sem_in, sem_out = refs
    step, n_steps = pl.program_id(0), pl.num_programs(0)

    def loads(at_step):
        half = (at_step % 2) * seqs
        return [pltpu.make_async_copy(state_hbm.at[layer, at_step * seqs + u, h],
                                      sin_ref.at[half + u, h], sem_in.at[half + u])
                for u in range(seqs) for h in range(RET_HEADS)]

    def stores(at_step):
        return [pltpu.make_async_copy(sout_ref.at[u, h], new_hbm.at[layer, at_step * seqs + u, h],
                                      sem_out.at[u])
                for u in range(seqs) for h in range(RET_HEADS)]

    def columns(block, h):
        lo = (h % 2) * RET_DK
        rows = [ps_ref[block + h // 2, pl.ds(step * seqs + u, 1), :][:, lo:lo + RET_DK]
                for u in range(seqs)]
        return jnp.concatenate(rows + [jnp.zeros((8 - seqs, RET_DK), F32)], axis=0).T

    def run():
        @pl.when(step == 0)
        def _():
            for c in loads(0):
                c.start()

        for c in loads(step):
            c.wait()

        @pl.when(step + 1 < n_steps)
        def _():
            for c in loads(step + 1):
                c.start()

        @pl.when(step > 0)
        def _():
            for c in stores(step - 1):
                c.wait()

        half = (step % 2) * seqs
        for h in range(RET_HEADS):
            q_cols, k_cols = columns(RQ, h), columns(RK, h)
            for u in range(seqs):
                n = step * seqs + u
                s1 = (sin_ref[half + u, h] * gam_ref[h]
                      + k_cols[:, u:u + 1] * ps_ref[RV + h, pl.ds(n, 1), :])
                sout_ref[u, h] = s1
                ro_ref[h, pl.ds(n, 1), :] = jnp.sum(s1 * q_cols[:, u:u + 1], axis=0, keepdims=True)
        for c in stores(step):
            c.start()

    def finish():
        @pl.when(step == n_steps - 1)
        def _():
            for c in stores(step):
                c.wait()
            for h in range(RET_HEADS):
                o = ro_ref[h] * (RET_DK ** -0.5)
                ro_ref[h] = _group_norm(o) * _silu(ps_ref[RG + h])

    return run, finish


def _inproj_kernel(*refs, dils, tails, hosted):
    x_ref, g_ref, w_ref = refs[:3]
    refs = refs[3:]
    if hosted is not None:
        layer, seqs, aliased = hosted
        ps_ref, gam_ref, state_hbm = refs[:3]
        refs = refs[4:] if aliased else refs[3:]
    main_ref, a0_ref, a1_ref, a2_ref = refs[:4]
    n_tails = 0 if tails is None else len(tails[1])
    tail_refs = refs[4:4 + n_tails]
    refs = refs[4 + n_tails:]
    finish = None
    if hosted is not None:
        ro_ref, new_hbm = refs[:2]
        xn_ref, hn_ref, hd_ref, sin_ref, sout_ref, sem_in, sem_out = refs[2:]
        run, finish = _retention_host((ps_ref, gam_ref, state_hbm, new_hbm, ro_ref, sin_ref,
                                       sout_ref, sem_in, sem_out), layer, seqs)
        run()
    else:
        xn_ref, hn_ref, hd_ref = refs
    bm = x_ref.shape[0]
    xn = _rms(x_ref[...], g_ref[...])
    hn_ref[...] = xn.astype(BF16)
    dilated = [d for d in dils if d > 1]
    if dilated:
        for c in range(D_MODEL // LANES):
            xn_ref[c] = xn[:, c * LANES:(c + 1) * LANES]
    for slab, d in enumerate(dilated):
        n = bm // d
        for r in range(d):
            for c in range(D_MODEL // LANES):
                hd_ref[slab, r * n:(r + 1) * n, c * LANES:(c + 1) * LANES] = (
                    xn_ref[c, pl.ds(r, n, stride=d), :].astype(BF16))

    def w_block(j):
        return w_ref[:, j * COL:(j + 1) * COL]

    for g, (a_ref, d) in enumerate(zip((a0_ref, a1_ref, a2_ref), dils)):
        lhs = hn_ref[...] if d == 1 else hd_ref[dilated.index(d)]
        for c in range(3):
            out = _dot(lhs, w_block(ATT_SRC_BLOCK + c * N_GROUPS + g))
            a_ref[c] = out.reshape(d, bm // d, COL).astype(a_ref.dtype)
    for j in range(N_MAIN_BLOCKS):
        main_ref[j] = _dot(hn_ref[...], w_block(MAIN_SRC_BLOCKS[j])).astype(main_ref.dtype)
    if tails is not None:
        per_seq, keeps = tails
        in_seq = pl.program_id(0) % per_seq
        for g, (t_ref, keep) in enumerate(zip(tail_refs, keeps)):
            rows = t_ref.shape[-1]

            @pl.when(in_seq >= per_seq - keep // rows)
            def _():
                for c in (1, 2):
                    out = _dot(hn_ref[bm - rows:, :], w_block(ATT_SRC_BLOCK + c * N_GROUPS + g))
                    t_ref[c - 1] = out.T
    if finish is not None:
        finish()


def _in_proj(x, g, w, layer, bm, out_dtype, dils, tails=None, hosted=None):
    m = x.shape[0]
    steps = m // bm
    in_specs = [pl.BlockSpec((bm, D_MODEL), lambda i: (i, 0)),
                _resident((1, D_MODEL)),
                _resident(w.shape, layer)]
    args = [x, g.reshape(1, D_MODEL), w]
    host_specs, host_shapes, host_scratch, aliases, kernel_hosted = [], [], [], {}, None
    if hosted is not None:
        proj_s, state, new_state = hosted
        n = proj_s.shape[1]
        seqs = n // steps
        assert n == seqs * steps and 2 <= n and seqs <= 8, (n, steps)
        gam = jnp.exp(jnp.log1p(-jnp.exp2(-5.0 - jnp.arange(RET_HEADS, dtype=F32))))
        gam = jnp.broadcast_to(gam[:, None, None], (RET_HEADS, 1, RET_DV))
        in_specs += [pl.BlockSpec((RG + 4, n, COL), lambda i: (0, 0, 0), pipeline_mode=pl.Buffered(1)),
                     _resident(gam.shape), pl.BlockSpec(memory_space=pl.ANY)]
        args += [proj_s, gam, state]
        if new_state is not None:
            aliases = {len(args): 4 + (0 if tails is None else len(tails[1])) + 1}
            in_specs.append(pl.BlockSpec(memory_space=pl.ANY))
            args.append(new_state)
        kernel_hosted = (layer, seqs, new_state is not None)
        host_specs = [pl.BlockSpec((RET_HEADS, n, COL), lambda i: (0, 0, 0)),
                      pl.BlockSpec(memory_space=pl.ANY)]
        host_shapes = [jax.ShapeDtypeStruct((RET_HEADS, n, COL), F32),
                       jax.ShapeDtypeStruct(state.shape, F32)]
        host_scratch = [pltpu.VMEM((2 * seqs,) + state.shape[2:], F32),
                        pltpu.VMEM((seqs,) + state.shape[2:], F32),
                        pltpu.SemaphoreType.DMA((2 * seqs,)), pltpu.SemaphoreType.DMA((seqs,))]
    att_specs = [pl.BlockSpec((3, d, bm // d, COL), lambda i: (0, 0, i, 0)) for d in dils]
    att_shapes = [jax.ShapeDtypeStruct((3, d, m // d, COL), out_dtype) for d in dils]
    kernel_tails = None
    if tails is not None:
        batch, keeps = tails
        per_seq = m // batch // bm
        kernel_tails = (per_seq, keeps)
        for keep in keeps:
            rows = min(keep, bm)
            n_tail = keep // rows
            assert keep % rows == 0 and n_tail <= per_seq, (keep, bm, per_seq)
            att_specs.append(pl.BlockSpec(
                (None, 2, ATT_OUT, rows),
                lambda i, n_tail=n_tail: (i // per_seq, 0, 0,
                                          jnp.maximum(i % per_seq - (per_seq - n_tail), 0))))
            att_shapes.append(jax.ShapeDtypeStruct((batch, 2, ATT_OUT, keep), F32))
    return pl.pallas_call(
        functools.partial(_inproj_kernel, dils=dils, tails=kernel_tails, hosted=kernel_hosted),
        grid=(steps,),
        in_specs=in_specs,
        out_specs=([pl.BlockSpec((N_MAIN_BLOCKS, bm, COL), lambda i: (0, i, 0))] + att_specs
                   + host_specs),
        out_shape=([jax.ShapeDtypeStruct((N_MAIN_BLOCKS, m, COL), out_dtype)] + att_shapes
                   + host_shapes),
        scratch_shapes=[pltpu.VMEM((D_MODEL // LANES, bm, LANES), F32),
                        pltpu.VMEM((bm, D_MODEL), BF16),
                        pltpu.VMEM((max(1, sum(d > 1 for d in dils)), bm, D_MODEL), BF16)]
        + host_scratch,
        input_output_aliases=aliases,
        compiler_params=_params(1),
        name="in_proj",
    )(*args)


def _ret_prompt_kernel(q_ref, k_ref, v_ref, rg_ref, dmat_ref, inner_ref, tail_ref, gc_ref,
                       ro_ref, s_ref, *, batch):
    @pl.when(pl.program_id(0) == 0)
    def _():
        s_ref[...] = jnp.zeros_like(s_ref)

    chunk = dmat_ref.shape[1]
    for c in range(q_ref.shape[2] // chunk):
        rows = slice(c * chunk, (c + 1) * chunk)
        for b in range(batch):
            for h in range(RET_HEADS):
                lo = (h % 2) * RET_DK
                q = q_ref[h // 2, b, rows, lo:lo + RET_DK]
                k = k_ref[h // 2, b, rows, lo:lo + RET_DK]
                v = v_ref[h, b, rows, :]
                s0 = s_ref[b, h]
                scores = _dot_nt(q, k) * dmat_ref[h]
                o = _dot(scores.astype(BF16), v) + _dot(q, s0.astype(BF16)) * inner_ref[h]
                kt = (k.astype(F32) * tail_ref[h]).astype(BF16)
                s_ref[b, h] = s0 * gc_ref[h] + _dot_tn(kt, v)
                y = _group_norm(o) * _silu(rg_ref[h, b, rows, :].astype(F32))
                ro_ref[h, b, rows, :] = y.astype(ro_ref.dtype)


def _ret_tables(chunk):
    log_gamma = jnp.log1p(-jnp.exp2(-5.0 - jnp.arange(RET_HEADS, dtype=F32)))
    scale = RET_DK ** -0.5
    idx = jnp.arange(chunk, dtype=F32)
    diff = idx[:, None] - idx[None, :]
    dmat = jnp.where(diff >= 0, jnp.exp(jnp.maximum(diff, 0.0)[None] * log_gamma[:, None, None]), 0.0) * scale
    inner = jnp.exp((idx + 1.0)[None, :] * log_gamma[:, None]) * scale
    tail = jnp.exp((chunk - 1.0 - idx)[None, :] * log_gamma[:, None])
    gc = jnp.exp(chunk * log_gamma)
    inner = jnp.broadcast_to(inner[:, :, None], (RET_HEADS, chunk, RET_DV))
    tail = jnp.broadcast_to(tail[:, :, None], (RET_HEADS, chunk, RET_DK))
    gc = jnp.broadcast_to(gc[:, None, None], (RET_HEADS, 1, RET_DV))
    return dmat, inner, tail, gc


def _ret_prompt(proj4):
    _, b, t, _ = proj4.shape
    dmat, inner, tail, gc = _ret_tables(RET_CHUNK)
    c = RET_CHUNK * RET_CHUNKS_PER_STEP
    assert t % c == 0
    return pl.pallas_call(
        functools.partial(_ret_prompt_kernel, batch=b),
        grid=(t // c,),
        in_specs=[pl.BlockSpec((2, b, c, COL), lambda i: (RQ // 2, 0, i, 0)),
                  pl.BlockSpec((2, b, c, COL), lambda i: (RK // 2, 0, i, 0)),
                  pl.BlockSpec((4, b, c, COL), lambda i: (RV // 4, 0, i, 0)),
                  pl.BlockSpec((4, b, c, COL), lambda i: (RG // 4, 0, i, 0)),
                  _resident(dmat.shape), _resident(inner.shape), _resident(tail.shape),
                  _resident(gc.shape)],
        out_specs=[pl.BlockSpec((4, b, c, COL), lambda i: (0, 0, i, 0)),
                   pl.BlockSpec((b, RET_HEADS, RET_DK, RET_DV), lambda i: (0, 0, 0, 0))],
        out_shape=[jax.ShapeDtypeStruct((4, b, t, COL), BF16),
                   jax.ShapeDtypeStruct((b, RET_HEADS, RET_DK, RET_DV), F32)],
        compiler_params=_params(1),
        name="ret_prompt",
    )(proj4, proj4, proj4, proj4, dmat, inner, tail, gc)


def _attn_prompt_kernel(q_ref, kc_ref, kp_ref, vc_ref, vp_ref, bias_ref, o_ref, lse_ref):
    nres, rows_per_step, _ = q_ref.shape
    first_step = jnp.where(pl.program_id(2) == 0, 0, 1)
    lane_head = lax.broadcasted_iota(jnp.int32, (ATT_BLOCK, ATT_OUT), 1) // HEAD_DIM
    scale = jnp.asarray(HEAD_DIM ** -0.5, BF16)
    for r in range(nres):
        for j in range(rows_per_step // ATT_BLOCK):
            blk = slice(j * ATT_BLOCK, (j + 1) * ATT_BLOCK)
            q = q_ref[r, blk, :] * scale
            qs = jnp.concatenate([jnp.where(lane_head == h, q, jnp.zeros_like(q))
                                  for h in range(HEADS_PER_GROUP)], axis=0)
            if j == 0:
                k_prev, v_prev, bias = kp_ref[r], vp_ref[r], bias_ref[first_step]
            else:
                before = slice((j - 1) * ATT_BLOCK, j * ATT_BLOCK)
                k_prev, v_prev, bias = kc_ref[r, before, :], vc_ref[r, before, :], bias_ref[1]
            kcat = jnp.concatenate([k_prev, kc_ref[r, blk, :]], axis=0)
            vcat = jnp.concatenate([v_prev, vc_ref[r, blk, :]], axis=0)
            s = _dot_nt(qs, kcat) + bias
            m = jnp.max(s, axis=-1, keepdims=True)
            p = jnp.exp(s - m)
            l = jnp.sum(p, axis=-1, keepdims=True)
            on = _dot(p.astype(BF16), vcat) / l
            lse = m + jnp.log(l)
            o = jnp.zeros((ATT_BLOCK, ATT_OUT), F32)
            ls = jnp.zeros((ATT_BLOCK, ATT_OUT), F32)
            for h in range(HEADS_PER_GROUP):
                rows = slice(h * ATT_BLOCK, (h + 1) * ATT_BLOCK)
                o = jnp.where(lane_head == h, on[rows], o)
                ls = jnp.where(lane_head == h, lse[rows], ls)
            o_ref[r, blk, :] = o.astype(o_ref.dtype)
            lse_ref[r, blk, :] = ls


def _alibi_slopes():
    return jnp.exp2(-8.0 * (jnp.arange(ATT_HEADS, dtype=F32) + 1.0) / ATT_HEADS).reshape(
        N_GROUPS, HEADS_PER_GROUP)


def _attn_prompt_bias(g):
    window, dilation = ATT_GROUPS[g]
    steps = window // dilation
    qi = jnp.arange(ATT_BLOCK)[:, None]
    kj = jnp.arange(2 * ATT_BLOCK)[None, :]
    dist = qi + ATT_BLOCK - kj
    valid = (dist >= 0) & (dist <= steps)
    pen = -_alibi_slopes()[g][:, None, None] * (dist * dilation).astype(F32)[None]
    later = jnp.where(valid[None], pen, MASKED)
    first = jnp.where((valid & (kj >= ATT_BLOCK))[None], pen, MASKED)
    return jnp.stack([first, later]).reshape(2, HEADS_PER_GROUP * ATT_BLOCK, 2 * ATT_BLOCK)


def _attn_prompt(qkv, g, batch):
    _, dil, rows, _ = qkv.shape
    nres = min(dil, ATT_UNITS_PER_STEP)
    nblk = ATT_UNITS_PER_STEP // nres
    per_seq = rows // batch // ATT_BLOCK
    assert dil % nres == 0 and per_seq % nblk == 0
    steps = per_seq // nblk
    bias = _attn_prompt_bias(g)

    def cur(c):
        return pl.BlockSpec((None, nres, nblk * ATT_BLOCK, COL),
                            lambda b, r, n: (c, r, b * steps + n, 0))

    def prev(c):
        return pl.BlockSpec((None, nres, ATT_BLOCK, COL),
                            lambda b, r, n: (c, r, b * per_seq + jnp.maximum(n * nblk - 1, 0), 0))

    out_spec = pl.BlockSpec((nres, nblk * ATT_BLOCK, COL), lambda b, r, n: (r, b * steps + n, 0))
    return pl.pallas_call(
        _attn_prompt_kernel,
        grid=(batch, dil // nres, steps),
        in_specs=[cur(0), cur(1), prev(1), cur(2), prev(2), _resident(bias.shape)],
        out_specs=[out_spec, out_spec],
        out_shape=[jax.ShapeDtypeStruct((dil, rows, COL), BF16),
                   jax.ShapeDtypeStruct((dil, rows, COL), F32)],
        compiler_params=_params(3),
        name=f"attn_prompt_g{g}",
    )(qkv, qkv, qkv, qkv, qkv, bias)


def _token_order(ref, scr_ref):
    d, n, _ = ref.shape
    if d == 1:
        return ref[0].astype(F32)
    for r in range(d):
        part = ref[r].astype(F32)
        for c in range(COL // LANES):
            scr_ref[c, pl.ds(r, n, stride=d), :] = part[:, c * LANES:(c + 1) * LANES]
    return jnp.concatenate([scr_ref[c] for c in range(COL // LANES)], axis=1)


def _mix_out_kernel(*refs, n_att):
    att = refs[:n_att]
    ro_ref, gr_ref, ga_ref, x_ref, wr_ref, wa_ref, wo_ref, o_ref = refs[n_att:n_att + 8]
    scratch = refs[n_att + 8:]
    if n_att == 2:
        row_ref, col_ref = att
        ao = row_ref[...] + jnp.concatenate([col_ref[t].T for t in range(col_ref.shape[0])], axis=0)
    else:
        vals = [_token_order(r, s) for r, s in zip(att, scratch)]
        outs, lses = vals[:N_GROUPS], vals[N_GROUPS:]
        top = functools.reduce(jnp.maximum, lses)
        es = [jnp.exp(x - top) for x in lses]
        den = functools.reduce(jnp.add, es)
        ao = functools.reduce(jnp.add, [e * o for e, o in zip(es, outs)]) / den

    def token_major(ref):
        return jnp.concatenate([ref[c] for c in range(ref.shape[0])], axis=1)

    a_br = _dot(ao.astype(BF16), wa_ref[...])
    r_br = _dot(token_major(ro_ref).astype(BF16), wr_ref[...])
    merged = (jax.nn.sigmoid(token_major(gr_ref).astype(F32)) * r_br
              + jax.nn.sigmoid(token_major(ga_ref).astype(F32)) * a_br)
    o_ref[...] = x_ref[...] + _dot(merged.astype(BF16), wo_ref[...])


def _mix_out(att, ro, proj, x, wr, wa, wo, layer, bm):
    m = x.shape[0]
    if len(att) == 2:
        assert bm % LANES == 0, bm
        att_specs = [pl.BlockSpec((bm, COL), lambda i: (i, 0)),
                     pl.BlockSpec((bm // LANES, COL, LANES), lambda i: (i, 0, 0))]
        scratch = []
    else:
        att_specs = [pl.BlockSpec((a.shape[0], bm // a.shape[0], COL), lambda i: (0, i, 0))
                     for a in att]
        scratch = [pltpu.VMEM((COL // LANES, bm, LANES), F32) for a in att]
    return pl.pallas_call(
        functools.partial(_mix_out_kernel, n_att=len(att)),
        grid=(m // bm,),
        scratch_shapes=scratch,
        in_specs=att_specs + [
            pl.BlockSpec((4, bm, COL), lambda i: (0, i, 0)),
            pl.BlockSpec((4, bm, COL), lambda i: (GR // 4, i, 0)),
            pl.BlockSpec((4, bm, COL), lambda i: (GA // 4, i, 0)),
            pl.BlockSpec((bm, D_MODEL), lambda i: (i, 0)),
            _resident(wr.shape, layer), _resident(wa.shape, layer), _resident(wo.shape, layer)],
        out_specs=pl.BlockSpec((bm, D_MODEL), lambda i: (i, 0)),
        out_shape=jax.ShapeDtypeStruct((m, D_MODEL), F32),
        compiler_params=_params(1),
        name="mix_out",
    )(*att, ro, proj, proj, x, wr, wa, wo)


def _own_head():
    shape = (2 * HEADS_PER_GROUP, ATT_OUT)
    return (lax.broadcasted_iota(jnp.int32, shape, 0)
            == lax.broadcasted_iota(jnp.int32, shape, 1) // HEAD_DIM)


def _sample_scores(qkv, kv, biases, n):
    scale = HEAD_DIM ** -0.5
    stats = []
    for g in range(N_GROUPS):
        qbd = jnp.where(_own_head(), qkv[g][0, 0, pl.ds(n, 1), :], 0.0)
        s = _dot(qbd.astype(BF16), kv[g][0].astype(BF16)) * scale + biases[g][...]
        sn = jnp.sum(qbd * qkv[g][1, 0, pl.ds(n, 1), :], axis=-1, keepdims=True) * scale
        m = jnp.maximum(jnp.max(s, axis=-1, keepdims=True), sn)
        p = jnp.exp(s - m)
        pn = jnp.exp(sn - m)
        l = jnp.sum(p, axis=-1, keepdims=True) + pn
        stats.append((p, pn, l, m + jnp.log(l)))
    return stats


def _sample_values(qkv, kv, n, stats):
    def per_row(a):
        return jnp.concatenate([jnp.broadcast_to(a[h:h + 1, :], (HEAD_DIM, a.shape[1]))
                                for h in range(HEADS_PER_GROUP)], axis=0)

    lses = [s[3] for s in stats]
    top = functools.reduce(jnp.maximum, lses)
    es = [jnp.exp(x - top) for x in lses]
    den = functools.reduce(jnp.add, es)
    col = jnp.zeros((ATT_OUT, 1), F32)
    row = jnp.zeros((2 * HEADS_PER_GROUP, ATT_OUT), F32)
    for g, (p, pn, l, _) in enumerate(stats):
        w = es[g] / (den * l)
        col = col + jnp.sum(kv[g][1] * per_row(p), axis=-1, keepdims=True) * per_row(w)
        row = row + (pn * w) * qkv[g][2, 0, pl.ds(n, 1), :]
    return col, jnp.sum(jnp.where(_own_head(), row, 0.0), axis=0, keepdims=True)


def _ffn_kernel(*refs, final, hosted):
    x_ref, g_ref, wgu_ref, wd_ref, gf_ref = refs[:5]
    if hosted is None:
        o_ref, hn_ref, act_ref = refs[5:]
        seqs = 0
    else:
        layer, seqs = hosted
        qkv, caches, biases = refs[5:8], refs[8:11], refs[11:14]
        o_ref, ao_row_ref, ao_col_ref, hn_ref, act_ref = refs[14:19]
        bufs, sem = refs[19:22], refs[22]
        step, n_seqs = pl.program_id(0), pl.num_programs(0) * seqs
        n_slots = bufs[0].shape[0]

        def copies(n):
            slot = n % n_slots
            out = []
            for g in range(N_GROUPS):
                rows = min(ATT_OUT, CACHE_COPY_ELEMS // caches[g].shape[-1])
                for kv in range(2):
                    for r in range(0, ATT_OUT, rows):
                        out.append(pltpu.make_async_copy(
                            caches[g].at[layer, n, kv, pl.ds(r, rows)],
                            bufs[g].at[slot, kv, pl.ds(r, rows)], sem.at[g, slot]))
            return out

        @pl.when(step == 0)
        def _():
            for ahead in range(n_slots - 1):
                for c in copies(ahead):
                    c.start()
            ao_col_ref[...] = jnp.zeros_like(ao_col_ref)

        def fetch(u):
            n = step * seqs + u
            for c in copies(n):
                c.wait()

            @pl.when(n + n_slots - 1 < n_seqs)
            def _():
                for c in copies(n + n_slots - 1):
                    c.start()

        def window(u):
            n = step * seqs + u
            return n, [bufs[g].at[n % n_slots] for g in range(N_GROUPS)]

    first = [-(-u * FF_BLOCKS // seqs) for u in range(seqs)]
    stats = {}
    x = x_ref[...]
    hn_ref[...] = _rms(x, g_ref[...]).astype(BF16)
    for c in range(FF_BLOCKS):
        if c in first:
            fetch(first.index(c))
        gate = _dot(hn_ref[...], wgu_ref[:, c * COL:(c + 1) * COL])
        up = _dot(hn_ref[...], wgu_ref[:, D_FF + c * COL:D_FF + (c + 1) * COL])
        act_ref[:, c * COL:(c + 1) * COL] = (_silu(gate) * up).astype(BF16)
        if c in first:
            u = first.index(c)
            n, kv = window(u)
            stats[u] = _sample_scores(qkv, kv, biases, n)
        for u in [u for u in stats if c == min(first[u] + 1, (first + [FF_BLOCKS])[u + 1] - 1)]:
            n, kv = window(u)
            col, row = _sample_values(qkv, kv, n, stats.pop(u))
            ao_row_ref[pl.ds(n, 1), :] = row
            lane = lax.broadcasted_iota(jnp.int32, (ATT_OUT, LANES), 1)
            tile = ao_col_ref[n // LANES]
            ao_col_ref[n // LANES] = jnp.where(lane == n % LANES, col, tile)
    y = x + _dot(act_ref[...], wd_ref[...])
    o_ref[...] = _rms(y, gf_ref[...]) if final else y


def _sample_cache_views(caches):
    views, biases = [], []
    slopes = _alibi_slopes()
    for g, (window, dil) in enumerate(ATT_GROUPS):
        depth, n, wlen = caches[g].shape[:3]
        assert wlen == window == ATT_BLOCK * dil, (caches[g].shape, window, dil)
        views.append(caches[g].transpose(0, 1, 3, 4, 5, 2).reshape(depth, n, 2, ATT_OUT, wlen))
        back = wlen - jnp.arange(wlen)
        pen = -slopes[g][:, None] * back.astype(F32)[None, :]
        pen = jnp.where((back % dil == 0)[None, :], pen, MASKED)
        biases.append(jnp.pad(pen, ((0, HEADS_PER_GROUP), (0, 0))))
    return views, biases


def _ffn(x, g, wgu, wd, layer, gf, bm, final, hosted=None):
    m = x.shape[0]
    steps = m // bm
    in_specs = [pl.BlockSpec((bm, D_MODEL), lambda i: (i, 0)),
                _resident((1, D_MODEL)),
                _resident(wgu.shape, layer), _resident(wd.shape, layer),
                _resident((1, D_MODEL))]
    args = [x, g.reshape(1, D_MODEL), wgu, wd, gf.reshape(1, D_MODEL)]
    out_specs = [pl.BlockSpec((bm, D_MODEL), lambda i: (i, 0))]
    out_shape = [jax.ShapeDtypeStruct((m, D_MODEL), F32)]
    scratch = [pltpu.VMEM((bm, D_MODEL), BF16), pltpu.VMEM((bm, D_FF), BF16)]
    kernel_hosted = None
    if hosted is not None:
        qkv, views, biases = hosted
        n = qkv[0].shape[2]
        assert n % steps == 0 and n // steps <= FF_BLOCKS, (n, steps)
        kernel_hosted = (layer, n // steps)
        in_specs += ([_resident(a.shape) for a in qkv]
                     + [pl.BlockSpec(memory_space=pl.ANY) for _ in views]
                     + [_resident(b.shape) for b in biases])
        args += [*qkv, *views, *biases]
        assert n % LANES == 0, n
        out_specs += [pl.BlockSpec((n, ATT_OUT), lambda i: (0, 0)),
                      pl.BlockSpec((n // LANES, ATT_OUT, LANES), lambda i: (0, 0, 0))]
        out_shape += [jax.ShapeDtypeStruct((n, ATT_OUT), F32),
                      jax.ShapeDtypeStruct((n // LANES, ATT_OUT, LANES), F32)]
        scratch += [pltpu.VMEM((CACHE_SLOTS,) + v.shape[2:], F32) for v in views]
        scratch.append(pltpu.SemaphoreType.DMA((N_GROUPS, CACHE_SLOTS)))
    out = pl.pallas_call(
        functools.partial(_ffn_kernel, final=final, hosted=kernel_hosted),
        grid=(steps,),
        in_specs=in_specs,
        out_specs=out_specs,
        out_shape=out_shape,
        scratch_shapes=scratch,
        compiler_params=_params(1),
        name="ffn",
    )(*args)
    return out if hosted is not None else out[0]


def _prep_weights(w_in, w_ret_branch, w_att_branch, w_out, w_gate_up, w_down):
    return tuple(w.astype(BF16) for w in (w_in, w_ret_branch, w_att_branch, w_out, w_gate_up, w_down))


def _new_kv_rows(qkv):
    n = qkv.shape[2]
    return jnp.stack([qkv[1, 0], qkv[2, 0]], axis=1).reshape(n, 1, 2, HEADS_PER_GROUP, HEAD_DIM)


def _block_rows(m, target):
    bm = min(m, target)
    assert m % bm == 0, (m, bm)
    return bm


def kernel(x_prompt, x_sample, state_ret, cache_kv_w128, cache_kv_w512, cache_kv_w2048, norm_mix,
           w_in, w_ret_branch, w_att_branch, w_out, norm_ffn, w_gate_up, w_down, norm_final):
    depth = w_in.shape[0]
    batch, seq, _ = x_prompt.shape
    n_dec, dec_seq, _ = x_sample.shape
    assert dec_seq == 1 and seq % (ATT_BLOCK * ATT_GROUPS[-1][1]) == 0 and seq % RET_CHUNK == 0
    wi, wr, wa, wo, wgu, wd = _prep_weights(w_in, w_ret_branch, w_att_branch, w_out,
                                            w_gate_up, w_down)
    cache_views, cache_biases = _sample_cache_views((cache_kv_w128, cache_kv_w512, cache_kv_w2048))

    m = batch * seq
    bm = _block_rows(m, 512)
    bs = _block_rows(n_dec, 512)
    xp = x_prompt.reshape(m, D_MODEL)
    xs = x_sample.reshape(n_dec, D_MODEL)
    p_ret, p_kv = [], [[] for _ in ATT_GROUPS]
    s_ret, s_kv = None, [[] for _ in ATT_GROUPS]
    for l in range(depth):
        last = l == depth - 1
        proj_s, *qkv_s = _in_proj(xs, norm_mix[l], wi, l, bs, F32, (1,) * N_GROUPS)
        for g in range(N_GROUPS):
            s_kv[g].append(_new_kv_rows(qkv_s[g]))
        keeps = tuple(min(window, seq) for window, _ in ATT_GROUPS)
        proj, *rest = _in_proj(xp, norm_mix[l], wi, l, bm, BF16, DILATIONS, tails=(batch, keeps),
                               hosted=(proj_s, state_ret, s_ret))
        qkv, kv_tails, (ro_s, s_ret) = rest[:N_GROUPS], rest[N_GROUPS:2 * N_GROUPS], rest[2 * N_GROUPS:]
        ro, s_fin = _ret_prompt(proj.reshape(N_MAIN_BLOCKS, batch, seq, COL))
        p_ret.append(s_fin)
        outs, lses = [], []
        for g in range(N_GROUPS):
            o, lse = _attn_prompt(qkv[g], g, batch)
            outs.append(o), lses.append(lse)
            p_kv[g].append(kv_tails[g].reshape(batch, 2, HEADS_PER_GROUP, HEAD_DIM, keeps[g])
                           .transpose(0, 4, 1, 2, 3))
        xp = _mix_out(outs + lses, ro.reshape(4, m, COL), proj, xp, wr, wa, wo, l, bm)
        xp, *ao_s = _ffn(xp, norm_ffn[l], wgu, wd, l, norm_final, bm, last,
                         hosted=(qkv_s, cache_views, cache_biases))
        xs = _mix_out(ao_s, ro_s, proj_s, xs, wr, wa, wo, l, bs)
        xs = _ffn(xs, norm_ffn[l], wgu, wd, l, norm_final, bs, last)
    y_prompt = xp.reshape(batch, seq, D_MODEL)
    y_sample = xs.reshape(n_dec, 1, D_MODEL)

    return (y_prompt, y_sample, jnp.stack(p_ret),
            jnp.stack(p_kv[0]), jnp.stack(p_kv[1]), jnp.stack(p_kv[2]),
            s_ret, jnp.stack(s_kv[0]), jnp.stack(s_kv[1]), jnp.stack(s_kv[2]))
```

```python
import functools

import jax
import jax.numpy as jnp
from jax import lax
from jax.experimental import pallas as pl
from jax.experimental.pallas import tpu as pltpu

F32 = jnp.float32
BF16 = jnp.bfloat16

D_MODEL = 1024
RET_HEADS = 4
RET_DK = 128
RET_DV = 256
RET_CHUNK = 128
ATT_GROUPS = ((128, 1), (512, 4), (2048, 16))
N_GROUPS = 3
HEADS_PER_GROUP = 4
ATT_HEADS = N_GROUPS * HEADS_PER_GROUP
HEAD_DIM = 64
ATT_BLOCK = 128
ATT_OUT = HEADS_PER_GROUP * HEAD_DIM
D_FF = 2816
D_IN = 7424
RMS_EPS = 1e-6
GN_EPS = 1e-5

LANES = 128
COL = 256
N_COL_BLOCKS = D_IN // COL
FF_BLOCKS = D_FF // COL
RQ, RK, RV, RG, GR, GA = 0, 2, 4, 8, 12, 16
N_MAIN_BLOCKS = 20
ATT_SRC_BLOCK = 12
MAIN_SRC_BLOCKS = tuple(range(12)) + tuple(range(21, 29))
DILATIONS = tuple(d for _, d in ATT_GROUPS)
ATT_UNITS_PER_STEP = 8
RET_CHUNKS_PER_STEP = 4
CACHE_SLOTS = 3
CACHE_COPY_ELEMS = HEAD_DIM * 2048
MASKED = -1e30
VMEM_LIMIT = 56 * 1024 * 1024


def _params(n_axes):
    return pltpu.CompilerParams(dimension_semantics=("arbitrary",) * n_axes,
                                vmem_limit_bytes=VMEM_LIMIT)


def _dot(a, b):
    return jnp.dot(a, b, preferred_element_type=F32)


def _dot_nt(a, b):
    return lax.dot_general(a, b, (((1,), (1,)), ((), ())), preferred_element_type=F32)


def _dot_tn(a, b):
    return lax.dot_general(a, b, (((0,), (0,)), ((), ())), preferred_element_type=F32)


def _rms(x, g):
    return x * lax.rsqrt(jnp.mean(x * x, axis=-1, keepdims=True) + RMS_EPS) * g


def _silu(x):
    return x * jax.nn.sigmoid(x)


def _group_norm(o):
    mu = jnp.mean(o, axis=-1, keepdims=True)
    d = o - mu
    var = jnp.mean(d * d, axis=-1, keepdims=True)
    return d * lax.rsqrt(var + GN_EPS)


def _resident(shape, layer=None):
    if layer is None:
        zeros = (0,) * len(shape)
        return pl.BlockSpec(shape, lambda *_: zeros, pipeline_mode=pl.Buffered(1))
    index = (layer,) + (0,) * (len(shape) - 1)
    return pl.BlockSpec((None,) + tuple(shape[1:]), lambda *_: index, pipeline_mode=pl.Buffered(1))


def _retention_host(refs, layer, seqs):
    ps_ref, gam_ref, state_hbm, new_hbm, ro_ref, sin_ref, sout_ref, sem_in, sem_out = refs
    step, n_steps = pl.program_id(0), pl.num_programs(0)

    def loads(at_step):
        half = (at_step % 2) * seqs
        return [pltpu.make_async_copy(state_hbm.at[layer, at_step * seqs + u, h],
                                      sin_ref.at[half + u, h], sem_in.at[half + u])
                for u in range(seqs) for h in range(RET_HEADS)]

    def stores(at_step):
        return [pltpu.make_async_copy(sout_ref.at[u, h], new_hbm.at[layer, at_step * seqs + u, h],
                                      sem_out.at[u])
                for u in range(seqs) for h in range(RET_HEADS)]

    def columns(block, h):
        lo = (h % 2) * RET_DK
        rows = [ps_ref[block + h // 2, pl.ds(step * seqs + u, 1), :][:, lo:lo + RET_DK]
                for u in range(seqs)]
        return jnp.concatenate(rows + [jnp.zeros((8 - seqs, RET_DK), F32)], axis=0).T

    def run():
        @pl.when(step == 0)
        def _():
            for c in loads(0):
                c.start()

        for c in loads(step):
            c.wait()

        @pl.when(step + 1 < n_steps)
        def _():
            for c in loads(step + 1):
                c.start()

        @pl.when(step > 0)
        def _():
            for c in stores(step - 1):
                c.wait()

        half = (step % 2) * seqs
        for h in range(RET_HEADS):
            q_cols, k_cols = columns(RQ, h), columns(RK, h)
            for u in range(seqs):
                n = step * seqs + u
                s1 = (sin_ref[half + u, h] * gam_ref[h]
                      + k_cols[:, u:u + 1] * ps_ref[RV + h, pl.ds(n, 1), :])
                sout_ref[u, h] = s1
                ro_ref[h, pl.ds(n, 1), :] = jnp.sum(s1 * q_cols[:, u:u + 1], axis=0, keepdims=True)
        for c in stores(step):
            c.start()

    def finish():
        @pl.when(step == n_steps - 1)
        def _():
            for c in stores(step):
                c.wait()
            for h in range(RET_HEADS):
                o = ro_ref[h] * (RET_DK ** -0.5)
                ro_ref[h] = _group_norm(o) * _silu(ps_ref[RG + h])

    return run, finish


def _inproj_kernel(*refs, dils, tails, hosted):
    x_ref, g_ref, w_ref = refs[:3]
    refs = refs[3:]
    if hosted is not None:
        layer, seqs, aliased = hosted
        ps_ref, gam_ref, state_hbm = refs[:3]
        refs = refs[4:] if aliased else refs[3:]
    main_ref, a0_ref, a1_ref, a2_ref = refs[:4]
    n_tails = 0 if tails is None else len(tails[1])
    tail_refs = refs[4:4 + n_tails]
    refs = refs[4 + n_tails:]
    finish = None
    if hosted is not None:
        ro_ref, new_hbm = refs[:2]
        xn_ref, hn_ref, hd_ref, sin_ref, sout_ref, sem_in, sem_out = refs[2:]
        run, finish = _retention_host((ps_ref, gam_ref, state_hbm, new_hbm, ro_ref, sin_ref,
                                       sout_ref, sem_in, sem_out), layer, seqs)
        run()
    else:
        xn_ref, hn_ref, hd_ref = refs
    bm = x_ref.shape[0]
    xn = _rms(x_ref[...], g_ref[...])
    hn_ref[...] = xn.astype(BF16)
    dilated = [d for d in dils if d > 1]
    if dilated:
        for c in range(D_MODEL // LANES):
            xn_ref[c] = xn[:, c * LANES:(c + 1) * LANES]
    for slab, d in enumerate(dilated):
        n = bm // d
        for r in range(d):
            for c in range(D_MODEL // LANES):
                hd_ref[slab, r * n:(r + 1) * n, c * LANES:(c + 1) * LANES] = (
                    xn_ref[c, pl.ds(r, n, stride=d), :].astype(BF16))

    def w_block(j):
        return w_ref[:, j * COL:(j + 1) * COL]

    for g, (a_ref, d) in enumerate(zip((a0_ref, a1_ref, a2_ref), dils)):
        lhs = hn_ref[...] if d == 1 else hd_ref[dilated.index(d)]
        for c in range(3):
            out = _dot(lhs, w_block(ATT_SRC_BLOCK + c * N_GROUPS + g))
            a_ref[c] = out.reshape(d, bm // d, COL).astype(a_ref.dtype)
    for j in range(N_MAIN_BLOCKS):
        main_ref[j] = _dot(hn_ref[...], w_block(MAIN_SRC_BLOCKS[j])).astype(main_ref.dtype)
    if tails is not None:
        per_seq, keeps = tails
        in_seq = pl.program_id(0) % per_seq
        longest = max(keep // t_ref.shape[-1] for t_ref, keep in zip(tail_refs, keeps))

        @pl.when(in_seq >= per_seq - longest)
        def _():
            for g, t_ref in enumerate(tail_refs):
                rows = t_ref.shape[-1]
                for c in (1, 2):
                    out = _dot(hn_ref[bm - rows:, :], w_block(ATT_SRC_BLOCK + c * N_GROUPS + g))
                    t_ref[c - 1] = out.T
    if finish is not None:
        finish()


def _in_proj(x, g, w, layer, bm, out_dtype, dils, tails=None, hosted=None):
    m = x.shape[0]
    steps = m // bm
    in_specs = [pl.BlockSpec((bm, D_MODEL), lambda i: (i, 0)),
                _resident((1, D_MODEL)),
                _resident(w.shape, layer)]
    args = [x, g.reshape(1, D_MODEL), w]
    host_specs, host_shapes, host_scratch, aliases, kernel_hosted = [], [], [], {}, None
    if hosted is not None:
        proj_s, state, new_state = hosted
        n = proj_s.shape[1]
        seqs = n // steps
        assert n == seqs * steps and 2 <= n and seqs <= 8, (n, steps)
        gam = jnp.exp(jnp.log1p(-jnp.exp2(-5.0 - jnp.arange(RET_HEADS, dtype=F32))))
        gam = jnp.broadcast_to(gam[:, None, None], (RET_HEADS, 1, RET_DV))
        in_specs += [pl.BlockSpec((RG + 4, n, COL), lambda i: (0, 0, 0), pipeline_mode=pl.Buffered(1)),
                     _resident(gam.shape), pl.BlockSpec(memory_space=pl.ANY)]
        args += [proj_s, gam, state]
        if new_state is not None:
            aliases = {len(args): 4 + (0 if tails is None else len(tails[1])) + 1}
            in_specs.append(pl.BlockSpec(memory_space=pl.ANY))
            args.append(new_state)
        kernel_hosted = (layer, seqs, new_state is not None)
        host_specs = [pl.BlockSpec((RET_HEADS, n, COL), lambda i: (0, 0, 0)),
                      pl.BlockSpec(memory_space=pl.ANY)]
        host_shapes = [jax.ShapeDtypeStruct((RET_HEADS, n, COL), F32),
                       jax.ShapeDtypeStruct(state.shape, F32)]
        host_scratch = [pltpu.VMEM((2 * seqs,) + state.shape[2:], F32),
                        pltpu.VMEM((seqs,) + state.shape[2:], F32),
                        pltpu.SemaphoreType.DMA((2 * seqs,)), pltpu.SemaphoreType.DMA((seqs,))]
    att_specs = [pl.BlockSpec((3, d, bm // d, COL), lambda i: (0, 0, i, 0)) for d in dils]
    att_shapes = [jax.ShapeDtypeStruct((3, d, m // d, COL), out_dtype) for d in dils]
    kernel_tails = None
    if tails is not None:
        batch, keeps = tails
        per_seq = m // batch // bm
        kernel_tails = (per_seq, keeps)
        for keep in keeps:
            rows = min(keep, bm)
            n_tail = keep // rows
            assert keep % rows == 0 and n_tail <= per_seq, (keep, bm, per_seq)
            att_specs.append(pl.BlockSpec(
                (None, 2, ATT_OUT, rows),
                lambda i, n_tail=n_tail: (i // per_seq, 0, 0,
                                          jnp.maximum(i % per_seq - (per_seq - n_tail), 0))))
            att_shapes.append(jax.ShapeDtypeStruct((batch, 2, ATT_OUT, keep), F32))
    return pl.pallas_call(
        functools.partial(_inproj_kernel, dils=dils, tails=kernel_tails, hosted=kernel_hosted),
        grid=(steps,),
        in_specs=in_specs,
        out_specs=([pl.BlockSpec((N_MAIN_BLOCKS, bm, COL), lambda i: (0, i, 0))] + att_specs
                   + host_specs),
        out_shape=([jax.ShapeDtypeStruct((N_MAIN_BLOCKS, m, COL), out_dtype)] + att_shapes
                   + host_shapes),
        scratch_shapes=[pltpu.VMEM((D_MODEL // LANES, bm, LANES), F32),
                        pltpu.VMEM((bm, D_MODEL), BF16),
                        pltpu.VMEM((max(1, sum(d > 1 for d in dils)), bm, D_MODEL), BF16)]
        + host_scratch,
        input_output_aliases=aliases,
        compiler_params=_params(1),
        name="in_proj",
    )(*args)


def _ret_prompt_kernel(q_ref, k_ref, v_ref, rg_ref, dmat_ref, inner_ref, tail_ref, gc_ref,
                       ro_ref, s_ref, *, batch):
    @pl.when(pl.program_id(0) == 0)
    def _():
        s_ref[...] = jnp.zeros_like(s_ref)

    chunk = dmat_ref.shape[1]
    for c in range(q_ref.shape[2] // chunk):
        rows = slice(c * chunk, (c + 1) * chunk)
        for b in range(batch):
            for h in range(RET_HEADS):
                lo = (h % 2) * RET_DK
                q = q_ref[h // 2, b, rows, lo:lo + RET_DK]
                k = k_ref[h // 2, b, rows, lo:lo + RET_DK]
                v = v_ref[h, b, rows, :]
                s0 = s_ref[b, h]
                scores = _dot_nt(q, k) * dmat_ref[h]
                o = _dot(scores.astype(BF16), v) + _dot(q, s0.astype(BF16)) * inner_ref[h]
                kt = (k.astype(F32) * tail_ref[h]).astype(BF16)
                s_ref[b, h] = s0 * gc_ref[h] + _dot_tn(kt, v)
                y = _group_norm(o) * _silu(rg_ref[h, b, rows, :].astype(F32))
                ro_ref[h, b, rows, :] = y.astype(ro_ref.dtype)


def _ret_tables(chunk):
    log_gamma = jnp.log1p(-jnp.exp2(-5.0 - jnp.arange(RET_HEADS, dtype=F32)))
    scale = RET_DK ** -0.5
    idx = jnp.arange(chunk, dtype=F32)
    diff = idx[:, None] - idx[None, :]
    dmat = jnp.where(diff >= 0, jnp.exp(jnp.maximum(diff, 0.0)[None] * log_gamma[:, None, None]), 0.0) * scale
    inner = jnp.exp((idx + 1.0)[None, :] * log_gamma[:, None]) * scale
    tail = jnp.exp((chunk - 1.0 - idx)[None, :] * log_gamma[:, None])
    gc = jnp.exp(chunk * log_gamma)
    inner = jnp.broadcast_to(inner[:, :, None], (RET_HEADS, chunk, RET_DV))
    tail = jnp.broadcast_to(tail[:, :, None], (RET_HEADS, chunk, RET_DK))
    gc = jnp.broadcast_to(gc[:, None, None], (RET_HEADS, 1, RET_DV))
    return dmat, inner, tail, gc


def _ret_prompt(proj4):
    _, b, t, _ = proj4.shape
    dmat, inner, tail, gc = _ret_tables(RET_CHUNK)
    c = RET_CHUNK * RET_CHUNKS_PER_STEP
    assert t % c == 0
    return pl.pallas_call(
        functools.partial(_ret_prompt_kernel, batch=b),
        grid=(t // c,),
        in_specs=[pl.BlockSpec((2, b, c, COL), lambda i: (RQ // 2, 0, i, 0)),
                  pl.BlockSpec((2, b, c, COL), lambda i: (RK // 2, 0, i, 0)),
                  pl.BlockSpec((4, b, c, COL), lambda i: (RV // 4, 0, i, 0)),
                  pl.BlockSpec((4, b, c, COL), lambda i: (RG // 4, 0, i, 0)),
                  _resident(dmat.shape), _resident(inner.shape), _resident(tail.shape),
                  _resident(gc.shape)],
        out_specs=[pl.BlockSpec((4, b, c, COL), lambda i: (0, 0, i, 0)),
                   pl.BlockSpec((b, RET_HEADS, RET_DK, RET_DV), lambda i: (0, 0, 0, 0))],
        out_shape=[jax.ShapeDtypeStruct((4, b, t, COL), BF16),
                   jax.ShapeDtypeStruct((b, RET_HEADS, RET_DK, RET_DV), F32)],
        compiler_params=_params(1),
        name="ret_prompt",
    )(proj4, proj4, proj4, proj4, dmat, inner, tail, gc)


def _attn_prompt_kernel(q_ref, kc_ref, kp_ref, vc_ref, vp_ref, bias_ref, o_ref, lse_ref):
    nres, rows_per_step, _ = q_ref.shape
    first_step = jnp.where(pl.program_id(2) == 0, 0, 1)
    lane_head = lax.broadcasted_iota(jnp.int32, (ATT_BLOCK, ATT_OUT), 1) // HEAD_DIM
    scale = jnp.asarray(HEAD_DIM ** -0.5, BF16)
    for r in range(nres):
        for j in range(rows_per_step // ATT_BLOCK):
            blk = slice(j * ATT_BLOCK, (j + 1) * ATT_BLOCK)
            q = q_ref[r, blk, :] * scale
            qs = jnp.concatenate([jnp.where(lane_head == h, q, jnp.zeros_like(q))
                                  for h in range(HEADS_PER_GROUP)], axis=0)
            if j == 0:
                k_prev, v_prev, bias = kp_ref[r], vp_ref[r], bias_ref[first_step]
            else:
                before = slice((j - 1) * ATT_BLOCK, j * ATT_BLOCK)
                k_prev, v_prev, bias = kc_ref[r, before, :], vc_ref[r, before, :], bias_ref[1]
            kcat = jnp.concatenate([k_prev, kc_ref[r, blk, :]], axis=0)
            vcat = jnp.concatenate([v_prev, vc_ref[r, blk, :]], axis=0)
            s = _dot_nt(qs, kcat) + bias
            m = jnp.max(s, axis=-1, keepdims=True)
            p = jnp.exp(s - m)
            l = jnp.sum(p, axis=-1, keepdims=True)
            on = _dot(p.astype(BF16), vcat) / l
            lse = m + jnp.log(l)
            o = jnp.zeros((ATT_BLOCK, ATT_OUT), F32)
            ls = jnp.zeros((ATT_BLOCK, ATT_OUT), F32)
            for h in range(HEADS_PER_GROUP):
                rows = slice(h * ATT_BLOCK, (h + 1) * ATT_BLOCK)
                o = jnp.where(lane_head == h, on[rows], o)
                ls = jnp.where(lane_head == h, lse[rows], ls)
            o_ref[r, blk, :] = o.astype(o_ref.dtype)
            lse_ref[r, blk, :] = ls


def _alibi_slopes():
    return jnp.exp2(-8.0 * (jnp.arange(ATT_HEADS, dtype=F32) + 1.0) / ATT_HEADS).reshape(
        N_GROUPS, HEADS_PER_GROUP)


def _attn_prompt_bias(g):
    window, dilation = ATT_GROUPS[g]
    steps = window // dilation
    qi = jnp.arange(ATT_BLOCK)[:, None]
    kj = jnp.arange(2 * ATT_BLOCK)[None, :]
    dist = qi + ATT_BLOCK - kj
    valid = (dist >= 0) & (dist <= steps)
    pen = -_alibi_slopes()[g][:, None, None] * (dist * dilation).astype(F32)[None]
    later = jnp.where(valid[None], pen, MASKED)
    first = jnp.where((valid & (kj >= ATT_BLOCK))[None], pen, MASKED)
    return jnp.stack([first, later]).reshape(2, HEADS_PER_GROUP * ATT_BLOCK, 2 * ATT_BLOCK)


def _attn_prompt(qkv, g, batch):
    _, dil, rows, _ = qkv.shape
    nres = min(dil, ATT_UNITS_PER_STEP)
    nblk = ATT_UNITS_PER_STEP // nres
    per_seq = rows // batch // ATT_BLOCK
    assert dil % nres == 0 and per_seq % nblk == 0
    steps = per_seq // nblk
    bias = _attn_prompt_bias(g)

    def cur(c):
        return pl.BlockSpec((None, nres, nblk * ATT_BLOCK, COL),
                            lambda b, r, n: (c, r, b * steps + n, 0))

    def prev(c):
        return pl.BlockSpec((None, nres, ATT_BLOCK, COL),
                            lambda b, r, n: (c, r, b * per_seq + jnp.maximum(n * nblk - 1, 0), 0))

    out_spec = pl.BlockSpec((nres, nblk * ATT_BLOCK, COL), lambda b, r, n: (r, b * steps + n, 0))
    return pl.pallas_call(
        _attn_prompt_kernel,
        grid=(batch, dil // nres, steps),
        in_specs=[cur(0), cur(1), prev(1), cur(2), prev(2), _resident(bias.shape)],
        out_specs=[out_spec, out_spec],
        out_shape=[jax.ShapeDtypeStruct((dil, rows, COL), BF16),
                   jax.ShapeDtypeStruct((dil, rows, COL), F32)],
        compiler_params=_params(3),
        name=f"attn_prompt_g{g}",
    )(qkv, qkv, qkv, qkv, qkv, bias)


def _token_order(ref, scr_ref):
    d, n, _ = ref.shape
    if d == 1:
        return ref[0].astype(F32)
    for r in range(d):
        part = ref[r].astype(F32)
        for c in range(COL // LANES):
            scr_ref[c, pl.ds(r, n, stride=d), :] = part[:, c * LANES:(c + 1) * LANES]
    return jnp.concatenate([scr_ref[c] for c in range(COL // LANES)], axis=1)


def _mix_out_kernel(*refs, n_att):
    att = refs[:n_att]
    ro_ref, gr_ref, ga_ref, x_ref, wr_ref, wa_ref, wo_ref, o_ref = refs[n_att:n_att + 8]
    scratch = refs[n_att + 8:]
    if n_att == 2:
        row_ref, col_ref = att
        ao = row_ref[...] + jnp.concatenate([col_ref[t].T for t in range(col_ref.shape[0])], axis=0)
    else:
        vals = [_token_order(r, s) for r, s in zip(att, scratch)]
        outs, lses = vals[:N_GROUPS], vals[N_GROUPS:]
        top = functools.reduce(jnp.maximum, lses)
        es = [jnp.exp(x - top) for x in lses]
        den = functools.reduce(jnp.add, es)
        ao = functools.reduce(jnp.add, [e * o for e, o in zip(es, outs)]) / den

    def token_major(ref):
        return jnp.concatenate([ref[c] for c in range(ref.shape[0])], axis=1)

    a_br = _dot(ao.astype(BF16), wa_ref[...])
    r_br = _dot(token_major(ro_ref).astype(BF16), wr_ref[...])
    merged = (jax.nn.sigmoid(token_major(gr_ref).astype(F32)) * r_br
              + jax.nn.sigmoid(token_major(ga_ref).astype(F32)) * a_br)
    o_ref[...] = x_ref[...] + _dot(merged.astype(BF16), wo_ref[...])


def _mix_out(att, ro, proj, x, wr, wa, wo, layer, bm):
    m = x.shape[0]
    if len(att) == 2:
        assert bm % LANES == 0, bm
        att_specs = [pl.BlockSpec((bm, COL), lambda i: (i, 0)),
                     pl.BlockSpec((bm // LANES, COL, LANES), lambda i: (i, 0, 0))]
        scratch = []
    else:
        att_specs = [pl.BlockSpec((a.shape[0], bm // a.shape[0], COL), lambda i: (0, i, 0))
                     for a in att]
        scratch = [pltpu.VMEM((COL // LANES, bm, LANES), F32) for a in att]
    return pl.pallas_call(
        functools.partial(_mix_out_kernel, n_att=len(att)),
        grid=(m // bm,),
        scratch_shapes=scratch,
        in_specs=att_specs + [
            pl.BlockSpec((4, bm, COL), lambda i: (0, i, 0)),
            pl.BlockSpec((4, bm, COL), lambda i: (GR // 4, i, 0)),
            pl.BlockSpec((4, bm, COL), lambda i: (GA // 4, i, 0)),
            pl.BlockSpec((bm, D_MODEL), lambda i: (i, 0)),
            _resident(wr.shape, layer), _resident(wa.shape, layer), _resident(wo.shape, layer)],
        out_specs=pl.BlockSpec((bm, D_MODEL), lambda i: (i, 0)),
        out_shape=jax.ShapeDtypeStruct((m, D_MODEL), F32),
        compiler_params=_params(1),
        name="mix_out",
    )(*att, ro, proj, proj, x, wr, wa, wo)


def _own_head():
    shape = (2 * HEADS_PER_GROUP, ATT_OUT)
    return (lax.broadcasted_iota(jnp.int32, shape, 0)
            == lax.broadcasted_iota(jnp.int32, shape, 1) // HEAD_DIM)


def _sample_scores(qkv, kv, biases, n):
    scale = HEAD_DIM ** -0.5
    stats = []
    for g in range(N_GROUPS):
        qbd = jnp.where(_own_head(), qkv[g][0, 0, pl.ds(n, 1), :], 0.0)
        s = _dot(qbd.astype(BF16), kv[g][0].astype(BF16)) * scale + biases[g][...]
        sn = jnp.sum(qbd * qkv[g][1, 0, pl.ds(n, 1), :], axis=-1, keepdims=True) * scale
        m = jnp.maximum(jnp.max(s, axis=-1, keepdims=True), sn)
        p = jnp.exp(s - m)
        pn = jnp.exp(sn - m)
        l = jnp.sum(p, axis=-1, keepdims=True) + pn
        stats.append((p, pn, l, m + jnp.log(l)))
    return stats


def _sample_values(qkv, kv, n, stats):
    def per_row(a):
        return jnp.concatenate([jnp.broadcast_to(a[h:h + 1, :], (HEAD_DIM, a.shape[1]))
                                for h in range(HEADS_PER_GROUP)], axis=0)

    lses = [s[3] for s in stats]
    top = functools.reduce(jnp.maximum, lses)
    es = [jnp.exp(x - top) for x in lses]
    den = functools.reduce(jnp.add, es)
    col = jnp.zeros((ATT_OUT, 1), F32)
    row = jnp.zeros((2 * HEADS_PER_GROUP, ATT_OUT), F32)
    for g, (p, pn, l, _) in enumerate(stats):
        w = es[g] / (den * l)
        col = col + jnp.sum(kv[g][1] * per_row(p), axis=-1, keepdims=True) * per_row(w)
        row = row + (pn * w) * qkv[g][2, 0, pl.ds(n, 1), :]
    return col, jnp.sum(jnp.where(_own_head(), row, 0.0), axis=0, keepdims=True)


def _ffn_kernel(*refs, final, hosted):
    x_ref, g_ref, wgu_ref, wd_ref, gf_ref = refs[:5]
    if hosted is None:
        o_ref, hn_ref, act_ref = refs[5:]
        seqs = 0
    else:
        layer, seqs = hosted
        qkv, caches, biases = refs[5:8], refs[8:11], refs[11:14]
        o_ref, ao_row_ref, ao_col_ref, hn_ref, act_ref = refs[14:19]
        bufs, sem = refs[19:22], refs[22]
        step, n_seqs = pl.program_id(0), pl.num_programs(0) * seqs
        n_slots = bufs[0].shape[0]

        def copies(n):
            slot = n % n_slots
            out = []
            for g in range(N_GROUPS):
                rows = min(ATT_OUT, CACHE_COPY_ELEMS // caches[g].shape[-1])
                for kv in range(2):
                    for r in range(0, ATT_OUT, rows):
                        out.append(pltpu.make_async_copy(
                            caches[g].at[layer, n, kv, pl.ds(r, rows)],
                            bufs[g].at[slot, kv, pl.ds(r, rows)], sem.at[g, slot]))
            return out

        @pl.when(step == 0)
        def _():
            for ahead in range(n_slots - 1):
                for c in copies(ahead):
                    c.start()
            ao_col_ref[...] = jnp.zeros_like(ao_col_ref)

        def fetch(u):
            n = step * seqs + u
            for c in copies(n):
                c.wait()

            @pl.when(n + n_slots - 1 < n_seqs)
            def _():
                for c in copies(n + n_slots - 1):
                    c.start()

        def window(u):
            n = step * seqs + u
            return n, [bufs[g].at[n % n_slots] for g in range(N_GROUPS)]

    first = [-(-u * FF_BLOCKS // seqs) for u in range(seqs)]
    stats = {}
    x = x_ref[...]
    hn_ref[...] = _rms(x, g_ref[...]).astype(BF16)
    for c in range(FF_BLOCKS):
        if c in first:
            fetch(first.index(c))
        gate = _dot(hn_ref[...], wgu_ref[:, c * COL:(c + 1) * COL])
        up = _dot(hn_ref[...], wgu_ref[:, D_FF + c * COL:D_FF + (c + 1) * COL])
        act_ref[:, c * COL:(c + 1) * COL] = (_silu(gate) * up).astype(BF16)
        if c in first:
            u = first.index(c)
            n, kv = window(u)
            stats[u] = _sample_scores(qkv, kv, biases, n)
        for u in [u for u in stats if c == min(first[u] + 1, (first + [FF_BLOCKS])[u + 1] - 1)]:
            n, kv = window(u)
            col, row = _sample_values(qkv, kv, n, stats.pop(u))
            ao_row_ref[pl.ds(n, 1), :] = row
            lane = lax.broadcasted_iota(jnp.int32, (ATT_OUT, LANES), 1)
            tile = ao_col_ref[n // LANES]
            ao_col_ref[n // LANES] = jnp.where(lane == n % LANES, col, tile)
    y = x + _dot(act_ref[...], wd_ref[...])
    o_ref[...] = _rms(y, gf_ref[...]) if final else y


def _sample_cache_views(caches):
    views, biases = [], []
    slopes = _alibi_slopes()
    for g, (window, dil) in enumerate(ATT_GROUPS):
        depth, n, wlen = caches[g].shape[:3]
        assert wlen == window == ATT_BLOCK * dil, (caches[g].shape, window, dil)
        views.append(caches[g].transpose(0, 1, 3, 4, 5, 2).reshape(depth, n, 2, ATT_OUT, wlen))
        back = wlen - jnp.arange(wlen)
        pen = -slopes[g][:, None] * back.astype(F32)[None, :]
        pen = jnp.where((back % dil == 0)[None, :], pen, MASKED)
        biases.append(jnp.pad(pen, ((0, HEADS_PER_GROUP), (0, 0))))
    return views, biases


def _ffn(x, g, wgu, wd, layer, gf, bm, final, hosted=None):
    m = x.shape[0]
    steps = m // bm
    in_specs = [pl.BlockSpec((bm, D_MODEL), lambda i: (i, 0)),
                _resident((1, D_MODEL)),
                _resident(wgu.shape, layer), _resident(wd.shape, layer),
                _resident((1, D_MODEL))]
    args = [x, g.reshape(1, D_MODEL), wgu, wd, gf.reshape(1, D_MODEL)]
    out_specs = [pl.BlockSpec((bm, D_MODEL), lambda i: (i, 0))]
    out_shape = [jax.ShapeDtypeStruct((m, D_MODEL), F32)]
    scratch = [pltpu.VMEM((bm, D_MODEL), BF16), pltpu.VMEM((bm, D_FF), BF16)]
    kernel_hosted = None
    if hosted is not None:
        qkv, views, biases = hosted
        n = qkv[0].shape[2]
        assert n % steps == 0 and n // steps <= FF_BLOCKS, (n, steps)
        kernel_hosted = (layer, n // steps)
        in_specs += ([_resident(a.shape) for a in qkv]
                     + [pl.BlockSpec(memory_space=pl.ANY) for _ in views]
                     + [_resident(b.shape) for b in biases])
        args += [*qkv, *views, *biases]
        assert n % LANES == 0, n
        out_specs += [pl.BlockSpec((n, ATT_OUT), lambda i: (0, 0)),
                      pl.BlockSpec((n // LANES, ATT_OUT, LANES), lambda i: (0, 0, 0))]
        out_shape += [jax.ShapeDtypeStruct((n, ATT_OUT), F32),
                      jax.ShapeDtypeStruct((n // LANES, ATT_OUT, LANES), F32)]
        scratch += [pltpu.VMEM((CACHE_SLOTS,) + v.shape[2:], F32) for v in views]
        scratch.append(pltpu.SemaphoreType.DMA((N_GROUPS, CACHE_SLOTS)))
    out = pl.pallas_call(
        functools.partial(_ffn_kernel, final=final, hosted=kernel_hosted),
        grid=(steps,),
        in_specs=in_specs,
        out_specs=out_specs,
        out_shape=out_shape,
        scratch_shapes=scratch,
        compiler_params=_params(1),
        name="ffn",
    )(*args)
    return out if hosted is not None else out[0]


def _prep_weights(w_in, w_ret_branch, w_att_branch, w_out, w_gate_up, w_down):
    return tuple(w.astype(BF16) for w in (w_in, w_ret_branch, w_att_branch, w_out, w_gate_up, w_down))


def _new_kv_rows(qkv):
    n = qkv.shape[2]
    return jnp.stack([qkv[1, 0], qkv[2, 0]], axis=1).reshape(n, 1, 2, HEADS_PER_GROUP, HEAD_DIM)


def _block_rows(m, target):
    bm = min(m, target)
    assert m % bm == 0, (m, bm)
    return bm


def kernel(x_prompt, x_sample, state_ret, cache_kv_w128, cache_kv_w512, cache_kv_w2048, norm_mix,
           w_in, w_ret_branch, w_att_branch, w_out, norm_ffn, w_gate_up, w_down, norm_final):
    depth = w_in.shape[0]
    batch, seq, _ = x_prompt.shape
    n_dec, dec_seq, _ = x_sample.shape
    assert dec_seq == 1 and seq % (ATT_BLOCK * ATT_GROUPS[-1][1]) == 0 and seq % RET_CHUNK == 0
    wi, wr, wa, wo, wgu, wd = _prep_weights(w_in, w_ret_branch, w_att_branch, w_out,
                                            w_gate_up, w_down)
    cache_views, cache_biases = _sample_cache_views((cache_kv_w128, cache_kv_w512, cache_kv_w2048))

    m = batch * seq
    bm = _block_rows(m, 512)
    bs = _block_rows(n_dec, 512)
    xp = x_prompt.reshape(m, D_MODEL)
    xs = x_sample.reshape(n_dec, D_MODEL)
    p_ret, p_kv = [], [[] for _ in ATT_GROUPS]
    s_ret, s_kv = None, [[] for _ in ATT_GROUPS]
    for l in range(depth):
        last = l == depth - 1
        proj_s, *qkv_s = _in_proj(xs, norm_mix[l], wi, l, bs, F32, (1,) * N_GROUPS)
        for g in range(N_GROUPS):
            s_kv[g].append(_new_kv_rows(qkv_s[g]))
        keeps = tuple(min(window, seq) for window, _ in ATT_GROUPS)
        proj, *rest = _in_proj(xp, norm_mix[l], wi, l, bm, BF16, DILATIONS, tails=(batch, keeps),
                               hosted=(proj_s, state_ret, s_ret))
        qkv, kv_tails, (ro_s, s_ret) = rest[:N_GROUPS], rest[N_GROUPS:2 * N_GROUPS], rest[2 * N_GROUPS:]
        ro, s_fin = _ret_prompt(proj.reshape(N_MAIN_BLOCKS, batch, seq, COL))
        p_ret.append(s_fin)
        outs, lses = [], []
        for g in range(N_GROUPS):
            o, lse = _attn_prompt(qkv[g], g, batch)
            outs.append(o), lses.append(lse)
            p_kv[g].append(kv_tails[g].reshape(batch, 2, HEADS_PER_GROUP, HEAD_DIM, keeps[g])
                           .transpose(0, 4, 1, 2, 3))
        xp = _mix_out(outs + lses, ro.reshape(4, m, COL), proj, xp, wr, wa, wo, l,
                      _block_rows(m, 1024))
        xp, *ao_s = _ffn(xp, norm_ffn[l], wgu, wd, l, norm_final, bm, last,
                         hosted=(qkv_s, cache_views, cache_biases))
        xs = _mix_out(ao_s, ro_s, proj_s, xs, wr, wa, wo, l, bs)
        xs = _ffn(xs, norm_ffn[l], wgu, wd, l, norm_final, bs, last)
    y_prompt = xp.reshape(batch, seq, D_MODEL)
    y_sample = xs.reshape(n_dec, 1, D_MODEL)

    return (y_prompt, y_sample, jnp.stack(p_ret),
            jnp.stack(p_kv[0]), jnp.stack(p_kv[1]), jnp.stack(p_kv[2]),
            s_ret, jnp.stack(s_kv[0]), jnp.stack(s_kv[1]), jnp.stack(s_kv[2]))
```

```python
import functools

import jax
import jax.numpy as jnp
from jax import lax
from jax.experimental import pallas as pl
from jax.experimental.pallas import tpu as pltpu

F32 = jnp.float32
BF16 = jnp.bfloat16

D_MODEL = 1024
RET_HEADS = 4
RET_DK = 128
RET_DV = 256
RET_CHUNK = 128
ATT_GROUPS = ((128, 1), (512, 4), (2048, 16))
N_GROUPS = 3
HEADS_PER_GROUP = 4
ATT_HEADS = N_GROUPS * HEADS_PER_GROUP
HEAD_DIM = 64
ATT_BLOCK = 128
ATT_OUT = HEADS_PER_GROUP * HEAD_DIM
D_FF = 2816
D_IN = 7424
RMS_EPS = 1e-6
GN_EPS = 1e-5

LANES = 128
COL = 256
N_COL_BLOCKS = D_IN // COL
FF_BLOCKS = D_FF // COL
RQ, RK, RV, RG, GR, GA = 0, 2, 4, 8, 12, 16
N_MAIN_BLOCKS = 20
ATT_SRC_BLOCK = 12
MAIN_SRC_BLOCKS = tuple(range(12)) + tuple(range(21, 29))
DILATIONS = tuple(d for _, d in ATT_GROUPS)
ATT_UNITS_PER_STEP = 8
RET_CHUNKS_PER_STEP = 4
CACHE_SLOTS = 4
CACHE_COPY_ELEMS = HEAD_DIM * 2048
MASKED = -1e30
VMEM_LIMIT = 56 * 1024 * 1024


def _params(n_axes):
    return pltpu.CompilerParams(dimension_semantics=("arbitrary",) * n_axes,
                                vmem_limit_bytes=VMEM_LIMIT)


def _dot(a, b):
    return jnp.dot(a, b, preferred_element_type=F32)


def _dot_nt(a, b):
    return lax.dot_general(a, b, (((1,), (1,)), ((), ())), preferred_element_type=F32)


def _dot_tn(a, b):
    return lax.dot_general(a, b, (((0,), (0,)), ((), ())), preferred_element_type=F32)


def _rms(x, g):
    return x * lax.rsqrt(jnp.mean(x * x, axis=-1, keepdims=True) + RMS_EPS) * g


def _silu(x):
    return x * jax.nn.sigmoid(x)


def _group_norm(o):
    mu = jnp.mean(o, axis=-1, keepdims=True)
    d = o - mu
    var = jnp.mean(d * d, axis=-1, keepdims=True)
    return d * lax.rsqrt(var + GN_EPS)


def _resident(shape, layer=None):
    if layer is None:
        zeros = (0,) * len(shape)
        return pl.BlockSpec(shape, lambda *_: zeros, pipeline_mode=pl.Buffered(1))
    index = (layer,) + (0,) * (len(shape) - 1)
    return pl.BlockSpec((None,) + tuple(shape[1:]), lambda *_: index, pipeline_mode=pl.Buffered(1))


def _retention_host(refs, layer, seqs):
    ps_ref, gam_ref, state_hbm, new_hbm, ro_ref, sin_ref, sout_ref, sem_in, sem_out = refs
    step, n_steps = pl.program_id(0), pl.num_programs(0)

    def loads(at_step):
        half = (at_step % 2) * seqs
        return [pltpu.make_async_copy(state_hbm.at[layer, at_step * seqs + u, h],
                                      sin_ref.at[half + u, h], sem_in.at[half + u])
                for u in range(seqs) for h in range(RET_HEADS)]

    def stores(at_step):
        return [pltpu.make_async_copy(sout_ref.at[u, h], new_hbm.at[layer, at_step * seqs + u, h],
                                      sem_out.at[u])
                for u in range(seqs) for h in range(RET_HEADS)]

    def columns(block, h):
        lo = (h % 2) * RET_DK
        rows = [ps_ref[block + h // 2, pl.ds(step * seqs + u, 1), :][:, lo:lo + RET_DK]
                for u in range(seqs)]
        return jnp.concatenate(rows + [jnp.zeros((8 - seqs, RET_DK), F32)], axis=0).T

    def run():
        @pl.when(step == 0)
        def _():
            for c in loads(0):
                c.start()

        for c in loads(step):
            c.wait()

        @pl.when(step + 1 < n_steps)
        def _():
            for c in loads(step + 1):
                c.start()

        @pl.when(step > 0)
        def _():
            for c in stores(step - 1):
                c.wait()

        half = (step % 2) * seqs
        for h in range(RET_HEADS):
            q_cols, k_cols = columns(RQ, h), columns(RK, h)
            for u in range(seqs):
                n = step * seqs + u
                s1 = (sin_ref[half + u, h] * gam_ref[h]
                      + k_cols[:, u:u + 1] * ps_ref[RV + h, pl.ds(n, 1), :])
                sout_ref[u, h] = s1
                ro_ref[h, pl.ds(n, 1), :] = jnp.sum(s1 * q_cols[:, u:u + 1], axis=0, keepdims=True)
        for c in stores(step):
            c.start()

    def finish():
        @pl.when(step == n_steps - 1)
        def _():
            for c in stores(step):
                c.wait()
            for h in range(RET_HEADS):
                o = ro_ref[h] * (RET_DK ** -0.5)
                ro_ref[h] = _group_norm(o) * _silu(ps_ref[RG + h])

    return run, finish


def _inproj_kernel(*refs, dils, tails, hosted):
    x_ref, g_ref, w_ref = refs[:3]
    refs = refs[3:]
    if hosted is not None:
        layer, seqs, aliased = hosted
        ps_ref, gam_ref, state_hbm = refs[:3]
        refs = refs[4:] if aliased else refs[3:]
    main_ref, a0_ref, a1_ref, a2_ref = refs[:4]
    n_tails = 0 if tails is None else len(tails[1])
    tail_refs = refs[4:4 + n_tails]
    refs = refs[4 + n_tails:]
    finish = None
    if hosted is not None:
        ro_ref, new_hbm = refs[:2]
        xn_ref, hn_ref, hd_ref, sin_ref, sout_ref, sem_in, sem_out = refs[2:]
        run, finish = _retention_host((ps_ref, gam_ref, state_hbm, new_hbm, ro_ref, sin_ref,
                                       sout_ref, sem_in, sem_out), layer, seqs)
        run()
    else:
        xn_ref, hn_ref, hd_ref = refs
    bm = x_ref.shape[0]
    xn = _rms(x_ref[...], g_ref[...])
    hn_ref[...] = xn.astype(BF16)
    dilated = [d for d in dils if d > 1]
    if dilated:
        for c in range(D_MODEL // LANES):
            xn_ref[c] = xn[:, c * LANES:(c + 1) * LANES]
    for slab, d in enumerate(dilated):
        n = bm // d
        for r in range(d):
            for c in range(D_MODEL // LANES):
                hd_ref[slab, r * n:(r + 1) * n, c * LANES:(c + 1) * LANES] = (
                    xn_ref[c, pl.ds(r, n, stride=d), :].astype(BF16))

    def w_block(j):
        return w_ref[:, j * COL:(j + 1) * COL]

    for g, (a_ref, d) in enumerate(zip((a0_ref, a1_ref, a2_ref), dils)):
        lhs = hn_ref[...] if d == 1 else hd_ref[dilated.index(d)]
        for c in range(3):
            out = _dot(lhs, w_block(ATT_SRC_BLOCK + c * N_GROUPS + g))
            a_ref[c] = out.reshape(d, bm // d, COL).astype(a_ref.dtype)
    for j in range(N_MAIN_BLOCKS):
        main_ref[j] = _dot(hn_ref[...], w_block(MAIN_SRC_BLOCKS[j])).astype(main_ref.dtype)
    if tails is not None:
        per_seq, keeps = tails
        in_seq = pl.program_id(0) % per_seq
        longest = max(keep // t_ref.shape[-1] for t_ref, keep in zip(tail_refs, keeps))

        @pl.when(in_seq >= per_seq - longest)
        def _():
            for g, t_ref in enumerate(tail_refs):
                rows = t_ref.shape[-1]
                for c in (1, 2):
                    out = _dot(hn_ref[bm - rows:, :], w_block(ATT_SRC_BLOCK + c * N_GROUPS + g))
                    t_ref[c - 1] = out.T
    if finish is not None:
        finish()


def _in_proj(x, g, w, layer, bm, out_dtype, dils, tails=None, hosted=None):
    m = x.shape[0]
    steps = m // bm
    in_specs = [pl.BlockSpec((bm, D_MODEL), lambda i: (i, 0)),
                _resident((1, D_MODEL)),
                _resident(w.shape, layer)]
    args = [x, g.reshape(1, D_MODEL), w]
    host_specs, host_shapes, host_scratch, aliases, kernel_hosted = [], [], [], {}, None
    if hosted is not None:
        proj_s, state, new_state = hosted
        n = proj_s.shape[1]
        seqs = n // steps
        assert n == seqs * steps and 2 <= n and seqs <= 8, (n, steps)
        gam = jnp.exp(jnp.log1p(-jnp.exp2(-5.0 - jnp.arange(RET_HEADS, dtype=F32))))
        gam = jnp.broadcast_to(gam[:, None, None], (RET_HEADS, 1, RET_DV))
        in_specs += [pl.BlockSpec((RG + 4, n, COL), lambda i: (0, 0, 0), pipeline_mode=pl.Buffered(1)),
                     _resident(gam.shape), pl.BlockSpec(memory_space=pl.ANY)]
        args += [proj_s, gam, state]
        if new_state is not None:
            aliases = {len(args): 4 + (0 if tails is None else len(tails[1])) + 1}
            in_specs.append(pl.BlockSpec(memory_space=pl.ANY))
            args.append(new_state)
        kernel_hosted = (layer, seqs, new_state is not None)
        host_specs = [pl.BlockSpec((RET_HEADS, n, COL), lambda i: (0, 0, 0)),
                      pl.BlockSpec(memory_space=pl.ANY)]
        host_shapes = [jax.ShapeDtypeStruct((RET_HEADS, n, COL), F32),
                       jax.ShapeDtypeStruct(state.shape, F32)]
        host_scratch = [pltpu.VMEM((2 * seqs,) + state.shape[2:], F32),
                        pltpu.VMEM((seqs,) + state.shape[2:], F32),
                        pltpu.SemaphoreType.DMA((2 * seqs,)), pltpu.SemaphoreType.DMA((seqs,))]
    att_specs = [pl.BlockSpec((3, d, bm // d, COL), lambda i: (0, 0, i, 0)) for d in dils]
    att_shapes = [jax.ShapeDtypeStruct((3, d, m // d, COL), out_dtype) for d in dils]
    kernel_tails = None
    if tails is not None:
        batch, keeps = tails
        per_seq = m // batch // bm
        kernel_tails = (per_seq, keeps)
        for keep in keeps:
            rows = min(keep, bm)
            n_tail = keep // rows
            assert keep % rows == 0 and n_tail <= per_seq, (keep, bm, per_seq)
            att_specs.append(pl.BlockSpec(
                (None, 2, ATT_OUT, rows),
                lambda i, n_tail=n_tail: (i // per_seq, 0, 0,
                                          jnp.maximum(i % per_seq - (per_seq - n_tail), 0))))
            att_shapes.append(jax.ShapeDtypeStruct((batch, 2, ATT_OUT, keep), F32))
    return pl.pallas_call(
        functools.partial(_inproj_kernel, dils=dils, tails=kernel_tails, hosted=kernel_hosted),
        grid=(steps,),
        in_specs=in_specs,
        out_specs=([pl.BlockSpec((N_MAIN_BLOCKS, bm, COL), lambda i: (0, i, 0))] + att_specs
                   + host_specs),
        out_shape=([jax.ShapeDtypeStruct((N_MAIN_BLOCKS, m, COL), out_dtype)] + att_shapes
                   + host_shapes),
        scratch_shapes=[pltpu.VMEM((D_MODEL // LANES, bm, LANES), F32),
                        pltpu.VMEM((bm, D_MODEL), BF16),
                        pltpu.VMEM((max(1, sum(d > 1 for d in dils)), bm, D_MODEL), BF16)]
        + host_scratch,
        input_output_aliases=aliases,
        compiler_params=_params(1),
        name="in_proj",
    )(*args)


def _ret_prompt_kernel(q_ref, k_ref, v_ref, rg_ref, dmat_ref, inner_ref, tail_ref, gc_ref,
                       ro_ref, s_ref, *, batch):
    @pl.when(pl.program_id(0) == 0)
    def _():
        s_ref[...] = jnp.zeros_like(s_ref)

    chunk = dmat_ref.shape[1]
    for c in range(q_ref.shape[2] // chunk):
        rows = slice(c * chunk, (c + 1) * chunk)
        for b in range(batch):
            for h in range(RET_HEADS):
                lo = (h % 2) * RET_DK
                q = q_ref[h // 2, b, rows, lo:lo + RET_DK]
                k = k_ref[h // 2, b, rows, lo:lo + RET_DK]
                v = v_ref[h, b, rows, :]
                s0 = s_ref[b, h]
                scores = _dot_nt(q, k) * dmat_ref[h]
                o = _dot(scores.astype(BF16), v) + _dot(q, s0.astype(BF16)) * inner_ref[h]
                kt = (k.astype(F32) * tail_ref[h]).astype(BF16)
                s_ref[b, h] = s0 * gc_ref[h] + _dot_tn(kt, v)
                y = _group_norm(o) * _silu(rg_ref[h, b, rows, :].astype(F32))
                ro_ref[h, b, rows, :] = y.astype(ro_ref.dtype)


def _ret_tables(chunk):
    log_gamma = jnp.log1p(-jnp.exp2(-5.0 - jnp.arange(RET_HEADS, dtype=F32)))
    scale = RET_DK ** -0.5
    idx = jnp.arange(chunk, dtype=F32)
    diff = idx[:, None] - idx[None, :]
    dmat = jnp.where(diff >= 0, jnp.exp(jnp.maximum(diff, 0.0)[None] * log_gamma[:, None, None]), 0.0) * scale
    inner = jnp.exp((idx + 1.0)[None, :] * log_gamma[:, None]) * scale
    tail = jnp.exp((chunk - 1.0 - idx)[None, :] * log_gamma[:, None])
    gc = jnp.exp(chunk * log_gamma)
    inner = jnp.broadcast_to(inner[:, :, None], (RET_HEADS, chunk, RET_DV))
    tail = jnp.broadcast_to(tail[:, :, None], (RET_HEADS, chunk, RET_DK))
    gc = jnp.broadcast_to(gc[:, None, None], (RET_HEADS, 1, RET_DV))
    return dmat, inner, tail, gc


def _ret_prompt(proj4):
    _, b, t, _ = proj4.shape
    dmat, inner, tail, gc = _ret_tables(RET_CHUNK)
    c = RET_CHUNK * RET_CHUNKS_PER_STEP
    assert t % c == 0
    return pl.pallas_call(
        functools.partial(_ret_prompt_kernel, batch=b),
        grid=(t // c,),
        in_specs=[pl.BlockSpec((2, b, c, COL), lambda i: (RQ // 2, 0, i, 0)),
                  pl.BlockSpec((2, b, c, COL), lambda i: (RK // 2, 0, i, 0)),
                  pl.BlockSpec((4, b, c, COL), lambda i: (RV // 4, 0, i, 0)),
                  pl.BlockSpec((4, b, c, COL), lambda i: (RG // 4, 0, i, 0)),
                  _resident(dmat.shape), _resident(inner.shape), _resident(tail.shape),
                  _resident(gc.shape)],
        out_specs=[pl.BlockSpec((4, b, c, COL), lambda i: (0, 0, i, 0)),
                   pl.BlockSpec((b, RET_HEADS, RET_DK, RET_DV), lambda i: (0, 0, 0, 0))],
        out_shape=[jax.ShapeDtypeStruct((4, b, t, COL), BF16),
                   jax.ShapeDtypeStruct((b, RET_HEADS, RET_DK, RET_DV), F32)],
        compiler_params=_params(1),
        name="ret_prompt",
    )(proj4, proj4, proj4, proj4, dmat, inner, tail, gc)


def _attn_prompt_kernel(q_ref, kc_ref, kp_ref, vc_ref, vp_ref, bias_ref, o_ref, lse_ref):
    nres, rows_per_step, _ = q_ref.shape
    first_step = jnp.where(pl.program_id(2) == 0, 0, 1)
    lane_head = lax.broadcasted_iota(jnp.int32, (ATT_BLOCK, ATT_OUT), 1) // HEAD_DIM
    scale = jnp.asarray(HEAD_DIM ** -0.5, BF16)
    for r in range(nres):
        for j in range(rows_per_step // ATT_BLOCK):
            blk = slice(j * ATT_BLOCK, (j + 1) * ATT_BLOCK)
            q = q_ref[r, blk, :] * scale
            qs = jnp.concatenate([jnp.where(lane_head == h, q, jnp.zeros_like(q))
                                  for h in range(HEADS_PER_GROUP)], axis=0)
            if j == 0:
                k_prev, v_prev, bias = kp_ref[r], vp_ref[r], bias_ref[first_step]
            else:
                before = slice((j - 1) * ATT_BLOCK, j * ATT_BLOCK)
                k_prev, v_prev, bias = kc_ref[r, before, :], vc_ref[r, before, :], bias_ref[1]
            kcat = jnp.concatenate([k_prev, kc_ref[r, blk, :]], axis=0)
            vcat = jnp.concatenate([v_prev, vc_ref[r, blk, :]], axis=0)
            s = _dot_nt(qs, kcat) + bias
            m = jnp.max(s, axis=-1, keepdims=True)
            p = jnp.exp(s - m)
            l = jnp.sum(p, axis=-1, keepdims=True)
            on = _dot(p.astype(BF16), vcat) / l
            lse = m + jnp.log(l)
            o = jnp.zeros((ATT_BLOCK, ATT_OUT), F32)
            ls = jnp.zeros((ATT_BLOCK, ATT_OUT), F32)
            for h in range(HEADS_PER_GROUP):
                rows = slice(h * ATT_BLOCK, (h + 1) * ATT_BLOCK)
                o = jnp.where(lane_head == h, on[rows], o)
                ls = jnp.where(lane_head == h, lse[rows], ls)
            o_ref[r, blk, :] = o.astype(o_ref.dtype)
            lse_ref[r, blk, :] = ls


def _alibi_slopes():
    return jnp.exp2(-8.0 * (jnp.arange(ATT_HEADS, dtype=F32) + 1.0) / ATT_HEADS).reshape(
        N_GROUPS, HEADS_PER_GROUP)


def _attn_prompt_bias(g):
    window, dilation = ATT_GROUPS[g]
    steps = window // dilation
    qi = jnp.arange(ATT_BLOCK)[:, None]
    kj = jnp.arange(2 * ATT_BLOCK)[None, :]
    dist = qi + ATT_BLOCK - kj
    valid = (dist >= 0) & (dist <= steps)
    pen = -_alibi_slopes()[g][:, None, None] * (dist * dilation).astype(F32)[None]
    later = jnp.where(valid[None], pen, MASKED)
    first = jnp.where((valid & (kj >= ATT_BLOCK))[None], pen, MASKED)
    return jnp.stack([first, later]).reshape(2, HEADS_PER_GROUP * ATT_BLOCK, 2 * ATT_BLOCK)


def _attn_prompt(qkv, g, batch):
    _, dil, rows, _ = qkv.shape
    nres = min(dil, ATT_UNITS_PER_STEP)
    nblk = ATT_UNITS_PER_STEP // nres
    per_seq = rows // batch // ATT_BLOCK
    assert dil % nres == 0 and per_seq % nblk == 0
    steps = per_seq // nblk
    bias = _attn_prompt_bias(g)

    def cur(c):
        return pl.BlockSpec((None, nres, nblk * ATT_BLOCK, COL),
                            lambda b, r, n: (c, r, b * steps + n, 0))

    def prev(c):
        return pl.BlockSpec((None, nres, ATT_BLOCK, COL),
                            lambda b, r, n: (c, r, b * per_seq + jnp.maximum(n * nblk - 1, 0), 0))

    out_spec = pl.BlockSpec((nres, nblk * ATT_BLOCK, COL), lambda b, r, n: (r, b * steps + n, 0))
    return pl.pallas_call(
        _attn_prompt_kernel,
        grid=(batch, dil // nres, steps),
        in_specs=[cur(0), cur(1), prev(1), cur(2), prev(2), _resident(bias.shape)],
        out_specs=[out_spec, out_spec],
        out_shape=[jax.ShapeDtypeStruct((dil, rows, COL), BF16),
                   jax.ShapeDtypeStruct((dil, rows, COL), F32)],
        compiler_params=_params(3),
        name=f"attn_prompt_g{g}",
    )(qkv, qkv, qkv, qkv, qkv, bias)


def _token_order(ref, scr_ref):
    d, n, _ = ref.shape
    if d == 1:
        return ref[0].astype(F32)
    for r in range(d):
        part = ref[r].astype(F32)
        for c in range(COL // LANES):
            scr_ref[c, pl.ds(r, n, stride=d), :] = part[:, c * LANES:(c + 1) * LANES]
    return jnp.concatenate([scr_ref[c] for c in range(COL // LANES)], axis=1)


def _mix_out_kernel(*refs, n_att):
    att = refs[:n_att]
    ro_ref, gr_ref, ga_ref, x_ref, wr_ref, wa_ref, wo_ref, o_ref = refs[n_att:n_att + 8]
    scratch = refs[n_att + 8:]
    if n_att == 2:
        row_ref, col_ref = att
        ao = row_ref[...] + jnp.concatenate([col_ref[t].T for t in range(col_ref.shape[0])], axis=0)
    else:
        vals = [_token_order(r, s) for r, s in zip(att, scratch)]
        outs, lses = vals[:N_GROUPS], vals[N_GROUPS:]
        top = functools.reduce(jnp.maximum, lses)
        es = [jnp.exp(x - top) for x in lses]
        den = functools.reduce(jnp.add, es)
        ao = functools.reduce(jnp.add, [e * o for e, o in zip(es, outs)]) / den

    def token_major(ref):
        return jnp.concatenate([ref[c] for c in range(ref.shape[0])], axis=1)

    a_br = _dot(ao.astype(BF16), wa_ref[...])
    r_br = _dot(token_major(ro_ref).astype(BF16), wr_ref[...])
    merged = (jax.nn.sigmoid(token_major(gr_ref).astype(F32)) * r_br
              + jax.nn.sigmoid(token_major(ga_ref).astype(F32)) * a_br)
    o_ref[...] = x_ref[...] + _dot(merged.astype(BF16), wo_ref[...])


def _mix_out(att, ro, proj, x, wr, wa, wo, layer, bm):
    m = x.shape[0]
    if len(att) == 2:
        assert bm % LANES == 0, bm
        att_specs = [pl.BlockSpec((bm, COL), lambda i: (i, 0)),
                     pl.BlockSpec((bm // LANES, COL, LANES), lambda i: (i, 0, 0))]
        scratch = []
    else:
        att_specs = [pl.BlockSpec((a.shape[0], bm // a.shape[0], COL), lambda i: (0, i, 0))
                     for a in att]
        scratch = [pltpu.VMEM((COL // LANES, bm, LANES), F32) for a in att]
    return pl.pallas_call(
        functools.partial(_mix_out_kernel, n_att=len(att)),
        grid=(m // bm,),
        scratch_shapes=scratch,
        in_specs=att_specs + [
            pl.BlockSpec((4, bm, COL), lambda i: (0, i, 0)),
            pl.BlockSpec((4, bm, COL), lambda i: (GR // 4, i, 0)),
            pl.BlockSpec((4, bm, COL), lambda i: (GA // 4, i, 0)),
            pl.BlockSpec((bm, D_MODEL), lambda i: (i, 0)),
            _resident(wr.shape, layer), _resident(wa.shape, layer), _resident(wo.shape, layer)],
        out_specs=pl.BlockSpec((bm, D_MODEL), lambda i: (i, 0)),
        out_shape=jax.ShapeDtypeStruct((m, D_MODEL), F32),
        compiler_params=_params(1),
        name="mix_out",
    )(*att, ro, proj, proj, x, wr, wa, wo)


def _own_head():
    shape = (2 * HEADS_PER_GROUP, ATT_OUT)
    return (lax.broadcasted_iota(jnp.int32, shape, 0)
            == lax.broadcasted_iota(jnp.int32, shape, 1) // HEAD_DIM)


def _sample_scores(qkv, kv, biases, n):
    scale = HEAD_DIM ** -0.5
    stats = []
    for g in range(N_GROUPS):
        qbd = jnp.where(_own_head(), qkv[g][0, 0, pl.ds(n, 1), :], 0.0)
        s = _dot(qbd.astype(BF16), kv[g][0].astype(BF16)) * scale + biases[g][...]
        sn = jnp.sum(qbd * qkv[g][1, 0, pl.ds(n, 1), :], axis=-1, keepdims=True) * scale
        m = jnp.maximum(jnp.max(s, axis=-1, keepdims=True), sn)
        p = jnp.exp(s - m)
        pn = jnp.exp(sn - m)
        l = jnp.sum(p, axis=-1, keepdims=True) + pn
        stats.append((p, pn, l, m + jnp.log(l)))
    return stats


def _sample_values(qkv, kv, n, stats):
    def per_row(a):
        return jnp.concatenate([jnp.broadcast_to(a[h:h + 1, :], (HEAD_DIM, a.shape[1]))
                                for h in range(HEADS_PER_GROUP)], axis=0)

    lses = [s[3] for s in stats]
    top = functools.reduce(jnp.maximum, lses)
    es = [jnp.exp(x - top) for x in lses]
    den = functools.reduce(jnp.add, es)
    col = jnp.zeros((ATT_OUT, 1), F32)
    row = jnp.zeros((2 * HEADS_PER_GROUP, ATT_OUT), F32)
    for g, (p, pn, l, _) in enumerate(stats):
        w = es[g] / (den * l)
        col = col + jnp.sum(kv[g][1] * per_row(p), axis=-1, keepdims=True) * per_row(w)
        row = row + (pn * w) * qkv[g][2, 0, pl.ds(n, 1), :]
    return col, jnp.sum(jnp.where(_own_head(), row, 0.0), axis=0, keepdims=True)


def _ffn_kernel(*refs, final, hosted):
    x_ref, g_ref, wgu_ref, wd_ref, gf_ref = refs[:5]
    if hosted is None:
        o_ref, hn_ref, act_ref = refs[5:]
        fetch_before, score_after, value_after = {}, {}, {}
    else:
        layer, seqs = hosted
        qkv, caches, biases = refs[5:8], refs[8:11], refs[11:14]
        o_ref, ao_row_ref, ao_col_ref, hn_ref, act_ref = refs[14:19]
        bufs, sem = refs[19:22], refs[22]
        step, n_seqs = pl.program_id(0), pl.num_programs(0) * seqs
        n_slots = bufs[0].shape[0]

        def copies(n):
            slot = n % n_slots
            out = []
            for g in range(N_GROUPS):
                rows = min(ATT_OUT, CACHE_COPY_ELEMS // caches[g].shape[-1])
                for kv in range(2):
                    for r in range(0, ATT_OUT, rows):
                        out.append(pltpu.make_async_copy(
                            caches[g].at[layer, n, kv, pl.ds(r, rows)],
                            bufs[g].at[slot, kv, pl.ds(r, rows)], sem.at[g, slot]))
            return out

        group = n_slots // 2
        assert n_slots == 2 * group and seqs % group == 0, (n_slots, seqs)

        def start_group(first_seq):
            for n in range(group):
                for c in copies(first_seq + n):
                    c.start()

        @pl.when(step == 0)
        def _():
            start_group(0)
            ao_col_ref[...] = jnp.zeros_like(ao_col_ref)

        def fetch(p):
            n0 = step * seqs + p * group
            for n in range(group):
                for c in copies(n0 + n):
                    c.wait()

            @pl.when(n0 + group < n_seqs)
            def _():
                start_group(n0 + group)

        def window(u):
            n = step * seqs + u
            return n, [bufs[g].at[n % n_slots] for g in range(N_GROUPS)]

        starts = [-(-p * group * FF_BLOCKS // seqs) for p in range(seqs // group)]
        fetch_before = {s: p for p, s in enumerate(starts)}
        score_after, value_after = {}, {}
        for p, s in enumerate(starts):
            for i in range(group):
                score_after.setdefault(min(s + i, FF_BLOCKS - 1), []).append(p * group + i)
                value_after.setdefault(min(s + group + i, FF_BLOCKS - 1), []).append(p * group + i)

    stats = {}
    x = x_ref[...]
    hn_ref[...] = _rms(x, g_ref[...]).astype(BF16)
    for c in range(FF_BLOCKS):
        if c in fetch_before:
            fetch(fetch_before[c])
        gate = _dot(hn_ref[...], wgu_ref[:, c * COL:(c + 1) * COL])
        up = _dot(hn_ref[...], wgu_ref[:, D_FF + c * COL:D_FF + (c + 1) * COL])
        act_ref[:, c * COL:(c + 1) * COL] = (_silu(gate) * up).astype(BF16)
        for u in score_after.get(c, ()):
            n, kv = window(u)
            stats[u] = _sample_scores(qkv, kv, biases, n)
        for u in value_after.get(c, ()):
            n, kv = window(u)
            col, row = _sample_values(qkv, kv, n, stats.pop(u))
            ao_row_ref[pl.ds(n, 1), :] = row
            lane = lax.broadcasted_iota(jnp.int32, (ATT_OUT, LANES), 1)
            tile = ao_col_ref[n // LANES]
            ao_col_ref[n // LANES] = jnp.where(lane == n % LANES, col, tile)
    y = x + _dot(act_ref[...], wd_ref[...])
    o_ref[...] = _rms(y, gf_ref[...]) if final else y


def _sample_cache_views(caches):
    views, biases = [], []
    slopes = _alibi_slopes()
    for g, (window, dil) in enumerate(ATT_GROUPS):
        depth, n, wlen = caches[g].shape[:3]
        assert wlen == window == ATT_BLOCK * dil, (caches[g].shape, window, dil)
        views.append(caches[g].transpose(0, 1, 3, 4, 5, 2).reshape(depth, n, 2, ATT_OUT, wlen))
        back = wlen - jnp.arange(wlen)
        pen = -slopes[g][:, None] * back.astype(F32)[None, :]
        pen = jnp.where((back % dil == 0)[None, :], pen, MASKED)
        biases.append(jnp.pad(pen, ((0, HEADS_PER_GROUP), (0, 0))))
    return views, biases


def _ffn(x, g, wgu, wd, layer, gf, bm, final, hosted=None):
    m = x.shape[0]
    steps = m // bm
    in_specs = [pl.BlockSpec((bm, D_MODEL), lambda i: (i, 0)),
                _resident((1, D_MODEL)),
                _resident(wgu.shape, layer), _resident(wd.shape, layer),
                _resident((1, D_MODEL))]
    args = [x, g.reshape(1, D_MODEL), wgu, wd, gf.reshape(1, D_MODEL)]
    out_specs = [pl.BlockSpec((bm, D_MODEL), lambda i: (i, 0))]
    out_shape = [jax.ShapeDtypeStruct((m, D_MODEL), F32)]
    scratch = [pltpu.VMEM((bm, D_MODEL), BF16), pltpu.VMEM((bm, D_FF), BF16)]
    kernel_hosted = None
    if hosted is not None:
        qkv, views, biases = hosted
        n = qkv[0].shape[2]
        assert n % steps == 0 and n // steps <= FF_BLOCKS, (n, steps)
        kernel_hosted = (layer, n // steps)
        in_specs += ([_resident(a.shape) for a in qkv]
                     + [pl.BlockSpec(memory_space=pl.ANY) for _ in views]
                     + [_resident(b.shape) for b in biases])
        args += [*qkv, *views, *biases]
        assert n % LANES == 0, n
        out_specs += [pl.BlockSpec((n, ATT_OUT), lambda i: (0, 0)),
                      pl.BlockSpec((n // LANES, ATT_OUT, LANES), lambda i: (0, 0, 0))]
        out_shape += [jax.ShapeDtypeStruct((n, ATT_OUT), F32),
                      jax.ShapeDtypeStruct((n // LANES, ATT_OUT, LANES), F32)]
        scratch += [pltpu.VMEM((CACHE_SLOTS,) + v.shape[2:], F32) for v in views]
        scratch.append(pltpu.SemaphoreType.DMA((N_GROUPS, CACHE_SLOTS)))
    out = pl.pallas_call(
        functools.partial(_ffn_kernel, final=final, hosted=kernel_hosted),
        grid=(steps,),
        in_specs=in_specs,
        out_specs=out_specs,
        out_shape=out_shape,
        scratch_shapes=scratch,
        compiler_params=_params(1),
        name="ffn",
    )(*args)
    return out if hosted is not None else out[0]


def _prep_weights(w_in, w_ret_branch, w_att_branch, w_out, w_gate_up, w_down):
    return tuple(w.astype(BF16) for w in (w_in, w_ret_branch, w_att_branch, w_out, w_gate_up, w_down))


def _new_kv_rows(qkv):
    n = qkv.shape[2]
    return jnp.stack([qkv[1, 0], qkv[2, 0]], axis=1).reshape(n, 1, 2, HEADS_PER_GROUP, HEAD_DIM)


def _block_rows(m, target):
    bm = min(m, target)
    assert m % bm == 0, (m, bm)
    return bm


def kernel(x_prompt, x_sample, state_ret, cache_kv_w128, cache_kv_w512, cache_kv_w2048, norm_mix,
           w_in, w_ret_branch, w_att_branch, w_out, norm_ffn, w_gate_up, w_down, norm_final):
    depth = w_in.shape[0]
    batch, seq, _ = x_prompt.shape
    n_dec, dec_seq, _ = x_sample.shape
    assert dec_seq == 1 and seq % (ATT_BLOCK * ATT_GROUPS[-1][1]) == 0 and seq % RET_CHUNK == 0
    wi, wr, wa, wo, wgu, wd = _prep_weights(w_in, w_ret_branch, w_att_branch, w_out,
                                            w_gate_up, w_down)
    cache_views, cache_biases = _sample_cache_views((cache_kv_w128, cache_kv_w512, cache_kv_w2048))

    m = batch * seq
    bm = _block_rows(m, 512)
    bs = _block_rows(n_dec, 512)
    xp = x_prompt.reshape(m, D_MODEL)
    xs = x_sample.reshape(n_dec, D_MODEL)
    p_ret, p_kv = [], [[] for _ in ATT_GROUPS]
    s_ret, s_kv = None, [[] for _ in ATT_GROUPS]
    for l in range(depth):
        last = l == depth - 1
        proj_s, *qkv_s = _in_proj(xs, norm_mix[l], wi, l, bs, F32, (1,) * N_GROUPS)
        for g in range(N_GROUPS):
            s_kv[g].append(_new_kv_rows(qkv_s[g]))
        keeps = tuple(min(window, seq) for window, _ in ATT_GROUPS)
        proj, *rest = _in_proj(xp, norm_mix[l], wi, l, bm, BF16, DILATIONS, tails=(batch, keeps),
                               hosted=(proj_s, state_ret, s_ret))
        qkv, kv_tails, (ro_s, s_ret) = rest[:N_GROUPS], rest[N_GROUPS:2 * N_GROUPS], rest[2 * N_GROUPS:]
        ro, s_fin = _ret_prompt(proj.reshape(N_MAIN_BLOCKS, batch, seq, COL))
        p_ret.append(s_fin)
        outs, lses = [], []
        for g in range(N_GROUPS):
            o, lse = _attn_prompt(qkv[g], g, batch)
            outs.append(o), lses.append(lse)
            p_kv[g].append(kv_tails[g].reshape(batch, 2, HEADS_PER_GROUP, HEAD_DIM, keeps[g])
                           .transpose(0, 4, 1, 2, 3))
        xp = _mix_out(outs + lses, ro.reshape(4, m, COL), proj, xp, wr, wa, wo, l,
                      _block_rows(m, 1024))
        xp, *ao_s = _ffn(xp, norm_ffn[l], wgu, wd, l, norm_final, bm, last,
                         hosted=(qkv_s, cache_views, cache_biases))
        xs = _mix_out(ao_s, ro_s, proj_s, xs, wr, wa, wo, l, bs)
        xs = _ffn(xs, norm_ffn[l], wgu, wd, l, norm_final, bs, last)
    y_prompt = xp.reshape(batch, seq, D_MODEL)
    y_sample = xs.reshape(n_dec, 1, D_MODEL)

    return (y_prompt, y_sample, jnp.stack(p_ret),
            jnp.stack(p_kv[0]), jnp.stack(p_kv[1]), jnp.stack(p_kv[2]),
            s_ret, jnp.stack(s_kv[0]), jnp.stack(s_kv[1]), jnp.stack(s_kv[2]))
```

```python
import functools

import jax
import jax.numpy as jnp
from jax import lax
from jax.experimental import pallas as pl
from jax.experimental.pallas import tpu as pltpu

F32 = jnp.float32
BF16 = jnp.bfloat16

D_MODEL = 1024
RET_HEADS = 4
RET_DK = 128
RET_DV = 256
RET_CHUNK = 128
ATT_GROUPS = ((128, 1), (512, 4), (2048, 16))
N_GROUPS = 3
HEADS_PER_GROUP = 4
ATT_HEADS = N_GROUPS * HEADS_PER_GROUP
HEAD_DIM = 64
ATT_BLOCK = 128
ATT_OUT = HEADS_PER_GROUP * HEAD_DIM
D_FF = 2816
D_IN = 7424
RMS_EPS = 1e-6
GN_EPS = 1e-5

LANES = 128
COL = 256
N_COL_BLOCKS = D_IN // COL
FF_BLOCKS = D_FF // COL
RQ, RK, RV, RG, GR, GA = 0, 2, 4, 8, 12, 16
N_MAIN_BLOCKS = 20
ATT_SRC_BLOCK = 12
MAIN_SRC_BLOCKS = tuple(range(12)) + tuple(range(21, 29))
DILATIONS = tuple(d for _, d in ATT_GROUPS)
ATT_UNITS_PER_STEP = 8
RET_CHUNKS_PER_STEP = 4
CACHE_SLOTS = 3
CACHE_GROUP = 1
CACHE_COPY_ELEMS = HEAD_DIM * 2048
MASKED = -1e30
VMEM_LIMIT = 56 * 1024 * 1024


def _params(n_axes):
    return pltpu.CompilerParams(dimension_semantics=("arbitrary",) * n_axes,
                                vmem_limit_bytes=VMEM_LIMIT)


def _dot(a, b):
    return jnp.dot(a, b, preferred_element_type=F32)


def _dot_nt(a, b):
    return lax.dot_general(a, b, (((1,), (1,)), ((), ())), preferred_element_type=F32)


def _dot_tn(a, b):
    return lax.dot_general(a, b, (((0,), (0,)), ((), ())), preferred_element_type=F32)


def _rms(x, g):
    return x * lax.rsqrt(jnp.mean(x * x, axis=-1, keepdims=True) + RMS_EPS) * g


def _silu(x):
    return x * jax.nn.sigmoid(x)


def _group_norm(o):
    mu = jnp.mean(o, axis=-1, keepdims=True)
    d = o - mu
    var = jnp.mean(d * d, axis=-1, keepdims=True)
    return d * lax.rsqrt(var + GN_EPS)


def _resident(shape, layer=None):
    if layer is None:
        zeros = (0,) * len(shape)
        return pl.BlockSpec(shape, lambda *_: zeros, pipeline_mode=pl.Buffered(1))
    index = (layer,) + (0,) * (len(shape) - 1)
    return pl.BlockSpec((None,) + tuple(shape[1:]), lambda *_: index, pipeline_mode=pl.Buffered(1))


def _retention_host(refs, layer, seqs):
    ps_ref, gam_ref, state_hbm, new_hbm, ro_ref, sin_ref, sout_ref, sem_in, sem_out = refs
    step, n_steps = pl.program_id(0), pl.num_programs(0)

    def loads(at_step):
        half = (at_step % 2) * seqs
        return [pltpu.make_async_copy(state_hbm.at[layer, at_step * seqs + u, h],
                                      sin_ref.at[half + u, h], sem_in.at[half + u])
                for u in range(seqs) for h in range(RET_HEADS)]

    def stores(at_step):
        return [pltpu.make_async_copy(sout_ref.at[u, h], new_hbm.at[layer, at_step * seqs + u, h],
                                      sem_out.at[u])
                for u in range(seqs) for h in range(RET_HEADS)]

    def columns(block, h):
        lo = (h % 2) * RET_DK
        rows = [ps_ref[block + h // 2, pl.ds(step * seqs + u, 1), :][:, lo:lo + RET_DK]
                for u in range(seqs)]
        return jnp.concatenate(rows + [jnp.zeros((8 - seqs, RET_DK), F32)], axis=0).T

    def run():
        @pl.when(step == 0)
        def _():
            for c in loads(0):
                c.start()

        for c in loads(step):
            c.wait()

        @pl.when(step + 1 < n_steps)
        def _():
            for c in loads(step + 1):
                c.start()

        @pl.when(step > 0)
        def _():
            for c in stores(step - 1):
                c.wait()

        half = (step % 2) * seqs
        for h in range(RET_HEADS):
            q_cols, k_cols = columns(RQ, h), columns(RK, h)
            for u in range(seqs):
                n = step * seqs + u
                s1 = (sin_ref[half + u, h] * gam_ref[h]
                      + k_cols[:, u:u + 1] * ps_ref[RV + h, pl.ds(n, 1), :])
                sout_ref[u, h] = s1
                ro_ref[h, pl.ds(n, 1), :] = jnp.sum(s1 * q_cols[:, u:u + 1], axis=0, keepdims=True)
        for c in stores(step):
            c.start()

    def finish():
        @pl.when(step == n_steps - 1)
        def _():
            for c in stores(step):
                c.wait()
            for h in range(RET_HEADS):
                o = ro_ref[h] * (RET_DK ** -0.5)
                ro_ref[h] = _group_norm(o) * _silu(ps_ref[RG + h])

    return run, finish


def _inproj_kernel(*refs, dils, tails, hosted):
    x_ref, g_ref, w_ref = refs[:3]
    refs = refs[3:]
    if hosted is not None:
        layer, seqs, aliased = hosted
        ps_ref, gam_ref, state_hbm = refs[:3]
        refs = refs[4:] if aliased else refs[3:]
    main_ref, a0_ref, a1_ref, a2_ref = refs[:4]
    n_tails = 0 if tails is None else len(tails[1])
    tail_refs = refs[4:4 + n_tails]
    refs = refs[4 + n_tails:]
    finish = None
    if hosted is not None:
        ro_ref, new_hbm = refs[:2]
        xn_ref, hn_ref, hd_ref, sin_ref, sout_ref, sem_in, sem_out = refs[2:]
        run, finish = _retention_host((ps_ref, gam_ref, state_hbm, new_hbm, ro_ref, sin_ref,
                                       sout_ref, sem_in, sem_out), layer, seqs)
        run()
    else:
        xn_ref, hn_ref, hd_ref = refs
    bm = x_ref.shape[0]
    xn = _rms(x_ref[...], g_ref[...])
    hn_ref[...] = xn.astype(BF16)
    dilated = [d for d in dils if d > 1]
    if dilated:
        for c in range(D_MODEL // LANES):
            xn_ref[c] = xn[:, c * LANES:(c + 1) * LANES]
    for slab, d in enumerate(dilated):
        n = bm // d
        for r in range(d):
            for c in range(D_MODEL // LANES):
                hd_ref[slab, r * n:(r + 1) * n, c * LANES:(c + 1) * LANES] = (
                    xn_ref[c, pl.ds(r, n, stride=d), :].astype(BF16))

    def w_block(j):
        return w_ref[:, j * COL:(j + 1) * COL]

    for g, (a_ref, d) in enumerate(zip((a0_ref, a1_ref, a2_ref), dils)):
        lhs = hn_ref[...] if d == 1 else hd_ref[dilated.index(d)]
        for c in range(3):
            out = _dot(lhs, w_block(ATT_SRC_BLOCK + c * N_GROUPS + g))
            a_ref[c] = out.reshape(d, bm // d, COL).astype(a_ref.dtype)
    for j in range(N_MAIN_BLOCKS):
        main_ref[j] = _dot(hn_ref[...], w_block(MAIN_SRC_BLOCKS[j])).astype(main_ref.dtype)
    if tails is not None:
        per_seq, keeps = tails
        in_seq = pl.program_id(0) % per_seq
        for g, (t_ref, keep) in enumerate(zip(tail_refs, keeps)):
            rows = t_ref.shape[-1]

            @pl.when(in_seq >= per_seq - keep // rows)
            def _():
                for c in (1, 2):
                    out = _dot(hn_ref[bm - rows:, :], w_block(ATT_SRC_BLOCK + c * N_GROUPS + g))
                    t_ref[c - 1] = out.T
    if finish is not None:
        finish()


def _in_proj(x, g, w, layer, bm, out_dtype, dils, tails=None, hosted=None):
    m = x.shape[0]
    steps = m // bm
    in_specs = [pl.BlockSpec((bm, D_MODEL), lambda i: (i, 0)),
                _resident((1, D_MODEL)),
                _resident(w.shape, layer)]
    args = [x, g.reshape(1, D_MODEL), w]
    host_specs, host_shapes, host_scratch, aliases, kernel_hosted = [], [], [], {}, None
    if hosted is not None:
        proj_s, state, new_state = hosted
        n = proj_s.shape[1]
        seqs = n // steps
        assert n == seqs * steps and 2 <= n and seqs <= 8, (n, steps)
        gam = jnp.exp(jnp.log1p(-jnp.exp2(-5.0 - jnp.arange(RET_HEADS, dtype=F32))))
        gam = jnp.broadcast_to(gam[:, None, None], (RET_HEADS, 1, RET_DV))
        in_specs += [pl.BlockSpec((RG + 4, n, COL), lambda i: (0, 0, 0), pipeline_mode=pl.Buffered(1)),
                     _resident(gam.shape), pl.BlockSpec(memory_space=pl.ANY)]
        args += [proj_s, gam, state]
        if new_state is not None:
            aliases = {len(args): 4 + (0 if tails is None else len(tails[1])) + 1}
            in_specs.append(pl.BlockSpec(memory_space=pl.ANY))
            args.append(new_state)
        kernel_hosted = (layer, seqs, new_state is not None)
        host_specs = [pl.BlockSpec((RET_HEADS, n, COL), lambda i: (0, 0, 0)),
                      pl.BlockSpec(memory_space=pl.ANY)]
        host_shapes = [jax.ShapeDtypeStruct((RET_HEADS, n, COL), F32),
                       jax.ShapeDtypeStruct(state.shape, F32)]
        host_scratch = [pltpu.VMEM((2 * seqs,) + state.shape[2:], F32),
                        pltpu.VMEM((seqs,) + state.shape[2:], F32),
                        pltpu.SemaphoreType.DMA((2 * seqs,)), pltpu.SemaphoreType.DMA((seqs,))]
    att_specs = [pl.BlockSpec((3, d, bm // d, COL), lambda i: (0, 0, i, 0)) for d in dils]
    att_shapes = [jax.ShapeDtypeStruct((3, d, m // d, COL), out_dtype) for d in dils]
    kernel_tails = None
    if tails is not None:
        batch, keeps = tails
        per_seq = m // batch // bm
        kernel_tails = (per_seq, keeps)
        for keep in keeps:
            rows = min(keep, bm)
            n_tail = keep // rows
            assert keep % rows == 0 and n_tail <= per_seq, (keep, bm, per_seq)
            att_specs.append(pl.BlockSpec(
                (None, 2, ATT_OUT, rows),
                lambda i, n_tail=n_tail: (i // per_seq, 0, 0,
                                          jnp.maximum(i % per_seq - (per_seq - n_tail), 0))))
            att_shapes.append(jax.ShapeDtypeStruct((batch, 2, ATT_OUT, keep), F32))
    return pl.pallas_call(
        functools.partial(_inproj_kernel, dils=dils, tails=kernel_tails, hosted=kernel_hosted),
        grid=(steps,),
        in_specs=in_specs,
        out_specs=([pl.BlockSpec((N_MAIN_BLOCKS, bm, COL), lambda i: (0, i, 0))] + att_specs
                   + host_specs),
        out_shape=([jax.ShapeDtypeStruct((N_MAIN_BLOCKS, m, COL), out_dtype)] + att_shapes
                   + host_shapes),
        scratch_shapes=[pltpu.VMEM((D_MODEL // LANES, bm, LANES), F32),
                        pltpu.VMEM((bm, D_MODEL), BF16),
                        pltpu.VMEM((max(1, sum(d > 1 for d in dils)), bm, D_MODEL), BF16)]
        + host_scratch,
        input_output_aliases=aliases,
        compiler_params=_params(1),
        name="in_proj",
    )(*args)


def _ret_prompt_kernel(q_ref, k_ref, v_ref, rg_ref, dmat_ref, inner_ref, tail_ref, gc_ref,
                       ro_ref, s_ref, *, batch):
    @pl.when(pl.program_id(0) == 0)
    def _():
        s_ref[...] = jnp.zeros_like(s_ref)

    chunk = dmat_ref.shape[1]
    for c in range(q_ref.shape[2] // chunk):
        rows = slice(c * chunk, (c + 1) * chunk)
        for b in range(batch):
            for h in range(RET_HEADS):
                lo = (h % 2) * RET_DK
                q = q_ref[h // 2, b, rows, lo:lo + RET_DK]
                k = k_ref[h // 2, b, rows, lo:lo + RET_DK]
                v = v_ref[h, b, rows, :]
                s0 = s_ref[b, h]
                scores = _dot_nt(q, k) * dmat_ref[h]
                o = _dot(scores.astype(BF16), v) + _dot(q, s0.astype(BF16)) * inner_ref[h]
                kt = (k.astype(F32) * tail_ref[h]).astype(BF16)
                s_ref[b, h] = s0 * gc_ref[h] + _dot_tn(kt, v)
                y = _group_norm(o) * _silu(rg_ref[h, b, rows, :].astype(F32))
                ro_ref[h, b, rows, :] = y.astype(ro_ref.dtype)


def _ret_tables(chunk):
    log_gamma = jnp.log1p(-jnp.exp2(-5.0 - jnp.arange(RET_HEADS, dtype=F32)))
    scale = RET_DK ** -0.5
    idx = jnp.arange(chunk, dtype=F32)
    diff = idx[:, None] - idx[None, :]
    dmat = jnp.where(diff >= 0, jnp.exp(jnp.maximum(diff, 0.0)[None] * log_gamma[:, None, None]), 0.0) * scale
    inner = jnp.exp((idx + 1.0)[None, :] * log_gamma[:, None]) * scale
    tail = jnp.exp((chunk - 1.0 - idx)[None, :] * log_gamma[:, None])
    gc = jnp.exp(chunk * log_gamma)
    inner = jnp.broadcast_to(inner[:, :, None], (RET_HEADS, chunk, RET_DV))
    tail = jnp.broadcast_to(tail[:, :, None], (RET_HEADS, chunk, RET_DK))
    gc = jnp.broadcast_to(gc[:, None, None], (RET_HEADS, 1, RET_DV))
    return dmat, inner, tail, gc


def _ret_prompt(proj4):
    _, b, t, _ = proj4.shape
    dmat, inner, tail, gc = _ret_tables(RET_CHUNK)
    c = RET_CHUNK * RET_CHUNKS_PER_STEP
    assert t % c == 0
    return pl.pallas_call(
        functools.partial(_ret_prompt_kernel, batch=b),
        grid=(t // c,),
        in_specs=[pl.BlockSpec((2, b, c, COL), lambda i: (RQ // 2, 0, i, 0)),
                  pl.BlockSpec((2, b, c, COL), lambda i: (RK // 2, 0, i, 0)),
                  pl.BlockSpec((4, b, c, COL), lambda i: (RV // 4, 0, i, 0)),
                  pl.BlockSpec((4, b, c, COL), lambda i: (RG // 4, 0, i, 0)),
                  _resident(dmat.shape), _resident(inner.shape), _resident(tail.shape),
                  _resident(gc.shape)],
        out_specs=[pl.BlockSpec((4, b, c, COL), lambda i: (0, 0, i, 0)),
                   pl.BlockSpec((b, RET_HEADS, RET_DK, RET_DV), lambda i: (0, 0, 0, 0))],
        out_shape=[jax.ShapeDtypeStruct((4, b, t, COL), BF16),
                   jax.ShapeDtypeStruct((b, RET_HEADS, RET_DK, RET_DV), F32)],
        compiler_params=_params(1),
        name="ret_prompt",
    )(proj4, proj4, proj4, proj4, dmat, inner, tail, gc)


def _attn_prompt_kernel(q_ref, kc_ref, kp_ref, vc_ref, vp_ref, bias_ref, o_ref, lse_ref):
    nres, rows_per_step, _ = q_ref.shape
    first_step = jnp.where(pl.program_id(2) == 0, 0, 1)
    lane_head = lax.broadcasted_iota(jnp.int32, (ATT_BLOCK, ATT_OUT), 1) // HEAD_DIM
    scale = jnp.asarray(HEAD_DIM ** -0.5, BF16)
    for r in range(nres):
        for j in range(rows_per_step // ATT_BLOCK):
            blk = slice(j * ATT_BLOCK, (j + 1) * ATT_BLOCK)
            q = q_ref[r, blk, :] * scale
            qs = jnp.concatenate([jnp.where(lane_head == h, q, jnp.zeros_like(q))
                                  for h in range(HEADS_PER_GROUP)], axis=0)
            if j == 0:
                k_prev, v_prev, bias = kp_ref[r], vp_ref[r], bias_ref[first_step]
            else:
                before = slice((j - 1) * ATT_BLOCK, j * ATT_BLOCK)
                k_prev, v_prev, bias = kc_ref[r, before, :], vc_ref[r, before, :], bias_ref[1]
            kcat = jnp.concatenate([k_prev, kc_ref[r, blk, :]], axis=0)
            vcat = jnp.concatenate([v_prev, vc_ref[r, blk, :]], axis=0)
            s = _dot_nt(qs, kcat) + bias
            m = jnp.max(s, axis=-1, keepdims=True)
            p = jnp.exp(s - m)
            l = jnp.sum(p, axis=-1, keepdims=True)
            on = _dot(p.astype(BF16), vcat) / l
            lse = m + jnp.log(l)
            o = jnp.zeros((ATT_BLOCK, ATT_OUT), F32)
            ls = jnp.zeros((ATT_BLOCK, ATT_OUT), F32)
            for h in range(HEADS_PER_GROUP):
                rows = slice(h * ATT_BLOCK, (h + 1) * ATT_BLOCK)
                o = jnp.where(lane_head == h, on[rows], o)
                ls = jnp.where(lane_head == h, lse[rows], ls)
            o_ref[r, blk, :] = o.astype(o_ref.dtype)
            lse_ref[r, blk, :] = ls


def _alibi_slopes():
    return jnp.exp2(-8.0 * (jnp.arange(ATT_HEADS, dtype=F32) + 1.0) / ATT_HEADS).reshape(
        N_GROUPS, HEADS_PER_GROUP)


def _attn_prompt_bias(g):
    window, dilation = ATT_GROUPS[g]
    steps = window // dilation
    qi = jnp.arange(ATT_BLOCK)[:, None]
    kj = jnp.arange(2 * ATT_BLOCK)[None, :]
    dist = qi + ATT_BLOCK - kj
    valid = (dist >= 0) & (dist <= steps)
    pen = -_alibi_slopes()[g][:, None, None] * (dist * dilation).astype(F32)[None]
    later = jnp.where(valid[None], pen, MASKED)
    first = jnp.where((valid & (kj >= ATT_BLOCK))[None], pen, MASKED)
    return jnp.stack([first, later]).reshape(2, HEADS_PER_GROUP * ATT_BLOCK, 2 * ATT_BLOCK)


def _attn_prompt(qkv, g, batch):
    _, dil, rows, _ = qkv.shape
    nres = min(dil, ATT_UNITS_PER_STEP)
    nblk = ATT_UNITS_PER_STEP // nres
    per_seq = rows // batch // ATT_BLOCK
    assert dil % nres == 0 and per_seq % nblk == 0
    steps = per_seq // nblk
    bias = _attn_prompt_bias(g)

    def cur(c):
        return pl.BlockSpec((None, nres, nblk * ATT_BLOCK, COL),
                            lambda b, r, n: (c, r, b * steps + n, 0))

    def prev(c):
        return pl.BlockSpec((None, nres, ATT_BLOCK, COL),
                            lambda b, r, n: (c, r, b * per_seq + jnp.maximum(n * nblk - 1, 0), 0))

    out_spec = pl.BlockSpec((nres, nblk * ATT_BLOCK, COL), lambda b, r, n: (r, b * steps + n, 0))
    return pl.pallas_call(
        _attn_prompt_kernel,
        grid=(batch, dil // nres, steps),
        in_specs=[cur(0), cur(1), prev(1), cur(2), prev(2), _resident(bias.shape)],
        out_specs=[out_spec, out_spec],
        out_shape=[jax.ShapeDtypeStruct((dil, rows, COL), BF16),
                   jax.ShapeDtypeStruct((dil, rows, COL), F32)],
        compiler_params=_params(3),
        name=f"attn_prompt_g{g}",
    )(qkv, qkv, qkv, qkv, qkv, bias)


def _token_order(ref, scr_ref):
    d, n, _ = ref.shape
    if d == 1:
        return ref[0].astype(F32)
    for r in range(d):
        part = ref[r].astype(F32)
        for c in range(COL // LANES):
            scr_ref[c, pl.ds(r, n, stride=d), :] = part[:, c * LANES:(c + 1) * LANES]
    return jnp.concatenate([scr_ref[c] for c in range(COL // LANES)], axis=1)


def _mix_out_kernel(*refs, n_att):
    att = refs[:n_att]
    ro_ref, gr_ref, ga_ref, x_ref, wr_ref, wa_ref, wo_ref, o_ref = refs[n_att:n_att + 8]
    scratch = refs[n_att + 8:]
    if n_att == 2:
        row_ref, col_ref = att
        ao = row_ref[...] + jnp.concatenate([col_ref[t].T for t in range(col_ref.shape[0])], axis=0)
    else:
        vals = [_token_order(r, s) for r, s in zip(att, scratch)]
        outs, lses = vals[:N_GROUPS], vals[N_GROUPS:]
        top = functools.reduce(jnp.maximum, lses)
        es = [jnp.exp(x - top) for x in lses]
        den = functools.reduce(jnp.add, es)
        ao = functools.reduce(jnp.add, [e * o for e, o in zip(es, outs)]) / den

    def token_major(ref):
        return jnp.concatenate([ref[c] for c in range(ref.shape[0])], axis=1)

    a_br = _dot(ao.astype(BF16), wa_ref[...])
    r_br = _dot(token_major(ro_ref).astype(BF16), wr_ref[...])
    merged = (jax.nn.sigmoid(token_major(gr_ref).astype(F32)) * r_br
              + jax.nn.sigmoid(token_major(ga_ref).astype(F32)) * a_br)
    o_ref[...] = x_ref[...] + _dot(merged.astype(BF16), wo_ref[...])


def _mix_out(att, ro, proj, x, wr, wa, wo, layer, bm):
    m = x.shape[0]
    if len(att) == 2:
        assert bm % LANES == 0, bm
        att_specs = [pl.BlockSpec((bm, COL), lambda i: (i, 0)),
                     pl.BlockSpec((bm // LANES, COL, LANES), lambda i: (i, 0, 0))]
        scratch = []
    else:
        att_specs = [pl.BlockSpec((a.shape[0], bm // a.shape[0], COL), lambda i: (0, i, 0))
                     for a in att]
        scratch = [pltpu.VMEM((COL // LANES, bm, LANES), F32) for a in att]
    return pl.pallas_call(
        functools.partial(_mix_out_kernel, n_att=len(att)),
        grid=(m // bm,),
        scratch_shapes=scratch,
        in_specs=att_specs + [
            pl.BlockSpec((4, bm, COL), lambda i: (0, i, 0)),
            pl.BlockSpec((4, bm, COL), lambda i: (GR // 4, i, 0)),
            pl.BlockSpec((4, bm, COL), lambda i: (GA // 4, i, 0)),
            pl.BlockSpec((bm, D_MODEL), lambda i: (i, 0)),
            _resident(wr.shape, layer), _resident(wa.shape, layer), _resident(wo.shape, layer)],
        out_specs=pl.BlockSpec((bm, D_MODEL), lambda i: (i, 0)),
        out_shape=jax.ShapeDtypeStruct((m, D_MODEL), F32),
        compiler_params=_params(1),
        name="mix_out",
    )(*att, ro, proj, proj, x, wr, wa, wo)


def _own_head():
    shape = (2 * HEADS_PER_GROUP, ATT_OUT)
    return (lax.broadcasted_iota(jnp.int32, shape, 0)
            == lax.broadcasted_iota(jnp.int32, shape, 1) // HEAD_DIM)


def _sample_scores(qkv, kv, biases, n):
    scale = HEAD_DIM ** -0.5
    stats = []
    for g in range(N_GROUPS):
        qbd = jnp.where(_own_head(), qkv[g][0, 0, pl.ds(n, 1), :], 0.0)
        s = _dot(qbd.astype(BF16), kv[g][0].astype(BF16)) * scale + biases[g][...]
        sn = jnp.sum(qbd * qkv[g][1, 0, pl.ds(n, 1), :], axis=-1, keepdims=True) * scale
        m = jnp.maximum(jnp.max(s, axis=-1, keepdims=True), sn)
        p = jnp.exp(s - m)
        pn = jnp.exp(sn - m)
        l = jnp.sum(p, axis=-1, keepdims=True) + pn
        stats.append((p, pn, l, m + jnp.log(l)))
    return stats


def _sample_values(qkv, kv, n, stats):
    def per_row(a):
        return jnp.concatenate([jnp.broadcast_to(a[h:h + 1, :], (HEAD_DIM, a.shape[1]))
                                for h in range(HEADS_PER_GROUP)], axis=0)

    lses = [s[3] for s in stats]
    top = functools.reduce(jnp.maximum, lses)
    es = [jnp.exp(x - top) for x in lses]
    den = functools.reduce(jnp.add, es)
    col = jnp.zeros((ATT_OUT, 1), F32)
    row = jnp.zeros((2 * HEADS_PER_GROUP, ATT_OUT), F32)
    for g, (p, pn, l, _) in enumerate(stats):
        w = es[g] / (den * l)
        col = col + jnp.sum(kv[g][1] * per_row(p), axis=-1, keepdims=True) * per_row(w)
        row = row + (pn * w) * qkv[g][2, 0, pl.ds(n, 1), :]
    return col, jnp.sum(jnp.where(_own_head(), row, 0.0), axis=0, keepdims=True)


def _ffn_kernel(*refs, final, hosted):
    x_ref, g_ref, wgu_ref, wd_ref, gf_ref = refs[:5]
    if hosted is None:
        o_ref, hn_ref, act_ref = refs[5:]
        fetch_before, score_after, value_after = {}, {}, {}
    else:
        layer, seqs = hosted
        qkv, caches, biases = refs[5:8], refs[8:11], refs[11:14]
        o_ref, ao_row_ref, ao_col_ref, hn_ref, act_ref = refs[14:19]
        bufs, sem = refs[19:22], refs[22]
        step, n_seqs = pl.program_id(0), pl.num_programs(0) * seqs
        n_slots = bufs[0].shape[0]

        def copies(n):
            slot = n % n_slots
            out = []
            for g in range(N_GROUPS):
                rows = min(ATT_OUT, CACHE_COPY_ELEMS // caches[g].shape[-1])
                for kv in range(2):
                    for r in range(0, ATT_OUT, rows):
                        out.append(pltpu.make_async_copy(
                            caches[g].at[layer, n, kv, pl.ds(r, rows)],
                            bufs[g].at[slot, kv, pl.ds(r, rows)], sem.at[g, slot]))
            return out

        group = CACHE_GROUP
        ahead = n_slots - group
        assert ahead > 0 and ahead % group == 0 and seqs % group == 0, (n_slots, group, seqs)

        def start_group(first_seq):
            for n in range(group):
                for c in copies(first_seq + n):
                    c.start()

        @pl.when(step == 0)
        def _():
            for first_seq in range(0, ahead, group):
                start_group(first_seq)
            ao_col_ref[...] = jnp.zeros_like(ao_col_ref)

        def fetch(p):
            n0 = step * seqs + p * group
            for n in range(group):
                for c in copies(n0 + n):
                    c.wait()

            @pl.when(n0 + ahead < n_seqs)
            def _():
                start_group(n0 + ahead)

        def window(u):
            n = step * seqs + u
            return n, [bufs[g].at[n % n_slots] for g in range(N_GROUPS)]

        starts = [-(-p * group * FF_BLOCKS // seqs) for p in range(seqs // group)]
        fetch_before = {s: p for p, s in enumerate(starts)}
        score_after, value_after = {}, {}
        for p, (s, end) in enumerate(zip(starts, starts[1:] + [FF_BLOCKS])):
            for i in range(group):
                score_after.setdefault(min(s + i, end - 1), []).append(p * group + i)
                value_after.setdefault(min(s + group + i, end - 1), []).append(p * group + i)

    stats = {}
    x = x_ref[...]
    hn_ref[...] = _rms(x, g_ref[...]).astype(BF16)
    for c in range(FF_BLOCKS):
        if c in fetch_before:
            fetch(fetch_before[c])
        gate = _dot(hn_ref[...], wgu_ref[:, c * COL:(c + 1) * COL])
        up = _dot(hn_ref[...], wgu_ref[:, D_FF + c * COL:D_FF + (c + 1) * COL])
        act_ref[:, c * COL:(c + 1) * COL] = (_silu(gate) * up).astype(BF16)
        for u in score_after.get(c, ()):
            n, kv = window(u)
            stats[u] = _sample_scores(qkv, kv, biases, n)
        for u in value_after.get(c, ()):
            n, kv = window(u)
            col, row = _sample_values(qkv, kv, n, stats.pop(u))
            ao_row_ref[pl.ds(n, 1), :] = row
            lane = lax.broadcasted_iota(jnp.int32, (ATT_OUT, LANES), 1)
            tile = ao_col_ref[n // LANES]
            ao_col_ref[n // LANES] = jnp.where(lane == n % LANES, col, tile)
    y = x + _dot(act_ref[...], wd_ref[...])
    o_ref[...] = _rms(y, gf_ref[...]) if final else y


def _sample_cache_views(caches):
    views, biases = [], []
    slopes = _alibi_slopes()
    for g, (window, dil) in enumerate(ATT_GROUPS):
        depth, n, wlen = caches[g].shape[:3]
        assert wlen == window == ATT_BLOCK * dil, (caches[g].shape, window, dil)
        views.append(caches[g].transpose(0, 1, 3, 4, 5, 2).reshape(depth, n, 2, ATT_OUT, wlen))
        back = wlen - jnp.arange(wlen)
        pen = -slopes[g][:, None] * back.astype(F32)[None, :]
        pen = jnp.where((back % dil == 0)[None, :], pen, MASKED)
        biases.append(jnp.pad(pen, ((0, HEADS_PER_GROUP), (0, 0))))
    return views, biases


def _ffn(x, g, wgu, wd, layer, gf, bm, final, hosted=None):
    m = x.shape[0]
    steps = m // bm
    in_specs = [pl.BlockSpec((bm, D_MODEL), lambda i: (i, 0)),
                _resident((1, D_MODEL)),
                _resident(wgu.shape, layer), _resident(wd.shape, layer),
                _resident((1, D_MODEL))]
    args = [x, g.reshape(1, D_MODEL), wgu, wd, gf.reshape(1, D_MODEL)]
    out_specs = [pl.BlockSpec((bm, D_MODEL), lambda i: (i, 0))]
    out_shape = [jax.ShapeDtypeStruct((m, D_MODEL), F32)]
    scratch = [pltpu.VMEM((bm, D_MODEL), BF16), pltpu.VMEM((bm, D_FF), BF16)]
    kernel_hosted = None
    if hosted is not None:
        qkv, views, biases = hosted
        n = qkv[0].shape[2]
        assert n % steps == 0 and n // steps <= FF_BLOCKS, (n, steps)
        kernel_hosted = (layer, n // steps)
        in_specs += ([_resident(a.shape) for a in qkv]
                     + [pl.BlockSpec(memory_space=pl.ANY) for _ in views]
                     + [_resident(b.shape) for b in biases])
        args += [*qkv, *views, *biases]
        assert n % LANES == 0, n
        out_specs += [pl.BlockSpec((n, ATT_OUT), lambda i: (0, 0)),
                      pl.BlockSpec((n // LANES, ATT_OUT, LANES), lambda i: (0, 0, 0))]
        out_shape += [jax.ShapeDtypeStruct((n, ATT_OUT), F32),
                      jax.ShapeDtypeStruct((n // LANES, ATT_OUT, LANES), F32)]
        scratch += [pltpu.VMEM((CACHE_SLOTS,) + v.shape[2:], F32) for v in views]
        scratch.append(pltpu.SemaphoreType.DMA((N_GROUPS, CACHE_SLOTS)))
    out = pl.pallas_call(
        functools.partial(_ffn_kernel, final=final, hosted=kernel_hosted),
        grid=(steps,),
        in_specs=in_specs,
        out_specs=out_specs,
        out_shape=out_shape,
        scratch_shapes=scratch,
        compiler_params=_params(1),
        name="ffn",
    )(*args)
    return out if hosted is not None else out[0]


def _prep_weights(w_in, w_ret_branch, w_att_branch, w_out, w_gate_up, w_down):
    return tuple(w.astype(BF16) for w in (w_in, w_ret_branch, w_att_branch, w_out, w_gate_up, w_down))


def _new_kv_rows(qkv):
    n = qkv.shape[2]
    return jnp.stack([qkv[1, 0], qkv[2, 0]], axis=1).reshape(n, 1, 2, HEADS_PER_GROUP, HEAD_DIM)


def _block_rows(m, target):
    bm = min(m, target)
    assert m % bm == 0, (m, bm)
    return bm


def kernel(x_prompt, x_sample, state_ret, cache_kv_w128, cache_kv_w512, cache_kv_w2048, norm_mix,
           w_in, w_ret_branch, w_att_branch, w_out, norm_ffn, w_gate_up, w_down, norm_final):
    depth = w_in.shape[0]
    batch, seq, _ = x_prompt.shape
    n_dec, dec_seq, _ = x_sample.shape
    assert dec_seq == 1 and seq % (ATT_BLOCK * ATT_GROUPS[-1][1]) == 0 and seq % RET_CHUNK == 0
    wi, wr, wa, wo, wgu, wd = _prep_weights(w_in, w_ret_branch, w_att_branch, w_out,
                                            w_gate_up, w_down)
    cache_views, cache_biases = _sample_cache_views((cache_kv_w128, cache_kv_w512, cache_kv_w2048))

    m = batch * seq
    bm = _block_rows(m, 512)
    bs = _block_rows(n_dec, 512)
    xp = x_prompt.reshape(m, D_MODEL)
    xs = x_sample.reshape(n_dec, D_MODEL)
    p_ret, p_kv = [], [[] for _ in ATT_GROUPS]
    s_ret, s_kv = None, [[] for _ in ATT_GROUPS]
    for l in range(depth):
        last = l == depth - 1
        proj_s, *qkv_s = _in_proj(xs, norm_mix[l], wi, l, bs, F32, (1,) * N_GROUPS)
        for g in range(N_GROUPS):
            s_kv[g].append(_new_kv_rows(qkv_s[g]))
        keeps = tuple(min(window, seq) for window, _ in ATT_GROUPS)
        proj, *rest = _in_proj(xp, norm_mix[l], wi, l, bm, BF16, DILATIONS, tails=(batch, keeps),
                               hosted=(proj_s, state_ret, s_ret))
        qkv, kv_tails, (ro_s, s_ret) = rest[:N_GROUPS], rest[N_GROUPS:2 * N_GROUPS], rest[2 * N_GROUPS:]
        ro, s_fin = _ret_prompt(proj.reshape(N_MAIN_BLOCKS, batch, seq, COL))
        p_ret.append(s_fin)
        outs, lses = [], []
        for g in range(N_GROUPS):
            o, lse = _attn_prompt(qkv[g], g, batch)
            outs.append(o), lses.append(lse)
            p_kv[g].append(kv_tails[g].reshape(batch, 2, HEADS_PER_GROUP, HEAD_DIM, keeps[g])
                           .transpose(0, 4, 1, 2, 3))
        xp = _mix_out(outs + lses, ro.reshape(4, m, COL), proj, xp, wr, wa, wo, l,
                      _block_rows(m, 1024))
        xp, *ao_s = _ffn(xp, norm_ffn[l], wgu, wd, l, norm_final, bm, last,
                         hosted=(qkv_s, cache_views, cache_biases))
        xs = _mix_out(ao_s, ro_s, proj_s, xs, wr, wa, wo, l, bs)
        xs = _ffn(xs, norm_ffn[l], wgu, wd, l, norm_final, bs, last)
    y_prompt = xp.reshape(batch, seq, D_MODEL)
    y_sample = xs.reshape(n_dec, 1, D_MODEL)

    return (y_prompt, y_sample, jnp.stack(p_ret),
            jnp.stack(p_kv[0]), jnp.stack(p_kv[1]), jnp.stack(p_kv[2]),
            s_ret, jnp.stack(s_kv[0]), jnp.stack(s_kv[1]), jnp.stack(s_kv[2]))
```

```python
import functools

import jax
import jax.numpy as jnp
from jax import lax
from jax.experimental import pallas as pl
from jax.experimental.pallas import tpu as pltpu

F32 = jnp.float32
BF16 = jnp.bfloat16

D_MODEL = 1024
RET_HEADS = 4
RET_DK = 128
RET_DV = 256
RET_CHUNK = 128
ATT_GROUPS = ((128, 1), (512, 4), (2048, 16))
N_GROUPS = 3
HEADS_PER_GROUP = 4
ATT_HEADS = N_GROUPS * HEADS_PER_GROUP
HEAD_DIM = 64
ATT_BLOCK = 128
ATT_OUT = HEADS_PER_GROUP * HEAD_DIM
D_FF = 2816
D_IN = 7424
RMS_EPS = 1e-6
GN_EPS = 1e-5

LANES = 128
COL = 256
N_COL_BLOCKS = D_IN // COL
FF_BLOCKS = D_FF // COL
RQ, RK, RV, RG, GR, GA = 0, 2, 4, 8, 12, 16
N_MAIN_BLOCKS = 20
ATT_SRC_BLOCK = 12
MAIN_SRC_BLOCKS = tuple(range(12)) + tuple(range(21, 29))
DILATIONS = tuple(d for _, d in ATT_GROUPS)
ATT_UNITS_PER_STEP = 8
RET_CHUNKS_PER_STEP = 4
CACHE_SLOTS = 3
CACHE_GROUP = 1
CACHE_COPY_ELEMS = HEAD_DIM * 1024
MASKED = -1e30
VMEM_LIMIT = 56 * 1024 * 1024


def _params(n_axes):
    return pltpu.CompilerParams(dimension_semantics=("arbitrary",) * n_axes,
                                vmem_limit_bytes=VMEM_LIMIT)


def _dot(a, b):
    return jnp.dot(a, b, preferred_element_type=F32)


def _dot_nt(a, b):
    return lax.dot_general(a, b, (((1,), (1,)), ((), ())), preferred_element_type=F32)


def _dot_tn(a, b):
    return lax.dot_general(a, b, (((0,), (0,)), ((), ())), preferred_element_type=F32)


def _rms(x, g):
    return x * lax.rsqrt(jnp.mean(x * x, axis=-1, keepdims=True) + RMS_EPS) * g


def _silu(x):
    return x * jax.nn.sigmoid(x)


def _group_norm(o):
    mu = jnp.mean(o, axis=-1, keepdims=True)
    d = o - mu
    var = jnp.mean(d * d, axis=-1, keepdims=True)
    return d * lax.rsqrt(var + GN_EPS)


def _resident(shape, layer=None):
    if layer is None:
        zeros = (0,) * len(shape)
        return pl.BlockSpec(shape, lambda *_: zeros, pipeline_mode=pl.Buffered(1))
    index = (layer,) + (0,) * (len(shape) - 1)
    return pl.BlockSpec((None,) + tuple(shape[1:]), lambda *_: index, pipeline_mode=pl.Buffered(1))


def _retention_host(refs, layer, seqs):
    ps_ref, gam_ref, state_hbm, new_hbm, ro_ref, sin_ref, sout_ref, sem_in, sem_out = refs
    step, n_steps = pl.program_id(0), pl.num_programs(0)

    def loads(at_step):
        half = (at_step % 2) * seqs
        return [pltpu.make_async_copy(state_hbm.at[layer, at_step * seqs + u, h],
                                      sin_ref.at[half + u, h], sem_in.at[half + u])
                for u in range(seqs) for h in range(RET_HEADS)]

    def stores(at_step):
        return [pltpu.make_async_copy(sout_ref.at[u, h], new_hbm.at[layer, at_step * seqs + u, h],
                                      sem_out.at[u])
                for u in range(seqs) for h in range(RET_HEADS)]

    def columns(block, h):
        lo = (h % 2) * RET_DK
        rows = [ps_ref[block + h // 2, pl.ds(step * seqs + u, 1), :][:, lo:lo + RET_DK]
                for u in range(seqs)]
        return jnp.concatenate(rows + [jnp.zeros((8 - seqs, RET_DK), F32)], axis=0).T

    def run():
        @pl.when(step == 0)
        def _():
            for c in loads(0):
                c.start()

        for c in loads(step):
            c.wait()

        @pl.when(step + 1 < n_steps)
        def _():
            for c in loads(step + 1):
                c.start()

        @pl.when(step > 0)
        def _():
            for c in stores(step - 1):
                c.wait()

        half = (step % 2) * seqs
        for h in range(RET_HEADS):
            q_cols, k_cols = columns(RQ, h), columns(RK, h)
            for u in range(seqs):
                n = step * seqs + u
                s1 = (sin_ref[half + u, h] * gam_ref[h]
                      + k_cols[:, u:u + 1] * ps_ref[RV + h, pl.ds(n, 1), :])
                sout_ref[u, h] = s1
                ro_ref[h, pl.ds(n, 1), :] = jnp.sum(s1 * q_cols[:, u:u + 1], axis=0, keepdims=True)
        for c in stores(step):
            c.start()

    def finish():
        @pl.when(step == n_steps - 1)
        def _():
            for c in stores(step):
                c.wait()
            for h in range(RET_HEADS):
                o = ro_ref[h] * (RET_DK ** -0.5)
                ro_ref[h] = _group_norm(o) * _silu(ps_ref[RG + h])

    return run, finish


def _inproj_kernel(*refs, dils, tails, hosted):
    x_ref, g_ref, w_ref = refs[:3]
    refs = refs[3:]
    if hosted is not None:
        layer, seqs, aliased = hosted
        ps_ref, gam_ref, state_hbm = refs[:3]
        refs = refs[4:] if aliased else refs[3:]
    main_ref, a0_ref, a1_ref, a2_ref = refs[:4]
    n_tails = 0 if tails is None else len(tails[1])
    tail_refs = refs[4:4 + n_tails]
    refs = refs[4 + n_tails:]
    finish = None
    if hosted is not None:
        ro_ref, new_hbm = refs[:2]
        xn_ref, hn_ref, hd_ref, sin_ref, sout_ref, sem_in, sem_out = refs[2:]
        run, finish = _retention_host((ps_ref, gam_ref, state_hbm, new_hbm, ro_ref, sin_ref,
                                       sout_ref, sem_in, sem_out), layer, seqs)
        run()
    else:
        xn_ref, hn_ref, hd_ref = refs
    bm = x_ref.shape[0]
    xn = _rms(x_ref[...], g_ref[...])
    hn_ref[...] = xn.astype(BF16)
    dilated = [d for d in dils if d > 1]
    if dilated:
        for c in range(D_MODEL // LANES):
            xn_ref[c] = xn[:, c * LANES:(c + 1) * LANES]
    for slab, d in enumerate(dilated):
        n = bm // d
        for r in range(d):
            for c in range(D_MODEL // LANES):
                hd_ref[slab, r * n:(r + 1) * n, c * LANES:(c + 1) * LANES] = (
                    xn_ref[c, pl.ds(r, n, stride=d), :].astype(BF16))

    def w_block(j):
        return w_ref[:, j * COL:(j + 1) * COL]

    for g, (a_ref, d) in enumerate(zip((a0_ref, a1_ref, a2_ref), dils)):
        lhs = hn_ref[...] if d == 1 else hd_ref[dilated.index(d)]
        for c in range(3):
            out = _dot(lhs, w_block(ATT_SRC_BLOCK + c * N_GROUPS + g))
            a_ref[c] = out.reshape(d, bm // d, COL).astype(a_ref.dtype)
    for j in range(N_MAIN_BLOCKS):
        main_ref[j] = _dot(hn_ref[...], w_block(MAIN_SRC_BLOCKS[j])).astype(main_ref.dtype)
    if tails is not None:
        per_seq, keeps = tails
        in_seq = pl.program_id(0) % per_seq
        for g, (t_ref, keep) in enumerate(zip(tail_refs, keeps)):
            rows = t_ref.shape[-1]

            @pl.when(in_seq >= per_seq - keep // rows)
            def _():
                for c in (1, 2):
                    out = _dot(hn_ref[bm - rows:, :], w_block(ATT_SRC_BLOCK + c * N_GROUPS + g))
                    t_ref[c - 1] = out.T
    if finish is not None:
        finish()


def _in_proj(x, g, w, layer, bm, out_dtype, dils, tails=None, hosted=None):
    m = x.shape[0]
    steps = m // bm
    in_specs = [pl.BlockSpec((bm, D_MODEL), lambda i: (i, 0)),
                _resident((1, D_MODEL)),
                _resident(w.shape, layer)]
    args = [x, g.reshape(1, D_MODEL), w]
    host_specs, host_shapes, host_scratch, aliases, kernel_hosted = [], [], [], {}, None
    if hosted is not None:
        proj_s, state, new_state = hosted
        n = proj_s.shape[1]
        seqs = n // steps
        assert n == seqs * steps and 2 <= n and seqs <= 8, (n, steps)
        gam = jnp.exp(jnp.log1p(-jnp.exp2(-5.0 - jnp.arange(RET_HEADS, dtype=F32))))
        gam = jnp.broadcast_to(gam[:, None, None], (RET_HEADS, 1, RET_DV))
        in_specs += [pl.BlockSpec((RG + 4, n, COL), lambda i: (0, 0, 0), pipeline_mode=pl.Buffered(1)),
                     _resident(gam.shape), pl.BlockSpec(memory_space=pl.ANY)]
        args += [proj_s, gam, state]
        if new_state is not None:
            aliases = {len(args): 4 + (0 if tails is None else len(tails[1])) + 1}
            in_specs.append(pl.BlockSpec(memory_space=pl.ANY))
            args.append(new_state)
        kernel_hosted = (layer, seqs, new_state is not None)
        host_specs = [pl.BlockSpec((RET_HEADS, n, COL), lambda i: (0, 0, 0)),
                      pl.BlockSpec(memory_space=pl.ANY)]
        host_shapes = [jax.ShapeDtypeStruct((RET_HEADS, n, COL), F32),
                       jax.ShapeDtypeStruct(state.shape, F32)]
        host_scratch = [pltpu.VMEM((2 * seqs,) + state.shape[2:], F32),
                        pltpu.VMEM((seqs,) + state.shape[2:], F32),
                        pltpu.SemaphoreType.DMA((2 * seqs,)), pltpu.SemaphoreType.DMA((seqs,))]
    att_specs = [pl.BlockSpec((3, d, bm // d, COL), lambda i: (0, 0, i, 0)) for d in dils]
    att_shapes = [jax.ShapeDtypeStruct((3, d, m // d, COL), out_dtype) for d in dils]
    kernel_tails = None
    if tails is not None:
        batch, keeps = tails
        per_seq = m // batch // bm
        kernel_tails = (per_seq, keeps)
        for keep in keeps:
            rows = min(keep, bm)
            n_tail = keep // rows
            assert keep % rows == 0 and n_tail <= per_seq, (keep, bm, per_seq)
            att_specs.append(pl.BlockSpec(
                (None, 2, ATT_OUT, rows),
                lambda i, n_tail=n_tail: (i // per_seq, 0, 0,
                                          jnp.maximum(i % per_seq - (per_seq - n_tail), 0))))
            att_shapes.append(jax.ShapeDtypeStruct((batch, 2, ATT_OUT, keep), F32))
    return pl.pallas_call(
        functools.partial(_inproj_kernel, dils=dils, tails=kernel_tails, hosted=kernel_hosted),
        grid=(steps,),
        in_specs=in_specs,
        out_specs=([pl.BlockSpec((N_MAIN_BLOCKS, bm, COL), lambda i: (0, i, 0))] + att_specs
                   + host_specs),
        out_shape=([jax.ShapeDtypeStruct((N_MAIN_BLOCKS, m, COL), out_dtype)] + att_shapes
                   + host_shapes),
        scratch_shapes=[pltpu.VMEM((D_MODEL // LANES, bm, LANES), F32),
                        pltpu.VMEM((bm, D_MODEL), BF16),
                        pltpu.VMEM((max(1, sum(d > 1 for d in dils)), bm, D_MODEL), BF16)]
        + host_scratch,
        input_output_aliases=aliases,
        compiler_params=_params(1),
        name="in_proj",
    )(*args)


def _ret_prompt_kernel(q_ref, k_ref, v_ref, rg_ref, dmat_ref, inner_ref, tail_ref, gc_ref,
                       ro_ref, s_ref, *, batch):
    @pl.when(pl.program_id(0) == 0)
    def _():
        s_ref[...] = jnp.zeros_like(s_ref)

    chunk = dmat_ref.shape[1]
    for c in range(q_ref.shape[2] // chunk):
        rows = slice(c * chunk, (c + 1) * chunk)
        for b in range(batch):
            for h in range(RET_HEADS):
                lo = (h % 2) * RET_DK
                q = q_ref[h // 2, b, rows, lo:lo + RET_DK]
                k = k_ref[h // 2, b, rows, lo:lo + RET_DK]
                v = v_ref[h, b, rows, :]
                s0 = s_ref[b, h]
                scores = _dot_nt(q, k) * dmat_ref[h]
                o = _dot(scores.astype(BF16), v) + _dot(q, s0.astype(BF16)) * inner_ref[h]
                kt = (k.astype(F32) * tail_ref[h]).astype(BF16)
                s_ref[b, h] = s0 * gc_ref[h] + _dot_tn(kt, v)
                y = _group_norm(o) * _silu(rg_ref[h, b, rows, :].astype(F32))
                ro_ref[h, b, rows, :] = y.astype(ro_ref.dtype)


def _ret_tables(chunk):
    log_gamma = jnp.log1p(-jnp.exp2(-5.0 - jnp.arange(RET_HEADS, dtype=F32)))
    scale = RET_DK ** -0.5
    idx = jnp.arange(chunk, dtype=F32)
    diff = idx[:, None] - idx[None, :]
    dmat = jnp.where(diff >= 0, jnp.exp(jnp.maximum(diff, 0.0)[None] * log_gamma[:, None, None]), 0.0) * scale
    inner = jnp.exp((idx + 1.0)[None, :] * log_gamma[:, None]) * scale
    tail = jnp.exp((chunk - 1.0 - idx)[None, :] * log_gamma[:, None])
    gc = jnp.exp(chunk * log_gamma)
    inner = jnp.broadcast_to(inner[:, :, None], (RET_HEADS, chunk, RET_DV))
    tail = jnp.broadcast_to(tail[:, :, None], (RET_HEADS, chunk, RET_DK))
    gc = jnp.broadcast_to(gc[:, None, None], (RET_HEADS, 1, RET_DV))
    return dmat, inner, tail, gc


def _ret_prompt(proj4):
    _, b, t, _ = proj4.shape
    dmat, inner, tail, gc = _ret_tables(RET_CHUNK)
    c = RET_CHUNK * RET_CHUNKS_PER_STEP
    assert t % c == 0
    return pl.pallas_call(
        functools.partial(_ret_prompt_kernel, batch=b),
        grid=(t // c,),
        in_specs=[pl.BlockSpec((2, b, c, COL), lambda i: (RQ // 2, 0, i, 0)),
                  pl.BlockSpec((2, b, c, COL), lambda i: (RK // 2, 0, i, 0)),
                  pl.BlockSpec((4, b, c, COL), lambda i: (RV // 4, 0, i, 0)),
                  pl.BlockSpec((4, b, c, COL), lambda i: (RG // 4, 0, i, 0)),
                  _resident(dmat.shape), _resident(inner.shape), _resident(tail.shape),
                  _resident(gc.shape)],
        out_specs=[pl.BlockSpec((4, b, c, COL), lambda i: (0, 0, i, 0)),
                   pl.BlockSpec((b, RET_HEADS, RET_DK, RET_DV), lambda i: (0, 0, 0, 0))],
        out_shape=[jax.ShapeDtypeStruct((4, b, t, COL), BF16),
                   jax.ShapeDtypeStruct((b, RET_HEADS, RET_DK, RET_DV), F32)],
        compiler_params=_params(1),
        name="ret_prompt",
    )(proj4, proj4, proj4, proj4, dmat, inner, tail, gc)


def _attn_prompt_kernel(q_ref, kc_ref, kp_ref, vc_ref, vp_ref, bias_ref, o_ref, lse_ref):
    nres, rows_per_step, _ = q_ref.shape
    first_step = jnp.where(pl.program_id(2) == 0, 0, 1)
    lane_head = lax.broadcasted_iota(jnp.int32, (ATT_BLOCK, ATT_OUT), 1) // HEAD_DIM
    scale = jnp.asarray(HEAD_DIM ** -0.5, BF16)
    for r in range(nres):
        for j in range(rows_per_step // ATT_BLOCK):
            blk = slice(j * ATT_BLOCK, (j + 1) * ATT_BLOCK)
            q = q_ref[r, blk, :] * scale
            qs = jnp.concatenate([jnp.where(lane_head == h, q, jnp.zeros_like(q))
                                  for h in range(HEADS_PER_GROUP)], axis=0)
            if j == 0:
                k_prev, v_prev, bias = kp_ref[r], vp_ref[r], bias_ref[first_step]
            else:
                before = slice((j - 1) * ATT_BLOCK, j * ATT_BLOCK)
                k_prev, v_prev, bias = kc_ref[r, before, :], vc_ref[r, before, :], bias_ref[1]
            kcat = jnp.concatenate([k_prev, kc_ref[r, blk, :]], axis=0)
            vcat = jnp.concatenate([v_prev, vc_ref[r, blk, :]], axis=0)
            s = _dot_nt(qs, kcat) + bias
            m = jnp.max(s, axis=-1, keepdims=True)
            p = jnp.exp(s - m)
            l = jnp.sum(p, axis=-1, keepdims=True)
            on = _dot(p.astype(BF16), vcat) / l
            lse = m + jnp.log(l)
            o = jnp.zeros((ATT_BLOCK, ATT_OUT), F32)
            ls = jnp.zeros((ATT_BLOCK, ATT_OUT), F32)
            for h in range(HEADS_PER_GROUP):
                rows = slice(h * ATT_BLOCK, (h + 1) * ATT_BLOCK)
                o = jnp.where(lane_head == h, on[rows], o)
                ls = jnp.where(lane_head == h, lse[rows], ls)
            o_ref[r, blk, :] = o.astype(o_ref.dtype)
            lse_ref[r, blk, :] = ls


def _alibi_slopes():
    return jnp.exp2(-8.0 * (jnp.arange(ATT_HEADS, dtype=F32) + 1.0) / ATT_HEADS).reshape(
        N_GROUPS, HEADS_PER_GROUP)


def _attn_prompt_bias(g):
    window, dilation = ATT_GROUPS[g]
    steps = window // dilation
    qi = jnp.arange(ATT_BLOCK)[:, None]
    kj = jnp.arange(2 * ATT_BLOCK)[None, :]
    dist = qi + ATT_BLOCK - kj
    valid = (dist >= 0) & (dist <= steps)
    pen = -_alibi_slopes()[g][:, None, None] * (dist * dilation).astype(F32)[None]
    later = jnp.where(valid[None], pen, MASKED)
    first = jnp.where((valid & (kj >= ATT_BLOCK))[None], pen, MASKED)
    return jnp.stack([first, later]).reshape(2, HEADS_PER_GROUP * ATT_BLOCK, 2 * ATT_BLOCK)


def _attn_prompt(qkv, g, batch):
    _, dil, rows, _ = qkv.shape
    nres = min(dil, ATT_UNITS_PER_STEP)
    nblk = ATT_UNITS_PER_STEP // nres
    per_seq = rows // batch // ATT_BLOCK
    assert dil % nres == 0 and per_seq % nblk == 0
    steps = per_seq // nblk
    bias = _attn_prompt_bias(g)

    def cur(c):
        return pl.BlockSpec((None, nres, nblk * ATT_BLOCK, COL),
                            lambda b, r, n: (c, r, b * steps + n, 0))

    def prev(c):
        return pl.BlockSpec((None, nres, ATT_BLOCK, COL),
                            lambda b, r, n: (c, r, b * per_seq + jnp.maximum(n * nblk - 1, 0), 0))

    out_spec = pl.BlockSpec((nres, nblk * ATT_BLOCK, COL), lambda b, r, n: (r, b * steps + n, 0))
    return pl.pallas_call(
        _attn_prompt_kernel,
        grid=(batch, dil // nres, steps),
        in_specs=[cur(0), cur(1), prev(1), cur(2), prev(2), _resident(bias.shape)],
        out_specs=[out_spec, out_spec],
        out_shape=[jax.ShapeDtypeStruct((dil, rows, COL), BF16),
                   jax.ShapeDtypeStruct((dil, rows, COL), F32)],
        compiler_params=_params(3),
        name=f"attn_prompt_g{g}",
    )(qkv, qkv, qkv, qkv, qkv, bias)


def _token_order(ref, scr_ref):
    d, n, _ = ref.shape
    if d == 1:
        return ref[0].astype(F32)
    for r in range(d):
        part = ref[r].astype(F32)
        for c in range(COL // LANES):
            scr_ref[c, pl.ds(r, n, stride=d), :] = part[:, c * LANES:(c + 1) * LANES]
    return jnp.concatenate([scr_ref[c] for c in range(COL // LANES)], axis=1)


def _mix_out_kernel(*refs, n_att):
    att = refs[:n_att]
    ro_ref, gr_ref, ga_ref, x_ref, wr_ref, wa_ref, wo_ref, o_ref = refs[n_att:n_att + 8]
    scratch = refs[n_att + 8:]
    if n_att == 2:
        row_ref, col_ref = att
        ao = row_ref[...] + jnp.concatenate([col_ref[t].T for t in range(col_ref.shape[0])], axis=0)
    else:
        vals = [_token_order(r, s) for r, s in zip(att, scratch)]
        outs, lses = vals[:N_GROUPS], vals[N_GROUPS:]
        top = functools.reduce(jnp.maximum, lses)
        es = [jnp.exp(x - top) for x in lses]
        den = functools.reduce(jnp.add, es)
        ao = functools.reduce(jnp.add, [e * o for e, o in zip(es, outs)]) / den

    def token_major(ref):
        return jnp.concatenate([ref[c] for c in range(ref.shape[0])], axis=1)

    a_br = _dot(ao.astype(BF16), wa_ref[...])
    r_br = _dot(token_major(ro_ref).astype(BF16), wr_ref[...])
    merged = (jax.nn.sigmoid(token_major(gr_ref).astype(F32)) * r_br
              + jax.nn.sigmoid(token_major(ga_ref).astype(F32)) * a_br)
    o_ref[...] = x_ref[...] + _dot(merged.astype(BF16), wo_ref[...])


def _mix_out(att, ro, proj, x, wr, wa, wo, layer, bm):
    m = x.shape[0]
    if len(att) == 2:
        assert bm % LANES == 0, bm
        att_specs = [pl.BlockSpec((bm, COL), lambda i: (i, 0)),
                     pl.BlockSpec((bm // LANES, COL, LANES), lambda i: (i, 0, 0))]
        scratch = []
    else:
        att_specs = [pl.BlockSpec((a.shape[0], bm // a.shape[0], COL), lambda i: (0, i, 0))
                     for a in att]
        scratch = [pltpu.VMEM((COL // LANES, bm, LANES), F32) for a in att]
    return pl.pallas_call(
        functools.partial(_mix_out_kernel, n_att=len(att)),
        grid=(m // bm,),
        scratch_shapes=scratch,
        in_specs=att_specs + [
            pl.BlockSpec((4, bm, COL), lambda i: (0, i, 0)),
            pl.BlockSpec((4, bm, COL), lambda i: (GR // 4, i, 0)),
            pl.BlockSpec((4, bm, COL), lambda i: (GA // 4, i, 0)),
            pl.BlockSpec((bm, D_MODEL), lambda i: (i, 0)),
            _resident(wr.shape, layer), _resident(wa.shape, layer), _resident(wo.shape, layer)],
        out_specs=pl.BlockSpec((bm, D_MODEL), lambda i: (i, 0)),
        out_shape=jax.ShapeDtypeStruct((m, D_MODEL), F32),
        compiler_params=_params(1),
        name="mix_out",
    )(*att, ro, proj, proj, x, wr, wa, wo)


def _own_head():
    shape = (2 * HEADS_PER_GROUP, ATT_OUT)
    return (lax.broadcasted_iota(jnp.int32, shape, 0)
            == lax.broadcasted_iota(jnp.int32, shape, 1) // HEAD_DIM)


def _sample_scores(qkv, kv, biases, n):
    scale = HEAD_DIM ** -0.5
    stats = []
    for g in range(N_GROUPS):
        qbd = jnp.where(_own_head(), qkv[g][0, 0, pl.ds(n, 1), :], 0.0)
        s = _dot(qbd.astype(BF16), kv[g][0].astype(BF16)) * scale + biases[g][...]
        sn = jnp.sum(qbd * qkv[g][1, 0, pl.ds(n, 1), :], axis=-1, keepdims=True) * scale
        m = jnp.maximum(jnp.max(s, axis=-1, keepdims=True), sn)
        p = jnp.exp(s - m)
        pn = jnp.exp(sn - m)
        l = jnp.sum(p, axis=-1, keepdims=True) + pn
        stats.append((p, pn, l, m + jnp.log(l)))
    return stats


def _sample_values(qkv, kv, n, stats):
    def per_row(a):
        return jnp.concatenate([jnp.broadcast_to(a[h:h + 1, :], (HEAD_DIM, a.shape[1]))
                                for h in range(HEADS_PER_GROUP)], axis=0)

    lses = [s[3] for s in stats]
    top = functools.reduce(jnp.maximum, lses)
    es = [jnp.exp(x - top) for x in lses]
    den = functools.reduce(jnp.add, es)
    col = jnp.zeros((ATT_OUT, 1), F32)
    row = jnp.zeros((2 * HEADS_PER_GROUP, ATT_OUT), F32)
    for g, (p, pn, l, _) in enumerate(stats):
        w = es[g] / (den * l)
        col = col + jnp.sum(kv[g][1] * per_row(p), axis=-1, keepdims=True) * per_row(w)
        row = row + (pn * w) * qkv[g][2, 0, pl.ds(n, 1), :]
    return col, jnp.sum(jnp.where(_own_head(), row, 0.0), axis=0, keepdims=True)


def _ffn_kernel(*refs, final, hosted):
    x_ref, g_ref, wgu_ref, wd_ref, gf_ref = refs[:5]
    if hosted is None:
        o_ref, hn_ref, act_ref = refs[5:]
        fetch_before, score_after, value_after = {}, {}, {}
    else:
        layer, seqs = hosted
        qkv, caches, biases = refs[5:8], refs[8:11], refs[11:14]
        o_ref, ao_row_ref, ao_col_ref, hn_ref, act_ref = refs[14:19]
        bufs, sem = refs[19:22], refs[22]
        step, n_seqs = pl.program_id(0), pl.num_programs(0) * seqs
        n_slots = bufs[0].shape[0]

        def copies(n):
            slot = n % n_slots
            out = []
            for g in range(N_GROUPS):
                rows = min(ATT_OUT, CACHE_COPY_ELEMS // caches[g].shape[-1])
                for kv in range(2):
                    for r in range(0, ATT_OUT, rows):
                        out.append(pltpu.make_async_copy(
                            caches[g].at[layer, n, kv, pl.ds(r, rows)],
                            bufs[g].at[slot, kv, pl.ds(r, rows)], sem.at[g, slot]))
            return out

        group = CACHE_GROUP
        ahead = n_slots - group
        assert ahead > 0 and ahead % group == 0 and seqs % group == 0, (n_slots, group, seqs)

        def start_group(first_seq):
            for n in range(group):
                for c in copies(first_seq + n):
                    c.start()

        @pl.when(step == 0)
        def _():
            for first_seq in range(0, ahead, group):
                start_group(first_seq)
            ao_col_ref[...] = jnp.zeros_like(ao_col_ref)

        def fetch(p):
            n0 = step * seqs + p * group
            for n in range(group):
                for c in copies(n0 + n):
                    c.wait()

            @pl.when(n0 + ahead < n_seqs)
            def _():
                start_group(n0 + ahead)

        def window(u):
            n = step * seqs + u
            return n, [bufs[g].at[n % n_slots] for g in range(N_GROUPS)]

        starts = [-(-p * group * FF_BLOCKS // seqs) for p in range(seqs // group)]
        fetch_before = {s: p for p, s in enumerate(starts)}
        score_after, value_after = {}, {}
        for p, (s, end) in enumerate(zip(starts, starts[1:] + [FF_BLOCKS])):
            for i in range(group):
                score_after.setdefault(min(s + i, end - 1), []).append(p * group + i)
                value_after.setdefault(min(s + group + i, end - 1), []).append(p * group + i)

    stats = {}
    x = x_ref[...]
    hn_ref[...] = _rms(x, g_ref[...]).astype(BF16)
    for c in range(FF_BLOCKS):
        if c in fetch_before:
            fetch(fetch_before[c])
        gate = _dot(hn_ref[...], wgu_ref[:, c * COL:(c + 1) * COL])
        up = _dot(hn_ref[...], wgu_ref[:, D_FF + c * COL:D_FF + (c + 1) * COL])
        act_ref[:, c * COL:(c + 1) * COL] = (_silu(gate) * up).astype(BF16)
        for u in score_after.get(c, ()):
            n, kv = window(u)
            stats[u] = _sample_scores(qkv, kv, biases, n)
        for u in value_after.get(c, ()):
            n, kv = window(u)
            col, row = _sample_values(qkv, kv, n, stats.pop(u))
            ao_row_ref[pl.ds(n, 1), :] = row
            lane = lax.broadcasted_iota(jnp.int32, (ATT_OUT, LANES), 1)
            tile = ao_col_ref[n // LANES]
            ao_col_ref[n // LANES] = jnp.where(lane == n % LANES, col, tile)
    y = x + _dot(act_ref[...], wd_ref[...])
    o_ref[...] = _rms(y, gf_ref[...]) if final else y


def _sample_cache_views(caches):
    views, biases = [], []
    slopes = _alibi_slopes()
    for g, (window, dil) in enumerate(ATT_GROUPS):
        depth, n, wlen = caches[g].shape[:3]
        assert wlen == window == ATT_BLOCK * dil, (caches[g].shape, window, dil)
        views.append(caches[g].transpose(0, 1, 3, 4, 5, 2).reshape(depth, n, 2, ATT_OUT, wlen))
        back = wlen - jnp.arange(wlen)
        pen = -slopes[g][:, None] * back.astype(F32)[None, :]
        pen = jnp.where((back % dil == 0)[None, :], pen, MASKED)
        biases.append(jnp.pad(pen, ((0, HEADS_PER_GROUP), (0, 0))))
    return views, biases


def _ffn(x, g, wgu, wd, layer, gf, bm, final, hosted=None):
    m = x.shape[0]
    steps = m // bm
    in_specs = [pl.BlockSpec((bm, D_MODEL), lambda i: (i, 0)),
                _resident((1, D_MODEL)),
                _resident(wgu.shape, layer), _resident(wd.shape, layer),
                _resident((1, D_MODEL))]
    args = [x, g.reshape(1, D_MODEL), wgu, wd, gf.reshape(1, D_MODEL)]
    out_specs = [pl.BlockSpec((bm, D_MODEL), lambda i: (i, 0))]
    out_shape = [jax.ShapeDtypeStruct((m, D_MODEL), F32)]
    scratch = [pltpu.VMEM((bm, D_MODEL), BF16), pltpu.VMEM((bm, D_FF), BF16)]
    kernel_hosted = None
    if hosted is not None:
        qkv, views, biases = hosted
        n = qkv[0].shape[2]
        assert n % steps == 0 and n // steps <= FF_BLOCKS, (n, steps)
        kernel_hosted = (layer, n // steps)
        in_specs += ([_resident(a.shape) for a in qkv]
                     + [pl.BlockSpec(memory_space=pl.ANY) for _ in views]
                     + [_resident(b.shape) for b in biases])
        args += [*qkv, *views, *biases]
        assert n % LANES == 0, n
        out_specs += [pl.BlockSpec((n, ATT_OUT), lambda i: (0, 0)),
                      pl.BlockSpec((n // LANES, ATT_OUT, LANES), lambda i: (0, 0, 0))]
        out_shape += [jax.ShapeDtypeStruct((n, ATT_OUT), F32),
                      jax.ShapeDtypeStruct((n // LANES, ATT_OUT, LANES), F32)]
        scratch += [pltpu.VMEM((CACHE_SLOTS,) + v.shape[2:], F32) for v in views]
        scratch.append(pltpu.SemaphoreType.DMA((N_GROUPS, CACHE_SLOTS)))
    out = pl.pallas_call(
        functools.partial(_ffn_kernel, final=final, hosted=kernel_hosted),
        grid=(steps,),
        in_specs=in_specs,
        out_specs=out_specs,
        out_shape=out_shape,
        scratch_shapes=scratch,
        compiler_params=_params(1),
        name="ffn",
    )(*args)
    return out if hosted is not None else out[0]


def _prep_weights(w_in, w_ret_branch, w_att_branch, w_out, w_gate_up, w_down):
    return tuple(w.astype(BF16) for w in (w_in, w_ret_branch, w_att_branch, w_out, w_gate_up, w_down))


def _new_kv_rows(qkv):
    n = qkv.shape[2]
    return jnp.stack([qkv[1, 0], qkv[2, 0]], axis=1).reshape(n, 1, 2, HEADS_PER_GROUP, HEAD_DIM)


def _block_rows(m, target):
    bm = min(m, target)
    assert m % bm == 0, (m, bm)
    return bm


def kernel(x_prompt, x_sample, state_ret, cache_kv_w128, cache_kv_w512, cache_kv_w2048, norm_mix,
           w_in, w_ret_branch, w_att_branch, w_out, norm_ffn, w_gate_up, w_down, norm_final):
    depth = w_in.shape[0]
    batch, seq, _ = x_prompt.shape
    n_dec, dec_seq, _ = x_sample.shape
    assert dec_seq == 1 and seq % (ATT_BLOCK * ATT_GROUPS[-1][1]) == 0 and seq % RET_CHUNK == 0
    wi, wr, wa, wo, wgu, wd = _prep_weights(w_in, w_ret_branch, w_att_branch, w_out,
                                            w_gate_up, w_down)
    cache_views, cache_biases = _sample_cache_views((cache_kv_w128, cache_kv_w512, cache_kv_w2048))

    m = batch * seq
    bm = _block_rows(m, 512)
    bs = _block_rows(n_dec, 512)
    xp = x_prompt.reshape(m, D_MODEL)
    xs = x_sample.reshape(n_dec, D_MODEL)
    p_ret, p_kv = [], [[] for _ in ATT_GROUPS]
    s_ret, s_kv = None, [[] for _ in ATT_GROUPS]
    for l in range(depth):
        last = l == depth - 1
        proj_s, *qkv_s = _in_proj(xs, norm_mix[l], wi, l, bs, F32, (1,) * N_GROUPS)
        for g in range(N_GROUPS):
            s_kv[g].append(_new_kv_rows(qkv_s[g]))
        keeps = tuple(min(window, seq) for window, _ in ATT_GROUPS)
        proj, *rest = _in_proj(xp, norm_mix[l], wi, l, bm, BF16, DILATIONS, tails=(batch, keeps),
                               hosted=(proj_s, state_ret, s_ret))
        qkv, kv_tails, (ro_s, s_ret) = rest[:N_GROUPS], rest[N_GROUPS:2 * N_GROUPS], rest[2 * N_GROUPS:]
        ro, s_fin = _ret_prompt(proj.reshape(N_MAIN_BLOCKS, batch, seq, COL))
        p_ret.append(s_fin)
        outs, lses = [], []
        for g in range(N_GROUPS):
            o, lse = _attn_prompt(qkv[g], g, batch)
            outs.append(o), lses.append(lse)
            p_kv[g].append(kv_tails[g].reshape(batch, 2, HEADS_PER_GROUP, HEAD_DIM, keeps[g])
                           .transpose(0, 4, 1, 2, 3))
        xp = _mix_out(outs + lses, ro.reshape(4, m, COL), proj, xp, wr, wa, wo, l,
                      _block_rows(m, 1024))
        xp, *ao_s = _ffn(xp, norm_ffn[l], wgu, wd, l, norm_final, bm, last,
                         hosted=(qkv_s, cache_views, cache_biases))
        xs = _mix_out(ao_s, ro_s, proj_s, xs, wr, wa, wo, l, bs)
        xs = _ffn(xs, norm_ffn[l], wgu, wd, l, norm_final, bs, last)
    y_prompt = xp.reshape(batch, seq, D_MODEL)
    y_sample = xs.reshape(n_dec, 1, D_MODEL)

    return (y_prompt, y_sample, jnp.stack(p_ret),
            jnp.stack(p_kv[0]), jnp.stack(p_kv[1]), jnp.stack(p_kv[2]),
            s_ret, jnp.stack(s_kv[0]), jnp.stack(s_kv[1]), jnp.stack(s_kv[2]))
```
